```python
import math
import jax
import jax.numpy as jnp
from jax import lax
import numpy as np

D_MODEL = 1024
BATCH = 4
SEQ = 4096
DEPTH = 4
DEC_BATCH = 32
DEC_SEQ = 1
PAST_LEN = 8192
PAGE_SIZE = 128

HEAD_DIM = 64
A_HEADS = 6
B_GROUPS = 4
C_HEADS = 6
C_KV = 2
C_REP = C_HEADS // C_KV
A_W = A_HEADS * HEAD_DIM
B_W = B_GROUPS * HEAD_DIM
C_W = C_HEADS * HEAD_DIM
KV_W = C_KV * HEAD_DIM
MIX_W = A_W + B_W + C_W
Q_BLOCK = 128
CHUNK = 128
CMP_LEN = 32
CMP_STRIDE = 16
CMP_HIDDEN = 256
SLC_BLOCK = 64
SLC_TOPN = 16
WINDOW = 512
NUM_BUCKETS = 32
MAX_DISTANCE = 128
D_FF = -((-8 * D_MODEL) // (3 * 256)) * 256
EPS = 1e-6
NEG = -1e30
FORCE = 1e4
PROJ_WIDTHS = (A_W, A_W, A_W, A_HEADS, B_W, B_W, C_W, KV_W, KV_W, KV_W, KV_W, KV_W, KV_W, 3 * C_HEADS)
IN_COLS = sum(PROJ_WIDTHS)
PROJ_SPLITS = [int(s) for s in np.cumsum(PROJ_WIDTHS)[:-1]]

kernel_name = 'hybrid_fox_gmlp_nsa_decoder_step'


def rmsnorm(x, g):
    xf = x.astype(jnp.float32)
    y = xf * lax.rsqrt(jnp.mean(xf * xf, axis=-1, keepdims=True) + EPS)
    return (y * g).astype(x.dtype)


def layernorm(x, g, b):
    xf = x.astype(jnp.float32)
    mu = jnp.mean(xf, axis=-1, keepdims=True)
    var = jnp.mean(jnp.square(xf - mu), axis=-1, keepdims=True)
    return ((xf - mu) * lax.rsqrt(var + EPS) * g + b).astype(x.dtype)


def t5_bucket(dist):
    n = jnp.maximum(dist, 0)
    max_exact = NUM_BUCKETS // 2
    nf = jnp.maximum(n, 1).astype(jnp.float32)
    large = max_exact + (jnp.log(nf / max_exact) / math.log(MAX_DISTANCE / max_exact)
                         * (NUM_BUCKETS - max_exact)).astype(jnp.int32)
    return jnp.where(n < max_exact, n, jnp.minimum(large, NUM_BUCKETS - 1))


def gqa_attend(q, k, v, bias, mask):
    s = jnp.einsum('btgrd,bsgd->bgrts', q, k).astype(jnp.float32) * (HEAD_DIM ** -0.5) + bias
    p = jax.nn.softmax(jnp.where(mask, s, NEG), axis=-1)
    return jnp.einsum('bgrts,bsgd->btgrd', p.astype(v.dtype), v)


def gather_pages(pool, page_table):
    g = pool[page_table]
    return g.reshape(g.shape[0], g.shape[1] * g.shape[2], *g.shape[3:])


def split_projection(h, w_in, b_forget):
    B, T = h.shape[:2]
    parts = jnp.split(h @ w_in, PROJ_SPLITS, axis=-1)
    a_q, a_k, a_v, a_f, b_u, b_v, c_q, c_kc, c_vc, c_ks, c_vs, c_kw, c_vw, c_g = parts
    hd = lambda t, n: t.reshape(B, T, n, HEAD_DIM)
    return {
        'a_q': hd(a_q, A_HEADS), 'a_k': hd(a_k, A_HEADS), 'a_v': hd(a_v, A_HEADS),
        'a_lf': jax.nn.log_sigmoid((a_f + b_forget).astype(jnp.float32)),
        'b_u': jax.nn.gelu(b_u), 'b_v': jax.nn.gelu(b_v),
        'c_q': c_q.reshape(B, T, C_KV, C_REP, HEAD_DIM),
        'c_kc': hd(c_kc, C_KV), 'c_vc': hd(c_vc, C_KV),
        'c_ks': hd(c_ks, C_KV), 'c_vs': hd(c_vs, C_KV),
        'c_kw': hd(c_kw, C_KV), 'c_vw': hd(c_vw, C_KV),
        'c_g': jax.nn.sigmoid(c_g).reshape(B, T, 3, C_KV, C_REP),
    }


def fox_attend(q, c_q, q_pos, k, v, c_k, k_pos):
    bias = (jnp.swapaxes(c_q, 1, 2)[:, :, None, :, None] - jnp.swapaxes(c_k, 1, 2)[:, :, None, None, :])
    mask = k_pos[None, :] <= q_pos[:, None]
    return gqa_attend(q[:, :, :, None], k, v, bias, mask)[:, :, :, 0]


def fox_prompt(q, k, v, lf):
    B, S, H, D = q.shape
    c = jnp.cumsum(lf, axis=1)
    pos = jnp.arange(S)
    nblk = S // Q_BLOCK
    qb = q.reshape(B, nblk, Q_BLOCK, H, D).swapaxes(0, 1)
    cb = c.reshape(B, nblk, Q_BLOCK, H).swapaxes(0, 1)
    pb = pos.reshape(nblk, Q_BLOCK)
    out = lax.map(lambda a: fox_attend(a[0], a[1], a[2], k, v, c, pos), (qb, cb, pb))
    return out.swapaxes(0, 1).reshape(B, S, H * D)


def chunk_mix(u, v, ws, bs):
    B, T = v.shape[:2]
    nch = -(-T // CHUNK)
    vp = jnp.pad(v, ((0, 0), (0, nch * CHUNK - T), (0, 0), (0, 0))).reshape(B, nch, CHUNK, B_GROUPS, HEAD_DIM)
    z = jnp.einsum('hts,bnshd->bnthd', jnp.tril(ws), vp) + bs.T[None, None, :, :, None]
    return u * z.reshape(B, nch * CHUNK, B_GROUPS, HEAD_DIM)[:, :T]


def gmlp_mixer(u, v, p):
    B, T = v.shape[:2]
    vn = layernorm(v, p['gmlp_ln_g'], p['gmlp_ln_b']).reshape(B, T, B_GROUPS, HEAD_DIM)
    out = chunk_mix(u.reshape(B, T, B_GROUPS, HEAD_DIM), vn, p['gmlp_ws'], p['gmlp_bs'])
    return out.reshape(B, T, B_W), vn


def compress(rows, p, i):
    B, L, G, D = rows.shape
    nb = (L - CMP_LEN) // CMP_STRIDE + 1
    idx = np.arange(nb)[:, None] * CMP_STRIDE + np.arange(CMP_LEN)[None, :]
    blk = rows[:, idx] + p['cmp_pe'][i][:, None, :]
    flat = blk.transpose(0, 1, 3, 2, 4).reshape(B, nb, G, CMP_LEN * D)
    hid = jax.nn.gelu(flat @ p['cmp_w1'][i] + p['cmp_b1'][i])
    return hid @ p['cmp_w2'][i] + p['cmp_b2'][i]


def cmp_attend(q, q_pos, kc, vc):
    nb = kc.shape[1]
    blk_end = np.arange(nb) * CMP_STRIDE + CMP_LEN - 1
    valid = blk_end[None, :] <= q_pos[:, None]
    s = jnp.einsum('btgrd,bngd->bgrtn', q, kc).astype(jnp.float32) * (HEAD_DIM ** -0.5)
    prob = jax.nn.softmax(jnp.where(valid, s, NEG), axis=-1) * valid
    out = jnp.einsum('bgrtn,bngd->btgrd', prob.astype(vc.dtype), vc)
    return out, prob


def select_blocks(prob, q_pos, L):
    nsb = -(-L // SLC_BLOCK)
    per = SLC_BLOCK // CMP_STRIDE
    imp = prob.sum(axis=2)
    imp = jnp.pad(imp, ((0, 0), (0, 0), (0, 0), (0, nsb * per - imp.shape[-1])))
    imp = imp.reshape(*imp.shape[:-1], nsb, per).sum(-1)
    j = jnp.arange(nsb)[None, :]
    cur = (q_pos // SLC_BLOCK)[:, None]
    forced = (j == 0) | (j == cur) | (j == cur - 1)
    valid = j * SLC_BLOCK <= q_pos[:, None]
    score = jnp.where(forced, FORCE, jnp.where(valid, imp, -1.0))
    _, idx = lax.top_k(score, min(SLC_TOPN, nsb))
    return idx


def slc_attend(q, q_pos, k_rows, v_rows, idx, table):
    B, L, G, D = k_rows.shape
    nsb = -(-L // SLC_BLOCK)
    pad = ((0, 0), (0, nsb * SLC_BLOCK - L), (0, 0), (0, 0))
    kb = jnp.pad(k_rows, pad).reshape(B, nsb, SLC_BLOCK, G, D).transpose(0, 3, 1, 2, 4)
    vb = jnp.pad(v_rows, pad).reshape(B, nsb, SLC_BLOCK, G, D).transpose(0, 3, 1, 2, 4)
    gather = jax.vmap(jax.vmap(lambda a, i: a[i]))
    T, N = idx.shape[2], idx.shape[3]
    kg = gather(kb, idx).reshape(B, G, T, N * SLC_BLOCK, D)
    vg = gather(vb, idx).reshape(B, G, T, N * SLC_BLOCK, D)
    k_pos = (idx[..., None] * SLC_BLOCK + jnp.arange(SLC_BLOCK)).reshape(B, G, T, N * SLC_BLOCK)
    dist = q_pos[None, None, :, None] - k_pos
    tbl = table.reshape(NUM_BUCKETS, G, C_REP).transpose(1, 0, 2).astype(jnp.float32)
    bias = tbl[jnp.arange(G)[None, :, None, None], t5_bucket(dist)].transpose(0, 1, 4, 2, 3)
    s = jnp.einsum('btgrd,bgtkd->bgrtk', q, kg).astype(jnp.float32) * (HEAD_DIM ** -0.5) + bias
    prob = jax.nn.softmax(jnp.where((dist >= 0)[:, :, None], s, NEG), axis=-1)
    return jnp.einsum('bgrtk,bgtkd->btgrd', prob.astype(vg.dtype), vg)


def slc_prompt(q, pos, ks, vs, idx, table):
    B, S, G, R, D = q.shape
    nblk = S // Q_BLOCK
    qb = q.reshape(B, nblk, Q_BLOCK, G, R, D).swapaxes(0, 1)
    ib = idx.reshape(B, G, nblk, Q_BLOCK, idx.shape[-1]).transpose(2, 0, 1, 3, 4)
    pb = pos.reshape(nblk, Q_BLOCK)
    out = lax.map(lambda a: slc_attend(a[0], a[2], ks, vs, a[1], table), (qb, ib, pb))
    return out.swapaxes(0, 1).reshape(B, S, G, R, D)


def window_attend(q, q_pos, k, v, k_pos, table):
    dist = q_pos[:, None] - k_pos[None, :]
    mask = (dist >= 0) & (dist <= WINDOW) & (k_pos[None, :] >= 0)
    bias = table.reshape(NUM_BUCKETS, C_KV, C_REP)[t5_bucket(dist)].transpose(2, 3, 0, 1).astype(jnp.float32)
    return gqa_attend(q, k, v, bias, mask)


def window_prompt(q, k, v, table):
    B, S, G, R, D = q.shape
    kp = jnp.pad(k, ((0, 0), (WINDOW, 0), (0, 0), (0, 0)))
    vp = jnp.pad(v, ((0, 0), (WINDOW, 0), (0, 0), (0, 0)))
    nblk = S // Q_BLOCK
    qb = q.reshape(B, nblk, Q_BLOCK, G, R, D).swapaxes(0, 1)
    starts = jnp.arange(nblk) * Q_BLOCK

    def blk(a):
        qq, st = a
        kk = lax.dynamic_slice_in_dim(kp, st, WINDOW + Q_BLOCK, axis=1)
        vv = lax.dynamic_slice_in_dim(vp, st, WINDOW + Q_BLOCK, axis=1)
        return window_attend(qq, st + jnp.arange(Q_BLOCK), kk, vv, st - WINDOW + jnp.arange(WINDOW + Q_BLOCK), table)

    out = lax.map(blk, (qb, starts))
    return out.swapaxes(0, 1).reshape(B, S, G, R, D)


def combine_nsa(g, o_cmp, o_slc, o_win):
    B, T = g.shape[:2]
    out = g[:, :, 0, :, :, None] * o_cmp + g[:, :, 1, :, :, None] * o_slc + g[:, :, 2, :, :, None] * o_win
    return out.reshape(B, T, C_W)


def finish_layer(x, a_out, b_out, c_out, p):
    mix = jnp.concatenate([rmsnorm(a_out, p['norm_group_a']), rmsnorm(b_out, p['norm_group_b']),
                           rmsnorm(c_out, p['norm_group_c'])], axis=-1)
    x = x + rmsnorm(mix @ p['w_o'], p['norm_post_mix'])
    h = rmsnorm(x, p['norm_pre_ffn'])
    gate, up = jnp.split(h @ p['w_ffn_in'], 2, axis=-1)
    return x + rmsnorm((jax.nn.silu(gate) * up) @ p['w_ffn_out'], p['norm_post_ffn'])


def layer_prompt(x, p, table):
    B, S, _ = x.shape
    z = split_projection(rmsnorm(x, p['norm_pre_mix']), p['w_in'], p['b_forget'])
    pos = jnp.arange(S)
    a_out = fox_prompt(z['a_q'], z['a_k'], z['a_v'], z['a_lf'])
    b_out, _ = gmlp_mixer(z['b_u'], z['b_v'], p)
    kc = compress(z['c_kc'], p, 0)
    vc = compress(z['c_vc'], p, 1)
    o_cmp, prob = cmp_attend(z['c_q'], pos, kc, vc)
    idx = select_blocks(prob, pos, S)
    o_slc = slc_prompt(z['c_q'], pos, z['c_ks'], z['c_vs'], idx, table)
    o_win = window_prompt(z['c_q'], z['c_kw'], z['c_vw'], table)
    c_out = combine_nsa(z['c_g'], o_cmp, o_slc, o_win)
    x = finish_layer(x, a_out, b_out, c_out, p)
    wb = min(WINDOW, S)
    states = (jnp.stack([z['a_k'], z['a_v']], axis=2), z['a_lf'],
              jnp.stack([z['c_kc'], z['c_vc']], axis=2), jnp.stack([z['c_ks'], z['c_vs']], axis=2),
              jnp.stack([z['c_kw'], z['c_vw']], axis=2)[:, S - wb:])
    return x, states


def layer_sample(x, p, table, fox_kv_pool, fox_lf_pool, cmp_pool, slc_pool, win_buf, page_table):
    B, T, _ = x.shape
    past = page_table.shape[1] * fox_kv_pool.shape[1]
    L = past + T
    z = split_projection(rmsnorm(x, p['norm_pre_mix']), p['w_in'], p['b_forget'])
    q_pos = past + jnp.arange(T)
    kv_past = gather_pages(fox_kv_pool, page_table)
    lf_past = gather_pages(fox_lf_pool, page_table).astype(jnp.float32)
    k_all = jnp.concatenate([kv_past[:, :, 0], z['a_k']], axis=1)
    v_all = jnp.concatenate([kv_past[:, :, 1], z['a_v']], axis=1)
    c = jnp.cumsum(jnp.concatenate([lf_past, z['a_lf']], axis=1), axis=1)
    a_out = fox_attend(z['a_q'], c[:, past:], q_pos, k_all, v_all, c, jnp.arange(L)).reshape(B, T, A_W)
    b_out, v_rows = gmlp_mixer(z['b_u'], z['b_v'], p)
    new_cmp = jnp.stack([z['c_kc'], z['c_vc']], axis=2)
    new_slc = jnp.stack([z['c_ks'], z['c_vs']], axis=2)
    new_win = jnp.stack([z['c_kw'], z['c_vw']], axis=2)
    cmp_rows = jnp.concatenate([gather_pages(cmp_pool, page_table), new_cmp], axis=1)
    kc = compress(cmp_rows[:, :, 0], p, 0)
    vc = compress(cmp_rows[:, :, 1], p, 1)
    o_cmp, prob = cmp_attend(z['c_q'], q_pos, kc, vc)
    idx = select_blocks(prob, q_pos, L)
    slc_rows = jnp.concatenate([gather_pages(slc_pool, page_table), new_slc], axis=1)
    o_slc = slc_attend(z['c_q'], q_pos, slc_rows[:, :, 0], slc_rows[:, :, 1], idx, table)
    wb = win_buf.shape[1]
    win_rows = jnp.concatenate([win_buf, new_win], axis=1)
    o_win = window_attend(z['c_q'], q_pos, win_rows[:, :, 0], win_rows[:, :, 1],
                          past - wb + jnp.arange(wb + T), table)
    c_out = combine_nsa(z['c_g'], o_cmp, o_slc, o_win)
    x = finish_layer(x, a_out, b_out, c_out, p)
    states = (jnp.stack([z['a_k'], z['a_v']], axis=2), z['a_lf'], new_cmp, new_slc,
              win_rows[:, T:], v_rows)
    return x, states


def setup_inputs(seed: int = 0) -> dict:
    key = jax.random.key(seed)
    ks = jax.random.split(key, 32)
    f32 = jnp.float32
    nrm = lambda k, shape, scale: jax.random.normal(k, shape, f32) * scale
    gain = lambda k, shape: 1.0 + 0.05 * jax.random.normal(k, shape, f32)
    n_pages = PAST_LEN // PAGE_SIZE
    n_used = DEC_BATCH * n_pages
    n_pool = n_used + max(1, n_used // 4)
    win_buf = min(WINDOW, PAST_LEN)
    page_table = jax.random.permutation(ks[0], n_pool)[:n_used].reshape(DEC_BATCH, n_pages).astype(jnp.int32)
    return {
        'x_prompt': nrm(ks[1], (BATCH, SEQ, D_MODEL), 1.0),
        'x_sample': nrm(ks[2], (DEC_BATCH, DEC_SEQ, D_MODEL), 1.0),
        'cache_fox_kv': nrm(ks[3], (DEPTH, n_pool, PAGE_SIZE, 2, A_HEADS, HEAD_DIM), 1.0),
        'cache_fox_logf': jax.nn.log_sigmoid(3.0 + jax.random.normal(ks[4], (DEPTH, n_pool, PAGE_SIZE, A_HEADS), f32)),
        'cache_cmp_kv': nrm(ks[5], (DEPTH, n_pool, PAGE_SIZE, 2, C_KV, HEAD_DIM), 1.0),
        'cache_slc_kv': nrm(ks[6], (DEPTH, n_pool, PAGE_SIZE, 2, C_KV, HEAD_DIM), 1.0),
        'state_win_kv': nrm(ks[7], (DEPTH, DEC_BATCH, win_buf, 2, C_KV, HEAD_DIM), 1.0),
        'page_table': page_table,
        'rel_bias_table': nrm(ks[8], (NUM_BUCKETS, C_HEADS), 0.5),
        'norm_pre_mix': gain(ks[9], (DEPTH, D_MODEL)),
        'w_in': nrm(ks[10], (DEPTH, D_MODEL, IN_COLS), D_MODEL ** -0.5),
        'b_forget': 3.0 + nrm(ks[11], (DEPTH, A_HEADS), 0.5),
        'gmlp_ln_g': gain(ks[12], (DEPTH, B_W)),
        'gmlp_ln_b': nrm(ks[13], (DEPTH, B_W), 0.02),
        'gmlp_ws': nrm(ks[14], (DEPTH, B_GROUPS, CHUNK, CHUNK), 0.5 * CHUNK ** -0.5),
        'gmlp_bs': 1.0 + nrm(ks[15], (DEPTH, B_GROUPS, CHUNK), 0.1),
        'cmp_pe': nrm(ks[16], (DEPTH, 2, CMP_LEN, HEAD_DIM), 0.1),
        'cmp_w1': nrm(ks[17], (DEPTH, 2, CMP_LEN * HEAD_DIM, CMP_HIDDEN), (CMP_LEN * HEAD_DIM) ** -0.5),
        'cmp_b1': nrm(ks[18], (DEPTH, 2, CMP_HIDDEN), 0.02),
        'cmp_w2': nrm(ks[19], (DEPTH, 2, CMP_HIDDEN, HEAD_DIM), CMP_HIDDEN ** -0.5),
        'cmp_b2': nrm(ks[20], (DEPTH, 2, HEAD_DIM), 0.02),
        'norm_group_a': gain(ks[21], (DEPTH, A_W)),
        'norm_group_b': gain(ks[22], (DEPTH, B_W)),
        'norm_group_c': gain(ks[23], (DEPTH, C_W)),
        'w_o': nrm(ks[24], (DEPTH, MIX_W, D_MODEL), MIX_W ** -0.5),
        'norm_post_mix': gain(ks[25], (DEPTH, D_MODEL)),
        'norm_pre_ffn': gain(ks[26], (DEPTH, D_MODEL)),
        'w_ffn_in': nrm(ks[27], (DEPTH, D_MODEL, 2 * D_FF), D_MODEL ** -0.5),
        'w_ffn_out': nrm(ks[28], (DEPTH, D_FF, D_MODEL), D_FF ** -0.5),
        'norm_post_ffn': gain(ks[29], (DEPTH, D_MODEL)),
    }


def reference(x_prompt, x_sample, cache_fox_kv, cache_fox_logf, cache_cmp_kv, cache_slc_kv, state_win_kv,
              page_table, rel_bias_table, norm_pre_mix, w_in, b_forget, gmlp_ln_g, gmlp_ln_b, gmlp_ws, gmlp_bs,
              cmp_pe, cmp_w1, cmp_b1, cmp_w2, cmp_b2, norm_group_a, norm_group_b, norm_group_c, w_o,
              norm_post_mix, norm_pre_ffn, w_ffn_in, w_ffn_out, norm_post_ffn):
    xp, xs = x_prompt, x_sample
    sp = [[] for _ in range(5)]
    ss = [[] for _ in range(6)]
    for l in range(DEPTH):
        p = {
            'norm_pre_mix': norm_pre_mix[l], 'w_in': w_in[l], 'b_forget': b_forget[l],
            'gmlp_ln_g': gmlp_ln_g[l], 'gmlp_ln_b': gmlp_ln_b[l], 'gmlp_ws': gmlp_ws[l], 'gmlp_bs': gmlp_bs[l],
            'cmp_pe': cmp_pe[l], 'cmp_w1': cmp_w1[l], 'cmp_b1': cmp_b1[l], 'cmp_w2': cmp_w2[l], 'cmp_b2': cmp_b2[l],
            'norm_group_a': norm_group_a[l], 'norm_group_b': norm_group_b[l], 'norm_group_c': norm_group_c[l],
            'w_o': w_o[l], 'norm_post_mix': norm_post_mix[l], 'norm_pre_ffn': norm_pre_ffn[l],
            'w_ffn_in': w_ffn_in[l], 'w_ffn_out': w_ffn_out[l], 'norm_post_ffn': norm_post_ffn[l],
        }
        xp, st_p = layer_prompt(xp, p, rel_bias_table)
        xs, st_s = layer_sample(xs, p, rel_bias_table, cache_fox_kv[l], cache_fox_logf[l], cache_cmp_kv[l],
                                cache_slc_kv[l], state_win_kv[l], page_table)
        for lst, s in zip(sp, st_p):
            lst.append(s)
        for lst, s in zip(ss, st_s):
            lst.append(s)
    fox_kv_prompt = jnp.stack(sp[0])
    fox_logf_prompt = jnp.stack(sp[1])
    cmp_kv_prompt = jnp.stack(sp[2])
    slc_kv_prompt = jnp.stack(sp[3])
    win_kv_prompt = jnp.stack(sp[4])
    fox_kv_sample = jnp.stack(ss[0])
    fox_logf_sample = jnp.stack(ss[1])
    cmp_kv_sample = jnp.stack(ss[2])
    slc_kv_sample = jnp.stack(ss[3])
    win_kv_sample = jnp.stack(ss[4])
    gmlp_v_sample = jnp.stack(ss[5])
    return (xp, xs, fox_kv_prompt, fox_logf_prompt, cmp_kv_prompt, slc_kv_prompt, win_kv_prompt,
            fox_kv_sample, fox_logf_sample, cmp_kv_sample, slc_kv_sample, win_kv_sample, gmlp_v_sample)
```

```python
import functools
import math

import numpy as np
import jax
import jax.numpy as jnp
from jax import lax
from jax.experimental import pallas as pl
from jax.experimental.pallas import tpu as pltpu

F32 = jnp.float32
BF16 = jnp.bfloat16

D_MODEL = 1024
HEAD_DIM = 64
A_HEADS = 6
B_GROUPS = 4
C_HEADS = 6
C_KV = 2
C_REP = C_HEADS // C_KV
A_W = A_HEADS * HEAD_DIM
B_W = B_GROUPS * HEAD_DIM
C_W = C_HEADS * HEAD_DIM
KV_W = C_KV * HEAD_DIM
Q_BLOCK = 128
CHUNK = 128
CMP_LEN = 32
CMP_STRIDE = 16
CMP_HIDDEN = 256
SLC_BLOCK = 64
SLC_TOPN = 16
WINDOW = 512
NUM_BUCKETS = 32
MAX_DISTANCE = 128
D_FF = 2816
EPS = 1e-6
NEG = -1e30
FORCE = 1e4
SCALE = HEAD_DIM ** -0.5
PROJ_WIDTHS = (A_W, A_W, A_W, A_HEADS, B_W, B_W, C_W, KV_W, KV_W, KV_W, KV_W, KV_W, KV_W, 3 * C_HEADS)

LANES = 128
VMEM_LIMIT = 56 * 1024 * 1024

P_QA = (0, 384)
P_KVA = (384, 1152)
P_BU = (1152, 1408)
P_BV = (1408, 1664)
P_QC = (1664, 2048)
P_CMP = (2048, 2304)
P_SLC = (2304, 2560)
P_WIN = (2560, 2816)
P_MISC = (2816, 2944)
N_PROJ = 2944
GATE_OFF = A_HEADS


def _dot(a, b):
    return jnp.dot(a, b, preferred_element_type=F32)


def _dot_nt(a, b):
    return lax.dot_general(a, b, (((1,), (1,)), ((), ())), preferred_element_type=F32)


def _gelu(x):
    return 0.5 * x * (1.0 + jnp.tanh(math.sqrt(2.0 / math.pi) * (x + 0.044715 * (x * x * x))))


def _sigmoid(x):
    return 1.0 / (1.0 + jnp.exp(-x))


def _rms(x, g):
    return x * lax.rsqrt(jnp.mean(x * x, axis=-1, keepdims=True) + EPS) * g


def _iota(shape, dim):
    return lax.broadcasted_iota(jnp.int32, shape, dim)


def _shr(x, n):
    return jnp.right_shift(x, int(math.log2(n)))


def _cparams(sem):
    return pltpu.CompilerParams(dimension_semantics=sem, vmem_limit_bytes=VMEM_LIMIT)


def _const_spec(shape):
    nd = len(shape)
    return pl.BlockSpec(shape, lambda *_: (0,) * nd)


def _proj_kernel(x_ref, g_ref, w_ref, bf_ref, lng_ref, lnb_ref,
                 qa_ref, kva_ref, kvab_ref, bu_ref, vn_ref, qc_ref,
                 cmp_ref, slc_ref, win_ref, kvcb_ref, misc_ref):
    h = _rms(x_ref[...], g_ref[...]).astype(BF16)

    def mm(seg):
        return _dot(h, w_ref[:, seg[0]:seg[1]])

    qa_ref[...] = (mm(P_QA) * SCALE).astype(BF16)
    kva = mm(P_KVA)
    kva_ref[...] = kva
    kvab_ref[...] = kva.astype(BF16)
    bu_ref[...] = _gelu(mm(P_BU))
    v = _gelu(mm(P_BV))
    mu = jnp.mean(v, axis=-1, keepdims=True)
    var = jnp.mean(jnp.square(v - mu), axis=-1, keepdims=True)
    vn_ref[...] = (v - mu) * lax.rsqrt(var + EPS) * lng_ref[...] + lnb_ref[...]
    qc_ref[...] = (mm(P_QC) * SCALE).astype(BF16)
    c = mm(P_CMP)
    cmp_ref[...] = c
    kvcb_ref[:, 0:256] = c.astype(BF16)
    c = mm(P_SLC)
    slc_ref[...] = c
    kvcb_ref[:, 256:512] = c.astype(BF16)
    c = mm(P_WIN)
    win_ref[...] = c
    kvcb_ref[:, 512:768] = c.astype(BF16)
    m = mm(P_MISC) + bf_ref[...]
    lane = _iota(m.shape, 1)
    logsig = jnp.minimum(m, 0.0) - jnp.log1p(jnp.exp(-jnp.abs(m)))
    misc_ref[...] = jnp.where(lane < A_HEADS, logsig, _sigmoid(m))


def _proj(x2d, g, w, bf, lng, lnb, tm):
    R = x2d.shape[0]
    row = lambda n: pl.BlockSpec((tm, n), lambda i: (i, 0))
    outs = [(384, BF16), (768, F32), (768, BF16), (256, F32), (256, F32), (384, BF16),
            (256, F32), (256, F32), (256, F32), (768, BF16), (128, F32)]
    return pl.pallas_call(
        _proj_kernel,
        grid=(R // tm,),
        in_specs=[row(D_MODEL), _const_spec((1, D_MODEL)), _const_spec((D_MODEL, N_PROJ)),
                  _const_spec((1, 128)), _const_spec((1, B_W)), _const_spec((1, B_W))],
        out_specs=[row(n) for n, _ in outs],
        out_shape=[jax.ShapeDtypeStruct((R, n), dt) for n, dt in outs],
        compiler_params=_cparams(("parallel",)),
        name="proj",
    )(x2d, g, w, bf, lng, lnb)


def _fox_kernel(q_ref, k_ref, v_ref, c_ref, o_ref, m_sc, l_sc, acc_sc, *, tq, tk):
    qi = pl.program_id(2)
    ki = pl.program_id(3)

    @pl.when(ki == 0)
    def _():
        m_sc[...] = jnp.full(m_sc.shape, NEG, F32)
        l_sc[...] = jnp.zeros(l_sc.shape, F32)
        acc_sc[...] = jnp.zeros(acc_sc.shape, F32)

    @pl.when(ki <= qi)
    def _():
        q = q_ref[...]
        k = k_ref[...]
        v = v_ref[...]
        lane = _iota((1, LANES), 1)
        rows = qi * tq + _iota((tq, 1), 0)
        cols = ki * tk + _iota((1, tk), 1)
        causal = cols <= rows
        for hh in range(2):
            mine = lane < HEAD_DIM if hh == 0 else lane >= HEAD_DIM
            qm = jnp.where(mine, q, jnp.zeros_like(q))
            s = _dot_nt(qm, k) - c_ref[hh:hh + 1, :]
            s = jnp.where(causal, s, NEG)
            m_prev = m_sc[hh]
            m_new = jnp.maximum(m_prev, jnp.max(s, axis=-1, keepdims=True))
            alpha = jnp.exp(m_prev - m_new)
            p = jnp.exp(s - m_new)
            l_sc[hh] = alpha * l_sc[hh] + jnp.sum(p, axis=-1, keepdims=True)
            acc_sc[hh] = alpha * acc_sc[hh] + _dot(p.astype(BF16), v)
            m_sc[hh] = m_new

    @pl.when(ki == qi)
    def _():
        lane = _iota((1, LANES), 1)
        o_ref[...] = jnp.where(lane < HEAD_DIM, acc_sc[0] / l_sc[0], acc_sc[1] / l_sc[1])


def _fox_prompt(qa, kvab, crow, tq):
    B, S, _ = qa.shape
    nq = S // tq
    kern = functools.partial(_fox_kernel, tq=tq, tk=tq)
    return pl.pallas_call(
        kern,
        grid=(B, A_HEADS // 2, nq, nq),
        in_specs=[
            pl.BlockSpec((None, tq, LANES), lambda b, p, qi, ki: (b, qi, p)),
            pl.BlockSpec((None, tq, LANES), lambda b, p, qi, ki: (b, jnp.minimum(ki, qi), p)),
            pl.BlockSpec((None, tq, LANES), lambda b, p, qi, ki: (b, jnp.minimum(ki, qi), A_HEADS // 2 + p)),
            pl.BlockSpec((None, None, 2, tq), lambda b, p, qi, ki: (b, p, 0, jnp.minimum(ki, qi))),
        ],
        out_specs=pl.BlockSpec((None, tq, LANES), lambda b, p, qi, ki: (b, qi, p)),
        out_shape=jax.ShapeDtypeStruct((B, S, A_W), F32),
        scratch_shapes=[pltpu.VMEM((2, tq, 1), F32), pltpu.VMEM((2, tq, 1), F32), pltpu.VMEM((2, tq, LANES), F32)],
        compiler_params=_cparams(("parallel", "parallel", "parallel", "arbitrary")),
        name="fox_prompt",
    )(qa, kvab, kvab, crow)


def _gmlp_kernel(u_ref, vn_ref, ws_ref, bst_ref, o_ref, *, nchunk):
    tril = _iota((CHUNK, CHUNK), 1) <= _iota((CHUNK, CHUNK), 0)
    lane = _iota((1, B_W), 1)
    ws = [jnp.where(tril, ws_ref[h], 0.0).astype(BF16) for h in range(B_GROUPS)]
    for c in range(nchunk):
        rows = slice(c * CHUNK, (c + 1) * CHUNK)
        vn = vn_ref[rows, :].astype(BF16)
        z = jnp.zeros((CHUNK, B_W), F32)
        for h in range(B_GROUPS):
            zh = _dot(ws[h], vn) + bst_ref[:, h:h + 1]
            z = jnp.where(_shr(lane, HEAD_DIM) == h, zh, z)
        o_ref[rows, :] = u_ref[rows, :] * z


def _gmlp(u, vn, ws, bst, nchunk):
    R = u.shape[0]
    tm = nchunk * CHUNK
    row = pl.BlockSpec((tm, B_W), lambda i: (i, 0))
    return pl.pallas_call(
        functools.partial(_gmlp_kernel, nchunk=nchunk),
        grid=(R // tm,),
        in_specs=[row, row, _const_spec((B_GROUPS, CHUNK, CHUNK)), _const_spec((CHUNK, B_GROUPS))],
        out_specs=row,
        out_shape=jax.ShapeDtypeStruct((R, B_W), F32),
        compiler_params=_cparams(("parallel",)),
        name="gmlp",
    )(u, vn, ws, bst)


def _compress_core(x, wab, pe_a, pe_b, b1t, w2bd, b2t):
    nc = x.shape[0]
    half = 2 * CMP_HIDDEN
    h = _dot(x, wab)
    const = _dot(pe_a, wab[:, :half]) + _dot(pe_b, wab[:, half:])
    second = pltpu.roll(h[:, half:], nc - 1, 0)
    hid = _gelu(h[:, :half] + second + const[0:1, :] + b1t)
    return _dot(hid.astype(BF16), w2bd) + b2t


def _cmp_prompt_kernel(krows_ref, vrows_ref, wab_ref, pe_ref, b1_ref, w2_ref, b2_ref, o_ref, *, nc):
    for kv, rows_ref in enumerate((krows_ref, vrows_ref)):
        x = jnp.concatenate(
            [rows_ref[pl.ds(l, nc, stride=CMP_STRIDE), :] for l in range(CMP_STRIDE)], axis=-1).astype(BF16)
        o_ref[:, kv * KV_W:(kv + 1) * KV_W] = _compress_core(
            x, wab_ref[kv], pe_ref[kv, 0], pe_ref[kv, 1], b1_ref[kv], w2_ref[kv], b2_ref[kv])


def _compress_prompt(cmp_rows, cw):
    B, S, _ = cmp_rows.shape
    nc = S // CMP_STRIDE
    return pl.pallas_call(
        functools.partial(_cmp_prompt_kernel, nc=nc),
        grid=(B,),
        in_specs=[pl.BlockSpec((None, S, KV_W), lambda b: (b, 0, 0)), pl.BlockSpec((None, S, KV_W), lambda b: (b, 0, 1)),
                  _const_spec(cw["wab"].shape), _const_spec(cw["pe"].shape), _const_spec(cw["b1t"].shape),
                  _const_spec(cw["w2bd"].shape), _const_spec(cw["b2t"].shape)],
        out_specs=pl.BlockSpec((None, nc, 2 * KV_W), lambda b: (b, 0, 0)),
        out_shape=jax.ShapeDtypeStruct((B, nc, 2 * KV_W), F32),
        compiler_params=_cparams(("parallel",)),
        name="compress_prompt",
    )(cmp_rows, cmp_rows, cw["wab"], cw["pe"], cw["b1t"], cw["w2bd"], cw["b2t"])


def _softmax_tile(qg, k, v, bias, mask, carry):
    m, l, acc = carry
    s = _dot_nt(qg, k) + bias
    s = jnp.where(mask, s, NEG)
    m_new = jnp.maximum(m, jnp.max(s, axis=-1, keepdims=True))
    alpha = jnp.exp(m - m_new)
    p = jnp.where(mask, jnp.exp(s - m_new), 0.0)
    l = alpha * l + jnp.sum(p, axis=-1, keepdims=True)
    acc = alpha * acc + _dot(p.astype(BF16), v)
    return m_new, l, acc


def _rank_select(score, topn, ncols):
    n = score.shape[1]
    j = _iota((1, n), 1)
    rank = jnp.zeros(score.shape, F32)
    for jp in range(ncols):
        col = score[:, jp:jp + 1]
        beats = (col > score) | ((col == score) & (jp < j))
        rank = rank + jnp.where(beats, 1.0, 0.0)
    return jnp.where(rank < topn, 1.0, 0.0)


def _nsa_kernel(q_ref, kv_ref, cmp_ref, misc_ref, tb_ref, o_ref, *, nbp, nsb):
    qi = pl.program_id(1)
    tq = Q_BLOCK
    nrow = C_REP * tq
    t0 = qi * tq
    trow = t0 + _iota((tq, 1), 0)
    trow3 = jnp.concatenate([trow] * C_REP, axis=0)
    tl3 = trow3 - t0
    sl = _iota((1, tq), 1)
    causal3 = sl <= tl3
    edge3 = sl >= tl3
    true3 = causal3 | edge3
    topn = min(SLC_TOPN, nsb)

    for g in range(C_KV):
        qg = jnp.concatenate(
            [q_ref[:, (C_REP * g + r) * HEAD_DIM:(C_REP * g + r + 1) * HEAD_DIM] for r in range(C_REP)], axis=0)

        kc = cmp_ref[:, g * HEAD_DIM:(g + 1) * HEAD_DIM].astype(BF16)
        vc = cmp_ref[:, KV_W + g * HEAD_DIM:KV_W + (g + 1) * HEAD_DIM].astype(BF16)
        s = _dot_nt(qg, kc)
        valid = (_iota((1, nbp), 1) * CMP_STRIDE + (CMP_LEN - 1)) <= trow3
        s = jnp.where(valid, s, NEG)
        e = jnp.exp(s - jnp.max(s, axis=-1, keepdims=True))
        prob = jnp.where(valid, e / jnp.sum(e, axis=-1, keepdims=True), 0.0)
        o_cmp = _dot(prob.astype(BF16), vc)

        psum = prob[0:tq] + prob[tq:2 * tq] + prob[2 * tq:3 * tq]
        per = SLC_BLOCK // CMP_STRIDE
        pool = jnp.where(_shr(_iota((nbp, nsb), 0), per) == _iota((nbp, nsb), 1), 1.0, 0.0)
        imp = jnp.dot(psum, pool, preferred_element_type=F32, precision=lax.Precision.HIGHEST)
        j = _iota((1, nsb), 1)
        cur = _shr(trow, SLC_BLOCK)
        forced = (j == 0) | (j == cur) | (j == cur - 1)
        score = jnp.where(forced, FORCE, jnp.where(j * SLC_BLOCK <= trow, imp, -1.0))
        sel = _rank_select(score, topn, nsb).astype(BF16)
        sel3 = jnp.concatenate([sel] * C_REP, axis=0)

        def sel_mask(base, width):
            blk = base // SLC_BLOCK + _shr(_iota((nsb, width), 1), SLC_BLOCK)
            expand = jnp.where(_iota((nsb, width), 0) == blk, 1.0, 0.0).astype(BF16)
            return _dot(sel3, expand) > 0.5

        bias_diag = tb_ref[g, 0]
        bias_prev = tb_ref[g, 1]
        bias_far = tb_ref[g, 2][:, 0:1]
        ks_lo, vs_lo = 2 * KV_W + g * HEAD_DIM, 3 * KV_W + g * HEAD_DIM
        kw_lo, vw_lo = 4 * KV_W + g * HEAD_DIM, 5 * KV_W + g * HEAD_DIM

        def kv_tile(base, width, k_lo, v_lo):
            k = kv_ref[pl.ds(base, width), k_lo:k_lo + HEAD_DIM]
            v = kv_ref[pl.ds(base, width), v_lo:v_lo + HEAD_DIM]
            return k, v

        init = (jnp.full((nrow, 1), NEG, F32), jnp.zeros((nrow, 1), F32), jnp.zeros((nrow, HEAD_DIM), F32))

        n_far = jnp.maximum(qi - 1, 0)
        n_big = n_far // 4

        def big_body(i, carry):
            base = pl.multiple_of(i * (4 * tq), 4 * tq)
            k, v = kv_tile(base, 4 * tq, ks_lo, vs_lo)
            return _softmax_tile(qg, k, v, bias_far, sel_mask(base, 4 * tq), carry)

        def small_body(i, carry):
            base = pl.multiple_of((n_big * 4 + i) * tq, tq)
            k, v = kv_tile(base, tq, ks_lo, vs_lo)
            return _softmax_tile(qg, k, v, bias_far, sel_mask(base, tq), carry)

        carry = lax.fori_loop(0, n_big, big_body, init)
        carry = lax.fori_loop(0, n_far - 4 * n_big, small_body, carry)
        base = pl.multiple_of(jnp.maximum(qi - 1, 0) * tq, tq)
        k, v = kv_tile(base, tq, ks_lo, vs_lo)
        carry = _softmax_tile(qg, k, v, bias_prev, sel_mask(base, tq) & (qi >= 1), carry)
        base = pl.multiple_of(t0, tq)
        k, v = kv_tile(base, tq, ks_lo, vs_lo)
        _, l, acc = _softmax_tile(qg, k, v, bias_diag, sel_mask(base, tq) & causal3, carry)
        o_slc = acc / l

        carry = init
        nwin = WINDOW // tq
        for i in range(nwin + 1):
            ks = qi - nwin + i
            base = pl.multiple_of(jnp.maximum(ks, 0) * tq, tq)
            k, v = kv_tile(base, tq, kw_lo, vw_lo)
            shape_mask = edge3 if i == 0 else (causal3 if i == nwin else true3)
            bias = bias_diag if i == nwin else (bias_prev if i == nwin - 1 else bias_far)
            carry = _softmax_tile(qg, k, v, bias, shape_mask & (ks >= 0), carry)
        _, l, acc = carry
        o_win = acc / l

        for r in range(C_REP):
            h = C_REP * g + r
            rows = slice(r * tq, (r + 1) * tq)
            gate = lambda br: misc_ref[:, GATE_OFF + br * C_HEADS + h:GATE_OFF + br * C_HEADS + h + 1]
            o_ref[:, h * HEAD_DIM:(h + 1) * HEAD_DIM] = (
                gate(0) * o_cmp[rows] + gate(1) * o_slc[rows] + gate(2) * o_win[rows])


def _nsa_prompt(qc, kvcb, kcv, misc, tb):
    B, S, _ = qc.shape
    nbp = kcv.shape[1]
    nsb = nbp * CMP_STRIDE // SLC_BLOCK
    return pl.pallas_call(
        functools.partial(_nsa_kernel, nbp=nbp, nsb=nsb),
        grid=(B, S // Q_BLOCK),
        in_specs=[pl.BlockSpec((None, Q_BLOCK, C_W), lambda b, i: (b, i, 0)),
                  pl.BlockSpec((None, S, 6 * KV_W), lambda b, i: (b, 0, 0)),
                  pl.BlockSpec((None, nbp, 2 * KV_W), lambda b, i: (b, 0, 0)),
                  pl.BlockSpec((None, Q_BLOCK, 128), lambda b, i: (b, i, 0)),
                  _const_spec(tb.shape)],
        out_specs=pl.BlockSpec((None, Q_BLOCK, C_W), lambda b, i: (b, i, 0)),
        out_shape=jax.ShapeDtypeStruct((B, S, C_W), F32),
        compiler_params=_cparams(("parallel", "arbitrary")),
        name="nsa_prompt",
    )(qc, kvcb, kcv, misc, tb)


def _mix_ffn_kernel(a_ref, b_ref, c_ref, x_ref, ga_ref, gb_ref, gc_ref, wo_ref, gpm_ref, gpf_ref,
                    win_ref, wout_ref, gpo_ref, o_ref, *, nsplit):
    an = _rms(a_ref[...], ga_ref[...]).astype(BF16)
    bn = _rms(b_ref[...], gb_ref[...]).astype(BF16)
    cn = _rms(c_ref[...], gc_ref[...]).astype(BF16)
    mix = (_dot(an, wo_ref[0:A_W, :]) + _dot(bn, wo_ref[A_W:A_W + B_W, :])
           + _dot(cn, wo_ref[A_W + B_W:A_W + B_W + C_W, :]))
    x1 = x_ref[...] + _rms(mix, gpm_ref[...])
    h = _rms(x1, gpf_ref[...]).astype(BF16)
    wid = D_FF // nsplit
    y = jnp.zeros(x1.shape, F32)
    for c in range(nsplit):
        lo = c * wid
        gate = _dot(h, win_ref[:, lo:lo + wid])
        up = _dot(h, win_ref[:, D_FF + lo:D_FF + lo + wid])
        act = gate * _sigmoid(gate) * up
        y = y + _dot(act.astype(BF16), wout_ref[lo:lo + wid, :])
    o_ref[...] = x1 + _rms(y, gpo_ref[...])


def _mix_ffn(a, b, c, x, lw, tm):
    R = x.shape[0]
    row = lambda n: pl.BlockSpec((tm, n), lambda i: (i, 0))
    once = lambda shape: pl.BlockSpec(shape, lambda i: (0, 0), pipeline_mode=pl.Buffered(1))
    return pl.pallas_call(
        functools.partial(_mix_ffn_kernel, nsplit=2),
        grid=(R // tm,),
        in_specs=[row(A_W), row(B_W), row(C_W), row(D_MODEL),
                  once((1, A_W)), once((1, B_W)), once((1, C_W)), once((D_MODEL, D_MODEL)),
                  once((1, D_MODEL)), once((1, D_MODEL)), once((D_MODEL, 2 * D_FF)), once((D_FF, D_MODEL)),
                  once((1, D_MODEL))],
        out_specs=row(D_MODEL),
        out_shape=jax.ShapeDtypeStruct((R, D_MODEL), F32),
        compiler_params=_cparams(("parallel",)),
        name="mix_ffn",
    )(a, b, c, x, lw["ga"], lw["gb"], lw["gc"], lw["w_o"], lw["gpm"], lw["gpf"], lw["w_ffn_in"],
      lw["w_ffn_out"], lw["gpo"])


def _fox_dec_kernel(pt_ref, q_ref, kv_ref, c_ref, new_ref, cnew_ref, o_ref, m_sc, l_sc, acc_sc):
    p = pl.program_id(1)
    nh = 8
    hrow = _iota((nh, A_W), 0)
    own = _shr(_iota((nh, A_W), 1), HEAD_DIM) == hrow
    qbd = jnp.where(own, jnp.broadcast_to(q_ref[...].astype(F32), (nh, A_W)), 0.0).astype(BF16)

    @pl.when(p == 0)
    def _():
        m_sc[...] = jnp.full(m_sc.shape, NEG, F32)
        l_sc[...] = jnp.zeros(l_sc.shape, F32)
        acc_sc[...] = jnp.zeros(acc_sc.shape, F32)

    k = kv_ref[:, 0:A_W].astype(BF16)
    v = kv_ref[:, A_W:2 * A_W].astype(BF16)
    s = _dot_nt(qbd, k) - c_ref[...]
    m_prev = m_sc[...]
    m_new = jnp.maximum(m_prev, jnp.max(s, axis=-1, keepdims=True))
    alpha = jnp.exp(m_prev - m_new)
    pe = jnp.exp(s - m_new)
    l_sc[...] = alpha * l_sc[...] + jnp.sum(pe, axis=-1, keepdims=True)
    acc_sc[...] = alpha * acc_sc[...] + _dot(pe.astype(BF16), v)
    m_sc[...] = m_new

    @pl.when(p == pl.num_programs(1) - 1)
    def _():
        kn = new_ref[:, 0:A_W].astype(BF16).astype(F32)
        vn = new_ref[:, A_W:2 * A_W].astype(BF16).astype(F32)
        sn = jnp.sum(qbd.astype(F32) * kn, axis=-1, keepdims=True) - cnew_ref[:, 0:1]
        m_prev = m_sc[...]
        m_new = jnp.maximum(m_prev, sn)
        alpha = jnp.exp(m_prev - m_new)
        pn = jnp.exp(sn - m_new)
        l = alpha * l_sc[...] + pn
        acc = alpha * acc_sc[...] + pn.astype(BF16).astype(F32) * vn
        o_ref[...] = jnp.sum(jnp.where(own, acc / l, 0.0), axis=0, keepdims=True)


def _fox_decode(layer, page_table, qa, pool, crow, kva_new):
    nb, npages = page_table.shape
    past = npages * pool.shape[2]
    grid_spec = pltpu.PrefetchScalarGridSpec(
        num_scalar_prefetch=1,
        grid=(nb, npages),
        in_specs=[
            pl.BlockSpec((None, 1, A_W), lambda b, p, pt: (b, 0, 0)),
            pl.BlockSpec((None, None, pool.shape[2], 2 * A_W), lambda b, p, pt: (layer, pt[b, p], 0, 0)),
            pl.BlockSpec((None, 8, LANES), lambda b, p, pt: (b, 0, p)),
            pl.BlockSpec((None, 1, 2 * A_W), lambda b, p, pt: (b, 0, 0)),
            pl.BlockSpec((None, 8, LANES), lambda b, p, pt: (b, 0, past // LANES)),
        ],
        out_specs=pl.BlockSpec((None, 1, A_W), lambda b, p, pt: (b, 0, 0)),
        scratch_shapes=[pltpu.VMEM((8, 1), F32), pltpu.VMEM((8, 1), F32), pltpu.VMEM((8, A_W), F32)],
    )
    return pl.pallas_call(
        _fox_dec_kernel,
        grid_spec=grid_spec,
        out_shape=jax.ShapeDtypeStruct((nb, 1, A_W), F32),
        compiler_params=_cparams(("parallel", "arbitrary")),
        name="fox_decode",
    )(page_table, qa, pool, crow, kva_new, crow)


def _cmp_dec_kernel(pt_ref, kpage_ref, vpage_ref, wab_ref, pe_ref, b1_ref, w2_ref, b2_ref, q_ref,
                    ocmp_ref, sel_ref, x_sc, *, nc, nsb_all):
    p = pl.program_id(1)
    per_page = kpage_ref.shape[0] // CMP_STRIDE
    for kv, page_ref in enumerate((kpage_ref, vpage_ref)):
        for l in range(CMP_STRIDE):
            x_sc[kv, pl.ds(pl.multiple_of(p * per_page, per_page), per_page), l * KV_W:(l + 1) * KV_W] = (
                page_ref[pl.ds(l, per_page, stride=CMP_STRIDE), :])

    @pl.when(p == pl.num_programs(1) - 1)
    def _():
        kcv = [_compress_core(x_sc[kv].astype(BF16), wab_ref[kv], pe_ref[kv, 0], pe_ref[kv, 1],
                              b1_ref[kv], w2_ref[kv], b2_ref[kv]).astype(BF16) for kv in range(2)]
        q = q_ref[...]
        grp0 = _iota((8, 1), 0) < C_REP
        pick = lambda x0, x1: jnp.where(grp0, x0, x1)
        s = pick(_dot_nt(q, kcv[0][:, 0:HEAD_DIM]), _dot_nt(q, kcv[0][:, HEAD_DIM:KV_W]))
        valid = _iota((1, nc), 1) < nc - 1
        s = jnp.where(valid, s, NEG)
        e = jnp.exp(s - jnp.max(s, axis=-1, keepdims=True))
        prob = jnp.where(valid, e / jnp.sum(e, axis=-1, keepdims=True), 0.0)
        pb = prob.astype(BF16)
        ocmp_ref[...] = pick(_dot(pb, kcv[1][:, 0:HEAD_DIM]), _dot(pb, kcv[1][:, HEAD_DIM:KV_W]))

        per = SLC_BLOCK // CMP_STRIDE
        nsbp = sel_ref.shape[1]
        hrow = _iota((8, 1), 0)
        psum = jnp.concatenate(
            [jnp.sum(jnp.where((hrow >= C_REP * g) & (hrow < C_REP * (g + 1)), prob, 0.0), axis=0, keepdims=True)
             for g in range(C_KV)] + [jnp.zeros((8 - C_KV, nc), F32)], axis=0)
        pool = jnp.where(_shr(_iota((nc, nsbp), 0), per) == _iota((nc, nsbp), 1), 1.0, 0.0)
        imp = jnp.dot(psum, pool, preferred_element_type=F32, precision=lax.Precision.HIGHEST)
        j = _iota((1, nsbp), 1)
        cur = nsb_all - 1
        forced = (j == 0) | (j == cur) | (j == cur - 1)
        score = jnp.where(forced, FORCE, jnp.where(j <= cur, imp, -2.0))
        sel_ref[...] = _rank_select(score, min(SLC_TOPN, nsb_all), nsb_all)


def _cmp_decode(layer, page_table, pool, cw, qh):
    nb, npages = page_table.shape
    page = pool.shape[2]
    nc = npages * page // CMP_STRIDE
    nsb_all = npages * page // SLC_BLOCK + 1
    nsbp = -(-nsb_all // LANES) * LANES
    grid_spec = pltpu.PrefetchScalarGridSpec(
        num_scalar_prefetch=1,
        grid=(nb, npages),
        in_specs=[
            pl.BlockSpec((None, None, page, KV_W), lambda b, p, pt: (layer, pt[b, p], 0, 0)),
            pl.BlockSpec((None, None, page, KV_W), lambda b, p, pt: (layer, pt[b, p], 0, 1)),
            pl.BlockSpec(cw["wab"].shape, lambda b, p, pt: (0, 0, 0)),
            pl.BlockSpec(cw["pe"].shape, lambda b, p, pt: (0, 0, 0, 0)),
            pl.BlockSpec(cw["b1t"].shape, lambda b, p, pt: (0, 0, 0)),
            pl.BlockSpec(cw["w2bd"].shape, lambda b, p, pt: (0, 0, 0)),
            pl.BlockSpec(cw["b2t"].shape, lambda b, p, pt: (0, 0, 0)),
            pl.BlockSpec((None, 8, HEAD_DIM), lambda b, p, pt: (b, 0, 0)),
        ],
        out_specs=[pl.BlockSpec((None, 8, HEAD_DIM), lambda b, p, pt: (b, 0, 0)),
                   pl.BlockSpec((None, 8, nsbp), lambda b, p, pt: (b, 0, 0))],
        scratch_shapes=[pltpu.VMEM((2, nc, CMP_STRIDE * KV_W), F32)],
    )
    return pl.pallas_call(
        functools.partial(_cmp_dec_kernel, nc=nc, nsb_all=nsb_all),
        grid_spec=grid_spec,
        out_shape=[jax.ShapeDtypeStruct((nb, 8, HEAD_DIM), F32), jax.ShapeDtypeStruct((nb, 8, nsbp), F32)],
        compiler_params=_cparams(("parallel", "arbitrary")),
        name="cmp_decode",
    )(page_table, pool, pool, cw["wab"], cw["pe"], cw["b1t"], cw["w2bd"], cw["b2t"], qh)


def _slc_dec_kernel(eff_ref, need_ref, flag_ref, q_ref, page_ref, bias_ref, win_ref, wbias_ref, b0_ref,
                    new_ref, gate_ref, ocmp_ref, o_ref, m_sc, l_sc, acc_sc, *, nflag):
    b = pl.program_id(0)
    p = pl.program_id(1)
    npg = pl.num_programs(1)
    q = q_ref[...]
    grp0 = _iota((8, 1), 0) < C_REP
    pick = lambda x0, x1: jnp.where(grp0, x0, x1)

    @pl.when(p == 0)
    def _():
        m_sc[...] = jnp.full(m_sc.shape, NEG, F32)
        l_sc[...] = jnp.zeros(l_sc.shape, F32)
        acc_sc[...] = jnp.zeros(acc_sc.shape, F32)

    @pl.when(need_ref[b * npg + p] > 0)
    def _():
        width = page_ref.shape[0]
        lane = _iota((1, width), 1)
        blk = _shr(lane, SLC_BLOCK)
        nblk = width // SLC_BLOCK

        def lane_flags(g):
            f = jnp.zeros((1, width), jnp.int32)
            for i in range(nblk):
                f = jnp.where(blk == i, flag_ref[(b * C_KV + g) * nflag + p * nblk + i], f)
            return f

        mask = pick(lane_flags(0), lane_flags(1)) > 0
        k = [page_ref[:, g * HEAD_DIM:(g + 1) * HEAD_DIM].astype(BF16) for g in range(C_KV)]
        v = [page_ref[:, KV_W + g * HEAD_DIM:KV_W + (g + 1) * HEAD_DIM].astype(BF16) for g in range(C_KV)]
        s = pick(_dot_nt(q, k[0]), _dot_nt(q, k[1])) + bias_ref[...]
        s = jnp.where(mask, s, NEG)
        m_prev = m_sc[...]
        m_new = jnp.maximum(m_prev, jnp.max(s, axis=-1, keepdims=True))
        alpha = jnp.exp(m_prev - m_new)
        pe = jnp.where(mask, jnp.exp(s - m_new), 0.0)
        pb = pe.astype(BF16)
        l_sc[...] = alpha * l_sc[...] + jnp.sum(pe, axis=-1, keepdims=True)
        acc_sc[...] = alpha * acc_sc[...] + pick(_dot(pb, v[0]), _dot(pb, v[1]))
        m_sc[...] = m_new

    @pl.when(p == npg - 1)
    def _():
        qf = q.astype(F32)
        rnd = lambda x: x.astype(BF16).astype(F32)
        b0 = b0_ref[:, 0:1]
        sn = jnp.sum(qf * rnd(new_ref[0]), axis=-1, keepdims=True) + b0
        m_prev = m_sc[...]
        m_new = jnp.maximum(m_prev, sn)
        alpha = jnp.exp(m_prev - m_new)
        pn = jnp.exp(sn - m_new)
        l = alpha * l_sc[...] + pn
        o_slc = (alpha * acc_sc[...] + rnd(pn) * rnd(new_ref[1])) / l
        kw = [win_ref[:, g * HEAD_DIM:(g + 1) * HEAD_DIM].astype(BF16) for g in range(C_KV)]
        vw = [win_ref[:, KV_W + g * HEAD_DIM:KV_W + (g + 1) * HEAD_DIM].astype(BF16) for g in range(C_KV)]
        s = pick(_dot_nt(q, kw[0]), _dot_nt(q, kw[1])) + wbias_ref[...]
        sn = jnp.sum(qf * rnd(new_ref[2]), axis=-1, keepdims=True) + b0
        m = jnp.maximum(jnp.max(s, axis=-1, keepdims=True), sn)
        pe = jnp.exp(s - m)
        pn = jnp.exp(sn - m)
        l = jnp.sum(pe, axis=-1, keepdims=True) + pn
        pb = pe.astype(BF16)
        o_win = (pick(_dot(pb, vw[0]), _dot(pb, vw[1])) + rnd(pn) * rnd(new_ref[3])) / l
        o_ref[...] = (gate_ref[:, 0:1] * ocmp_ref[...] + gate_ref[:, 1:2] * o_slc + gate_ref[:, 2:3] * o_win)


def _slc_decode(layer, eff, need, flags, nflag, qh, pool, sbias, win, wbias, b0, newrows, gates, ocmp):
    nb = qh.shape[0]
    npages = eff.shape[0] // nb
    page = pool.shape[2]
    wb = win.shape[2]
    grid_spec = pltpu.PrefetchScalarGridSpec(
        num_scalar_prefetch=3,
        grid=(nb, npages),
        in_specs=[
            pl.BlockSpec((None, 8, HEAD_DIM), lambda b, p, e, n, f: (b, 0, 0)),
            pl.BlockSpec((None, None, page, 2 * KV_W), lambda b, p, e, n, f: (layer, e[b * npages + p], 0, 0)),
            pl.BlockSpec((8, page), lambda b, p, e, n, f: (0, p)),
            pl.BlockSpec((None, None, wb, 2 * KV_W), lambda b, p, e, n, f: (layer, b, 0, 0)),
            pl.BlockSpec((8, wb), lambda b, p, e, n, f: (0, 0)),
            pl.BlockSpec((8, LANES), lambda b, p, e, n, f: (0, 0)),
            pl.BlockSpec((None, 4, 8, HEAD_DIM), lambda b, p, e, n, f: (b, 0, 0, 0)),
            pl.BlockSpec((None, 8, LANES), lambda b, p, e, n, f: (b, 0, 0)),
            pl.BlockSpec((None, 8, HEAD_DIM), lambda b, p, e, n, f: (b, 0, 0)),
        ],
        out_specs=pl.BlockSpec((None, 8, HEAD_DIM), lambda b, p, e, n, f: (b, 0, 0)),
        scratch_shapes=[pltpu.VMEM((8, 1), F32), pltpu.VMEM((8, 1), F32), pltpu.VMEM((8, HEAD_DIM), F32)],
    )
    return pl.pallas_call(
        functools.partial(_slc_dec_kernel, nflag=nflag),
        grid_spec=grid_spec,
        out_shape=jax.ShapeDtypeStruct((nb, 8, HEAD_DIM), F32),
        compiler_params=_cparams(("parallel", "arbitrary")),
        name="slc_win_decode",
    )(eff, need, flags, qh, pool, sbias, win, wbias, b0, newrows, gates, ocmp)


def _t5_bucket(dist):
    n = jnp.maximum(dist, 0)
    max_exact = NUM_BUCKETS // 2
    nf = jnp.maximum(n, 1).astype(F32)
    large = max_exact + (jnp.log(nf / max_exact) / math.log(MAX_DISTANCE / max_exact)
                         * (NUM_BUCKETS - max_exact)).astype(jnp.int32)
    return jnp.where(n < max_exact, n, jnp.minimum(large, NUM_BUCKETS - 1))


def _prep_w_in(w_in):
    splits = [int(s) for s in np.cumsum(PROJ_WIDTHS)[:-1]]
    a_q, a_k, a_v, a_f, b_u, b_v, c_q, c_kc, c_vc, c_ks, c_vs, c_kw, c_vw, c_g = jnp.split(w_in, splits, axis=-1)
    pad = jnp.zeros(w_in.shape[:-1] + (LANES - A_HEADS - 3 * C_HEADS,), w_in.dtype)
    return jnp.concatenate([a_q, a_k, a_v, b_u, b_v, c_q, c_kc, c_vc, c_ks, c_vs, c_kw, c_vw, a_f, c_g, pad],
                           axis=-1).astype(BF16)


def _prep_compress(cmp_pe, cmp_w1, cmp_b1, cmp_w2, cmp_b2):
    depth = cmp_w1.shape[0]
    half = CMP_LEN // 2
    eye = jnp.eye(C_KV, dtype=F32)
    w1 = cmp_w1.reshape(depth, 2, 2, half, HEAD_DIM, CMP_HIDDEN)
    wab = jnp.einsum("zkhldj,gG->zklgdhGj", w1, eye).reshape(depth, 2, half * KV_W, 2 * C_KV * CMP_HIDDEN)
    pe = cmp_pe.reshape(depth, 2, 2, half, 1, HEAD_DIM)
    pe = jnp.broadcast_to(pe, (depth, 2, 2, half, C_KV, HEAD_DIM)).reshape(depth, 2, 2, 1, half * KV_W)
    pe = jnp.broadcast_to(pe, (depth, 2, 2, 8, half * KV_W))
    w2bd = jnp.einsum("zkjd,gG->zkgjGd", cmp_w2, eye).reshape(depth, 2, C_KV * CMP_HIDDEN, KV_W)
    return {
        "wab": wab.astype(BF16), "pe": pe.astype(BF16),
        "b1t": jnp.tile(cmp_b1, (1, 1, C_KV))[:, :, None, :],
        "w2bd": w2bd.astype(BF16),
        "b2t": jnp.tile(cmp_b2, (1, 1, C_KV))[:, :, None, :],
    }


def _prompt_bias_tiles(table):
    tl = jnp.arange(Q_BLOCK)[:, None]
    sl = jnp.arange(Q_BLOCK)[None, :]
    d = tl - sl
    tiles = []
    for delta in (0, Q_BLOCK, 8 * MAX_DISTANCE):
        b = table[_t5_bucket(d + delta)]
        b = b.transpose(2, 0, 1).reshape(C_KV, C_REP * Q_BLOCK, Q_BLOCK)
        tiles.append(b)
    return jnp.stack(tiles, axis=1).astype(F32)


def _sample_bias(table, past, wb):
    pad8 = lambda x: jnp.pad(x, ((0, 8 - C_HEADS), (0, 0)))
    sbias = pad8(table[_t5_bucket(past - jnp.arange(past))].T)
    wbias = pad8(table[_t5_bucket(wb - jnp.arange(wb))].T)
    b0 = pad8(jnp.broadcast_to(table[0][:, None], (C_HEADS, LANES)))
    return sbias.astype(F32), wbias.astype(F32), b0.astype(F32)


def _layer_prompt(x, lw, cw, tb, tm_proj, tm_mix, tq_fox):
    B, S, _ = x.shape
    R = B * S
    x2 = x.reshape(R, D_MODEL)
    qa, kva, kvab, bu, vn, qc, cmp_r, slc_r, win_r, kvcb, misc = _proj(
        x2, lw["g_pre"], lw["w_in"], lw["bf"], lw["lng"], lw["lnb"], tm_proj)
    lf = misc[:, :A_HEADS].reshape(B, S, A_HEADS)
    crow = jnp.cumsum(lf, axis=1).transpose(0, 2, 1).reshape(B, A_HEADS // 2, 2, S)
    a_out = _fox_prompt(qa.reshape(B, S, A_W), kvab.reshape(B, S, 2 * A_W), crow, tq_fox)
    b_out = _gmlp(bu, vn, lw["ws"], lw["bst"], min(8, R // CHUNK))
    kcv = _compress_prompt(cmp_r.reshape(B, S, 2 * KV_W), cw)
    c_out = _nsa_prompt(qc.reshape(B, S, C_W), kvcb.reshape(B, S, 6 * KV_W), kcv,
                        misc.reshape(B, S, 128), tb)
    y = _mix_ffn(a_out.reshape(R, A_W), b_out, c_out.reshape(R, C_W), x2, lw, tm_mix)
    wb = min(WINDOW, S)
    kv5 = lambda r: r.reshape(B, S, 2, C_KV, HEAD_DIM)
    states = (kva.reshape(B, S, 2, A_HEADS, HEAD_DIM), lf, kv5(cmp_r), kv5(slc_r), kv5(win_r)[:, S - wb:])
    return y.reshape(B, S, D_MODEL), states


def _layer_sample(layer, x, lw, cw, sb, pools, page_table):
    nb = x.shape[0]
    fox_pool, logf_pool, cmp_pool, slc_pool, win_buf = pools
    npages = page_table.shape[1]
    page = fox_pool.shape[2]
    past = npages * page
    x2 = x.reshape(nb, D_MODEL)
    qa, kva, _, bu, vn, qc, cmp_r, slc_r, win_r, _, misc = _proj(
        x2, lw["g_pre"], lw["w_in"], lw["bf"], lw["lng"], lw["lnb"], nb)
    lf = misc[:, :A_HEADS]

    lf_past = logf_pool[layer][page_table].reshape(nb, past, A_HEADS)
    c = jnp.cumsum(jnp.concatenate([lf_past, lf[:, None, :]], axis=1), axis=1)
    crow = jnp.zeros((nb, 8, past + LANES), F32).at[:, :A_HEADS, :past + 1].set(c.transpose(0, 2, 1))
    a_out = _fox_decode(layer, page_table, qa.reshape(nb, 1, A_W), fox_pool, crow,
                        kva.reshape(nb, 1, 2 * A_W)).reshape(nb, A_W)

    pad_chunk = lambda t: jnp.zeros((nb, CHUNK, B_W), F32).at[:, 0].set(t).reshape(nb * CHUNK, B_W)
    b_out = _gmlp(pad_chunk(bu), pad_chunk(vn), lw["ws"], lw["bst"], min(8, nb)).reshape(nb, CHUNK, B_W)[:, 0]

    qh = jnp.pad(qc.reshape(nb, C_HEADS, HEAD_DIM), ((0, 0), (0, 8 - C_HEADS), (0, 0)))
    o_cmp, sel = _cmp_decode(layer, page_table, cmp_pool, cw, qh)
    nsb_all = past // SLC_BLOCK + 1
    per_page = page // SLC_BLOCK
    nflag = (npages + 1) * per_page
    flags = jnp.pad(sel[:, :C_KV, :nsb_all] > 0.5, ((0, 0), (0, 0), (0, nflag - nsb_all)))
    need = flags[:, :, :npages * per_page].reshape(nb, C_KV, npages, per_page).any(axis=(1, 3))
    slot = jnp.where(need, jnp.arange(npages)[None, :], 0)
    last_needed = lax.cummax(slot, axis=1)
    eff = jnp.take_along_axis(page_table, last_needed, axis=1)
    head_rows = lambda r: jnp.pad(jnp.repeat(r.reshape(nb, C_KV, HEAD_DIM), C_REP, axis=1),
                                  ((0, 0), (0, 8 - C_HEADS), (0, 0)))
    newrows = jnp.stack([head_rows(slc_r[:, :KV_W]), head_rows(slc_r[:, KV_W:]),
                         head_rows(win_r[:, :KV_W]), head_rows(win_r[:, KV_W:])], axis=1)
    g3 = misc[:, GATE_OFF:GATE_OFF + 3 * C_HEADS].reshape(nb, 3, C_HEADS).transpose(0, 2, 1)
    gates = jnp.zeros((nb, 8, LANES), F32).at[:, :C_HEADS, :3].set(g3)
    sbias, wbias, b0 = sb
    c_heads = _slc_decode(layer, eff.reshape(-1).astype(jnp.int32), need.reshape(-1).astype(jnp.int32),
                          flags.reshape(-1).astype(jnp.int32), nflag, qh, slc_pool, sbias, win_buf, wbias, b0,
                          newrows, gates, o_cmp)
    c_out = c_heads[:, :C_HEADS].reshape(nb, C_W)

    y = _mix_ffn(a_out, b_out, c_out, x2, lw, nb)
    kv5 = lambda r: r.reshape(nb, 1, 2, C_KV, HEAD_DIM)
    win_new = jnp.concatenate([win_buf[layer].reshape(nb, -1, 2, C_KV, HEAD_DIM)[:, 1:], kv5(win_r)], axis=1)
    states = (kva.reshape(nb, 1, 2, A_HEADS, HEAD_DIM), lf[:, None, :], kv5(cmp_r), kv5(slc_r), win_new,
              vn.reshape(nb, 1, B_GROUPS, HEAD_DIM))
    return y.reshape(nb, 1, D_MODEL), states


def kernel(x_prompt, x_sample, cache_fox_kv, cache_fox_logf, cache_cmp_kv, cache_slc_kv, state_win_kv, page_table, rel_bias_table, norm_pre_mix, w_in, b_forget, gmlp_ln_g, gmlp_ln_b, gmlp_ws, gmlp_bs, cmp_pe, cmp_w1, cmp_b1, cmp_w2, cmp_b2, norm_group_a, norm_group_b, norm_group_c, w_o, norm_post_mix, norm_pre_ffn, w_ffn_in, w_ffn_out, norm_post_ffn):
    depth = w_in.shape[0]
    assert x_sample.shape[1] == 1
    B, S, _ = x_prompt.shape
    n_pool, page = cache_fox_kv.shape[1], cache_fox_kv.shape[2]
    past = page_table.shape[1] * page
    wb = state_win_kv.shape[2]

    w_in_p = _prep_w_in(w_in)
    bf = jnp.pad(b_forget, ((0, 0), (0, LANES - A_HEADS)))[:, None, :]
    cw_all = _prep_compress(cmp_pe, cmp_w1, cmp_b1, cmp_w2, cmp_b2)
    tb = _prompt_bias_tiles(rel_bias_table)
    sb = _sample_bias(rel_bias_table, past, wb)
    w_o_b = w_o.astype(BF16)
    w_fi_b = w_ffn_in.astype(BF16)
    w_fo_b = w_ffn_out.astype(BF16)
    bst = gmlp_bs.transpose(0, 2, 1)
    pools = (cache_fox_kv.reshape(depth, n_pool, page, 2 * A_W), cache_fox_logf,
             cache_cmp_kv.reshape(depth, n_pool, page, 2 * KV_W), cache_slc_kv.reshape(depth, n_pool, page, 2 * KV_W),
             state_win_kv.reshape(depth, x_sample.shape[0], wb, 2 * KV_W))

    tm_proj = min(512, B * S)
    tm_mix = min(256, B * S)
    tq_fox = min(512, S)

    xp, xs = x_prompt, x_sample
    sp = [[] for _ in range(5)]
    ss = [[] for _ in range(6)]
    for l in range(depth):
        r1 = lambda a: a[l][None, :]
        lw = {
            "g_pre": r1(norm_pre_mix), "w_in": w_in_p[l], "bf": bf[l], "lng": r1(gmlp_ln_g), "lnb": r1(gmlp_ln_b),
            "ws": gmlp_ws[l], "bst": bst[l], "ga": r1(norm_group_a), "gb": r1(norm_group_b), "gc": r1(norm_group_c),
            "w_o": w_o_b[l], "gpm": r1(norm_post_mix), "gpf": r1(norm_pre_ffn), "w_ffn_in": w_fi_b[l],
            "w_ffn_out": w_fo_b[l], "gpo": r1(norm_post_ffn),
        }
        cw = {k: v[l] for k, v in cw_all.items()}
        xp, st_p = _layer_prompt(xp, lw, cw, tb, tm_proj, tm_mix, tq_fox)
        xs, st_s = _layer_sample(l, xs, lw, cw, sb, pools, page_table)
        for lst, s in zip(sp, st_p):
            lst.append(s)
        for lst, s in zip(ss, st_s):
            lst.append(s)
    return tuple([xp, xs] + [jnp.stack(s) for s in sp] + [jnp.stack(s) for s in ss])
```

```python
import functools
import math

import numpy as np
import jax
import jax.numpy as jnp
from jax import lax
from jax.experimental import pallas as pl
from jax.experimental.pallas import tpu as pltpu

F32 = jnp.float32
BF16 = jnp.bfloat16
HIGHEST = lax.Precision.HIGHEST

D_MODEL = 1024
HEAD_DIM = 64
A_HEADS = 6
B_GROUPS = 4
C_HEADS = 6
C_KV = 2
C_REP = C_HEADS // C_KV
A_W = A_HEADS * HEAD_DIM
B_W = B_GROUPS * HEAD_DIM
C_W = C_HEADS * HEAD_DIM
KV_W = C_KV * HEAD_DIM
Q_BLOCK = 128
CHUNK = 128
CMP_LEN = 32
CMP_STRIDE = 16
CMP_HIDDEN = 256
SLC_BLOCK = 64
SLC_TOPN = 16
WINDOW = 512
NUM_BUCKETS = 32
MAX_DISTANCE = 128
D_FF = 2816
EPS = 1e-6
NEG = -1e30
M_INIT = 0.5 * NEG
FORCE = 1e4
SCALE = HEAD_DIM ** -0.5
PROJ_WIDTHS = (A_W, A_W, A_W, A_HEADS, B_W, B_W, C_W, KV_W, KV_W, KV_W, KV_W, KV_W, KV_W, 3 * C_HEADS)

LANES = 128
SUBLANES = 8
HEAD_PAD = 8
VMEM_LIMIT = 56 * 1024 * 1024
PAGES_PER_STEP = 8

P_QA = (0, 384)
P_KVA = (384, 1152)
P_BU = (1152, 1408)
P_BV = (1408, 1664)
P_QC = (1664, 2048)
P_CMP = (2048, 2304)
P_SLC = (2304, 2560)
P_WIN = (2560, 2816)
P_MISC = (2816, 2944)
N_PROJ = 2944
GATE_OFF = A_HEADS


def _dot(a, b):
    return jnp.dot(a, b, preferred_element_type=F32)


def _dot_exact(a, b):
    return jnp.dot(a, b, preferred_element_type=F32, precision=HIGHEST)


def _dot_nt(a, b, precision=None):
    return lax.dot_general(a, b, (((1,), (1,)), ((), ())), preferred_element_type=F32, precision=precision)


def _gelu(x):
    return 0.5 * x * (1.0 + jnp.tanh(math.sqrt(2.0 / math.pi) * (x + 0.044715 * (x * x * x))))


def _sigmoid(x):
    return 1.0 / (1.0 + jnp.exp(-x))


def _rms(x, g):
    return x * lax.rsqrt(jnp.mean(x * x, axis=-1, keepdims=True) + EPS) * g


def _iota(shape, dim):
    return lax.broadcasted_iota(jnp.int32, shape, dim)


def _shr(x, n):
    return jnp.right_shift(x, int(math.log2(n)))


def _cparams(sem):
    return pltpu.CompilerParams(dimension_semantics=sem, vmem_limit_bytes=VMEM_LIMIT)


def _const_spec(shape):
    nd = len(shape)
    return pl.BlockSpec(shape, lambda *_: (0,) * nd)


def _proj_kernel(x_ref, g_ref, w_ref, bf_ref, lng_ref, lnb_ref,
                 qa_ref, kva_ref, kvab_ref, bu_ref, vn_ref, qc_ref,
                 cmp_ref, slc_ref, win_ref, kvcb_ref, misc_ref):
    h = _rms(x_ref[...], g_ref[...]).astype(BF16)

    def mm(seg):
        return _dot(h, w_ref[:, seg[0]:seg[1]])

    qa_ref[...] = (mm(P_QA) * SCALE).astype(BF16)
    kva = mm(P_KVA)
    kva_ref[...] = kva
    kvab_ref[...] = kva.astype(BF16)
    bu_ref[...] = _gelu(mm(P_BU))
    v = _gelu(mm(P_BV))
    mu = jnp.mean(v, axis=-1, keepdims=True)
    var = jnp.mean(jnp.square(v - mu), axis=-1, keepdims=True)
    vn_ref[...] = (v - mu) * lax.rsqrt(var + EPS) * lng_ref[...] + lnb_ref[...]
    qc_ref[...] = (mm(P_QC) * SCALE).astype(BF16)
    c = mm(P_CMP)
    cmp_ref[...] = c
    kvcb_ref[:, 0:256] = c.astype(BF16)
    c = mm(P_SLC)
    slc_ref[...] = c
    kvcb_ref[:, 256:512] = c.astype(BF16)
    c = mm(P_WIN)
    win_ref[...] = c
    kvcb_ref[:, 512:768] = c.astype(BF16)
    m = mm(P_MISC) + bf_ref[...]
    lane = _iota(m.shape, 1)
    logsig = jnp.minimum(m, 0.0) - jnp.log1p(jnp.exp(-jnp.abs(m)))
    misc_ref[...] = jnp.where(lane < A_HEADS, logsig, _sigmoid(m))


def _proj(x2d, g, w, bf, lng, lnb, tm):
    R = x2d.shape[0]
    row = lambda n: pl.BlockSpec((tm, n), lambda i: (i, 0))
    outs = [(384, BF16), (768, F32), (768, BF16), (256, F32), (256, F32), (384, BF16),
            (256, F32), (256, F32), (256, F32), (768, BF16), (128, F32)]
    return pl.pallas_call(
        _proj_kernel,
        grid=(R // tm,),
        in_specs=[row(D_MODEL), _const_spec((1, D_MODEL)), _const_spec((D_MODEL, N_PROJ)),
                  _const_spec((1, 128)), _const_spec((1, B_W)), _const_spec((1, B_W))],
        out_specs=[row(n) for n, _ in outs],
        out_shape=[jax.ShapeDtypeStruct((R, n), dt) for n, dt in outs],
        compiler_params=_cparams(("parallel",)),
        name="proj",
    )(x2d, g, w, bf, lng, lnb)


def _fox_kernel(q_ref, k_ref, v_ref, c_ref, o_ref, *, tq):
    qi = pl.program_id(2)
    tk = tq
    q = q_ref[...]
    lane = _iota((1, LANES), 1)
    low = lane < HEAD_DIM
    zero = jnp.zeros_like(q)
    qm = (jnp.where(low, q, zero), jnp.where(low, zero, q))
    one_lane = (HEAD_DIM, 0)

    def v_aug(v, hh):
        mine = low if hh == 0 else jnp.logical_not(low)
        ones = jnp.where(lane == one_lane[hh], 1.0, 0.0).astype(BF16)
        return jnp.where(mine, v, jnp.broadcast_to(ones, v.shape))

    def tile(ki, carry, causal):
        base = pl.multiple_of(ki * tk, tk)
        k = k_ref[pl.ds(base, tk), :]
        v = v_ref[pl.ds(base, tk), :]
        out = []
        for hh in range(2):
            m, acc = carry[hh]
            s = _dot_nt(qm[hh], k) - c_ref[hh:hh + 1, pl.ds(base, tk)]
            if causal is not None:
                s = jnp.where(causal, s, NEG)
            m_new = jnp.maximum(m, jnp.max(s, axis=-1, keepdims=True))
            alpha = jnp.exp(m - m_new)
            p = jnp.exp(s - m_new)
            acc = alpha * acc + _dot(p.astype(BF16), v_aug(v, hh))
            out.append((m_new, acc))
        return tuple(out)

    init = tuple((jnp.full((tq, 1), M_INIT, F32), jnp.zeros((tq, LANES), F32)) for _ in range(2))
    carry = lax.fori_loop(0, qi, lambda ki, c: tile(ki, c, None), init)
    causal = _iota((1, tk), 1) <= _iota((tq, 1), 0)
    (_, acc0), (_, acc1) = tile(qi, carry, causal)
    l0 = acc0[:, one_lane[0]:one_lane[0] + 1]
    l1 = acc1[:, one_lane[1]:one_lane[1] + 1]
    o_ref[...] = jnp.where(low, acc0 / l0, acc1 / l1)


def _fox_prompt(qa, kvab, crow, tq):
    B, S, _ = qa.shape
    half = A_HEADS // 2
    return pl.pallas_call(
        functools.partial(_fox_kernel, tq=tq),
        grid=(B, half, S // tq),
        in_specs=[
            pl.BlockSpec((None, tq, LANES), lambda b, p, qi: (b, qi, p)),
            pl.BlockSpec((None, S, LANES), lambda b, p, qi: (b, 0, p)),
            pl.BlockSpec((None, S, LANES), lambda b, p, qi: (b, 0, half + p)),
            pl.BlockSpec((None, None, 2, S), lambda b, p, qi: (b, p, 0, 0)),
        ],
        out_specs=pl.BlockSpec((None, tq, LANES), lambda b, p, qi: (b, qi, p)),
        out_shape=jax.ShapeDtypeStruct((B, S, A_W), F32),
        compiler_params=_cparams(("parallel", "parallel", "arbitrary")),
        name="fox_prompt",
    )(qa, kvab, kvab, crow)


def _gmlp_kernel(u_ref, vn_ref, ws_ref, bst_ref, o_ref, *, nchunk):
    tril = _iota((CHUNK, CHUNK), 1) <= _iota((CHUNK, CHUNK), 0)
    lane = _iota((1, B_W), 1)
    ws = [jnp.where(tril, ws_ref[h], 0.0).astype(BF16) for h in range(B_GROUPS)]
    for c in range(nchunk):
        rows = slice(c * CHUNK, (c + 1) * CHUNK)
        vn = vn_ref[rows, :].astype(BF16)
        z = jnp.zeros((CHUNK, B_W), F32)
        for h in range(B_GROUPS):
            zh = _dot(ws[h], vn) + bst_ref[:, h:h + 1]
            z = jnp.where(_shr(lane, HEAD_DIM) == h, zh, z)
        o_ref[rows, :] = u_ref[rows, :] * z


def _gmlp(u, vn, ws, bst, nchunk):
    R = u.shape[0]
    tm = nchunk * CHUNK
    row = pl.BlockSpec((tm, B_W), lambda i: (i, 0))
    return pl.pallas_call(
        functools.partial(_gmlp_kernel, nchunk=nchunk),
        grid=(R // tm,),
        in_specs=[row, row, _const_spec((B_GROUPS, CHUNK, CHUNK)), _const_spec((CHUNK, B_GROUPS))],
        out_specs=row,
        out_shape=jax.ShapeDtypeStruct((R, B_W), F32),
        compiler_params=_cparams(("parallel",)),
        name="gmlp",
    )(u, vn, ws, bst)


def _compress_core(x, w, pe_a, pe_b, b1, w2, b2):
    nc = x.shape[0]
    half = w.shape[1] // 2
    h = _dot(x, w)
    const = _dot(pe_a, w[:, :half]) + _dot(pe_b, w[:, half:])
    second = pltpu.roll(h[:, half:], nc - 1, 0)
    hid = _gelu(h[:, :half] + second + const[0:1, :] + b1)
    return _dot(hid.astype(BF16), w2) + b2


def _cmp_prompt_kernel(krows_ref, vrows_ref, wab_ref, pe_ref, b1_ref, w2_ref, b2_ref, o_ref, *, nc):
    for kv, rows_ref in enumerate((krows_ref, vrows_ref)):
        x = jnp.concatenate(
            [rows_ref[pl.ds(l, nc, stride=CMP_STRIDE), :] for l in range(CMP_STRIDE)], axis=-1).astype(BF16)
        o_ref[:, kv * KV_W:(kv + 1) * KV_W] = _compress_core(
            x, wab_ref[kv], pe_ref[kv, 0], pe_ref[kv, 1], b1_ref[kv], w2_ref[kv], b2_ref[kv]).astype(BF16)


def _compress_prompt(cmp_rows, cw):
    B, S, _ = cmp_rows.shape
    nc = S // CMP_STRIDE
    return pl.pallas_call(
        functools.partial(_cmp_prompt_kernel, nc=nc),
        grid=(B,),
        in_specs=[pl.BlockSpec((None, S, KV_W), lambda b: (b, 0, 0)), pl.BlockSpec((None, S, KV_W), lambda b: (b, 0, 1)),
                  _const_spec(cw["wab"].shape), _const_spec(cw["pe"].shape), _const_spec(cw["b1t"].shape),
                  _const_spec(cw["w2bd"].shape), _const_spec(cw["b2t"].shape)],
        out_specs=pl.BlockSpec((None, nc, 2 * KV_W), lambda b: (b, 0, 0)),
        out_shape=jax.ShapeDtypeStruct((B, nc, 2 * KV_W), BF16),
        compiler_params=_cparams(("parallel",)),
        name="compress_prompt",
    )(cmp_rows, cmp_rows, cw["wab"], cw["pe"], cw["b1t"], cw["w2bd"], cw["b2t"])


def _rank_select_t(score_t, topn):
    n = score_t.shape[0]
    j = _iota((n, 1), 0)
    rank = jnp.zeros(score_t.shape, F32)
    for jp in range(n):
        row = score_t[jp:jp + 1, :]
        beats = (row > score_t) | ((row == score_t) & (j > jp))
        rank = rank + jnp.where(beats, 1.0, 0.0)
    return jnp.where(rank < topn, 1.0, 0.0)


def _attend_tiles(qg, tiles, carry):
    m, acc = carry
    s_parts = []
    for k, _, bias, mask in tiles:
        s = _dot_nt(qg, k)
        if bias is not None:
            s = s + bias
        s_parts.append(jnp.where(mask, s, NEG))
    m_new = m
    for s in s_parts:
        m_new = jnp.maximum(m_new, jnp.max(s, axis=-1, keepdims=True))
    acc = jnp.exp(m - m_new) * acc
    for s, (_, v, _, _) in zip(s_parts, tiles):
        acc = acc + _dot(jnp.exp(s - m_new).astype(BF16), v)
    return m_new, acc


def _nsa_kernel(q_ref, kv_ref, cmp_ref, misc_ref, tb_ref, o_ref, *, nbp, nsb):
    qi = pl.program_id(1)
    tq = Q_BLOCK
    nrow = C_REP * tq
    t0 = qi * tq
    trow = t0 + _iota((tq, 1), 0)
    trow3 = jnp.concatenate([trow] * C_REP, axis=0)
    tl3 = trow3 - t0
    sl = _iota((1, tq), 1)
    causal3 = sl <= tl3
    topn = min(SLC_TOPN, nsb)
    lane = _iota((1, LANES), 1)
    low = lane < HEAD_DIM
    per = SLC_BLOCK // CMP_STRIDE
    pool_t = jnp.where(_shr(_iota((nsb, nbp), 1), per) == _iota((nsb, nbp), 0), 1.0, 0.0)
    heads = [None] * C_HEADS

    for g in range(C_KV):
        mine = low if g == 0 else jnp.logical_not(low)
        one_lane = HEAD_DIM if g == 0 else 0
        ones = jnp.where(lane == one_lane, 1.0, 0.0).astype(BF16)

        def v_aug(v):
            return jnp.where(mine, v, jnp.broadcast_to(ones, v.shape))

        parts = []
        for r in range(C_REP):
            h = C_REP * g + r
            src = q_ref[:, (h // 2) * LANES:(h // 2 + 1) * LANES].astype(F32)
            if h % 2 != g:
                src = pltpu.roll(src, HEAD_DIM, 1)
            parts.append(jnp.where(mine, src, 0.0))
        qg = jnp.concatenate(parts, axis=0).astype(BF16)

        s = _dot_nt(qg, cmp_ref[:, 0:KV_W])
        valid = (_iota((1, nbp), 1) * CMP_STRIDE + (CMP_LEN - 1)) <= trow3
        s = jnp.where(valid, s, NEG)
        e = jnp.exp(s - jnp.max(s, axis=-1, keepdims=True))
        prob = jnp.where(valid, e / jnp.sum(e, axis=-1, keepdims=True), 0.0)
        o_cmp = _dot(prob.astype(BF16), cmp_ref[:, KV_W:2 * KV_W])

        psum = prob[0:tq] + prob[tq:2 * tq] + prob[2 * tq:3 * tq]
        imp_t = _dot_nt(pool_t, psum, HIGHEST)
        j = _iota((nsb, 1), 0)
        tcol = t0 + _iota((1, tq), 1)
        cur = _shr(tcol, SLC_BLOCK)
        forced = (j == 0) | (j == cur) | (j == cur - 1)
        score_t = jnp.where(forced, FORCE, jnp.where(j * SLC_BLOCK <= tcol, imp_t, -1.0))
        sel = _rank_select_t(score_t, topn).T.astype(BF16)

        def sel_mask(base, width):
            blk = base // SLC_BLOCK + _shr(_iota((nsb, width), 1), SLC_BLOCK)
            expand = jnp.where(_iota((nsb, width), 0) == blk, 1.0, 0.0).astype(BF16)
            m1 = _dot(sel, expand)
            return jnp.concatenate([m1] * C_REP, axis=0) > 0.5

        bias_diag = tb_ref[g, 0]
        bias_prev = tb_ref[g, 1]
        ks_lo, vs_lo, kw_lo, vw_lo = 2 * KV_W, 3 * KV_W, 4 * KV_W, 5 * KV_W

        def kv_tile(base, width, k_lo, v_lo):
            k = kv_ref[pl.ds(base, width), k_lo:k_lo + KV_W]
            v = v_aug(kv_ref[pl.ds(base, width), v_lo:v_lo + KV_W])
            return k, v

        init = (jnp.full((nrow, 1), M_INIT, F32), jnp.zeros((nrow, LANES), F32))

        n_far = jnp.maximum(qi - 1, 0)
        n_big = n_far // 4

        def big_body(i, carry):
            base = pl.multiple_of(i * (4 * tq), 4 * tq)
            k, v = kv_tile(base, 4 * tq, ks_lo, vs_lo)
            return _attend_tiles(qg, [(k, v, None, sel_mask(base, 4 * tq))], carry)

        def small_body(i, carry):
            base = pl.multiple_of((n_big * 4 + i) * tq, tq)
            k, v = kv_tile(base, tq, ks_lo, vs_lo)
            return _attend_tiles(qg, [(k, v, None, sel_mask(base, tq))], carry)

        carry = lax.fori_loop(0, n_big, big_body, init)
        carry = lax.fori_loop(0, n_far - 4 * n_big, small_body, carry)
        pbase = pl.multiple_of(jnp.maximum(qi - 1, 0) * tq, tq)
        dbase = pl.multiple_of(t0, tq)
        has_prev = qi >= 1
        kp, vp = kv_tile(pbase, tq, ks_lo, vs_lo)
        kd, vd = kv_tile(dbase, tq, ks_lo, vs_lo)
        _, acc = _attend_tiles(qg, [(kp, vp, bias_prev, sel_mask(pbase, tq) & has_prev),
                                    (kd, vd, bias_diag, sel_mask(dbase, tq) & causal3)], carry)
        o_slc = acc / acc[:, one_lane:one_lane + 1]

        far_w = WINDOW - tq
        fbase = pl.multiple_of(jnp.maximum(t0 - WINDOW, 0), tq)
        kpos = fbase + _iota((1, far_w), 1)
        far_mask = (kpos < t0 - tq) & (trow3 - kpos <= WINDOW)
        kf, vf = kv_tile(fbase, far_w, kw_lo, vw_lo)
        carry = _attend_tiles(qg, [(kf, vf, None, far_mask)], init)
        kp, vp = kv_tile(pbase, tq, kw_lo, vw_lo)
        kd, vd = kv_tile(dbase, tq, kw_lo, vw_lo)
        _, acc = _attend_tiles(qg, [(kp, vp, bias_prev, (tl3 >= 0) & has_prev),
                                    (kd, vd, bias_diag, causal3)], carry)
        o_win = acc / acc[:, one_lane:one_lane + 1]

        for r in range(C_REP):
            h = C_REP * g + r
            rows = slice(r * tq, (r + 1) * tq)
            gate = lambda br: misc_ref[:, GATE_OFF + br * C_HEADS + h:GATE_OFF + br * C_HEADS + h + 1]
            heads[h] = gate(0) * o_cmp[rows] + gate(1) * o_slc[rows] + gate(2) * o_win[rows]

    for pair in range(C_HEADS // 2):
        halves = []
        for h in (2 * pair, 2 * pair + 1):
            x = heads[h]
            if h // C_REP != h % 2:
                x = pltpu.roll(x, HEAD_DIM, 1)
            halves.append(x)
        o_ref[:, pair * LANES:(pair + 1) * LANES] = jnp.where(low, halves[0], halves[1])


def _nsa_prompt(qc, kvcb, kcv, misc, tb):
    B, S, _ = qc.shape
    nbp = kcv.shape[1]
    nsb = nbp * CMP_STRIDE // SLC_BLOCK
    return pl.pallas_call(
        functools.partial(_nsa_kernel, nbp=nbp, nsb=nsb),
        grid=(B, S // Q_BLOCK),
        in_specs=[pl.BlockSpec((None, Q_BLOCK, C_W), lambda b, i: (b, i, 0)),
                  pl.BlockSpec((None, S, 6 * KV_W), lambda b, i: (b, 0, 0)),
                  pl.BlockSpec((None, nbp, 2 * KV_W), lambda b, i: (b, 0, 0)),
                  pl.BlockSpec((None, Q_BLOCK, 128), lambda b, i: (b, i, 0)),
                  _const_spec(tb.shape)],
        out_specs=pl.BlockSpec((None, Q_BLOCK, C_W), lambda b, i: (b, i, 0)),
        out_shape=jax.ShapeDtypeStruct((B, S, C_W), F32),
        compiler_params=_cparams(("parallel", "arbitrary")),
        name="nsa_prompt",
    )(qc, kvcb, kcv, misc, tb)


def _mix_ffn_kernel(a_ref, b_ref, c_ref, x_ref, ga_ref, gb_ref, gc_ref, wo_ref, gpm_ref, gpf_ref,
                    win_ref, wout_ref, gpo_ref, o_ref, *, nsplit):
    an = _rms(a_ref[...], ga_ref[...]).astype(BF16)
    bn = _rms(b_ref[...], gb_ref[...]).astype(BF16)
    cn = _rms(c_ref[...], gc_ref[...]).astype(BF16)
    mix = (_dot(an, wo_ref[0:A_W, :]) + _dot(bn, wo_ref[A_W:A_W + B_W, :])
           + _dot(cn, wo_ref[A_W + B_W:A_W + B_W + C_W, :]))
    x1 = x_ref[...] + _rms(mix, gpm_ref[...])
    h = _rms(x1, gpf_ref[...]).astype(BF16)
    wid = D_FF // nsplit
    y = jnp.zeros(x1.shape, F32)
    for c in range(nsplit):
        lo = c * wid
        gate = _dot(h, win_ref[:, lo:lo + wid])
        up = _dot(h, win_ref[:, D_FF + lo:D_FF + lo + wid])
        act = gate * _sigmoid(gate) * up
        y = y + _dot(act.astype(BF16), wout_ref[lo:lo + wid, :])
    o_ref[...] = x1 + _rms(y, gpo_ref[...])


def _mix_ffn(a, b, c, x, lw, tm):
    R = x.shape[0]
    row = lambda n: pl.BlockSpec((tm, n), lambda i: (i, 0))
    once = lambda shape: pl.BlockSpec(shape, lambda i: (0, 0), pipeline_mode=pl.Buffered(1))
    return pl.pallas_call(
        functools.partial(_mix_ffn_kernel, nsplit=2),
        grid=(R // tm,),
        in_specs=[row(A_W), row(B_W), row(C_W), row(D_MODEL),
                  once((1, A_W)), once((1, B_W)), once((1, C_W)), once((D_MODEL, D_MODEL)),
                  once((1, D_MODEL)), once((1, D_MODEL)), once((D_MODEL, 2 * D_FF)), once((D_FF, D_MODEL)),
                  once((1, D_MODEL))],
        out_specs=row(D_MODEL),
        out_shape=jax.ShapeDtypeStruct((R, D_MODEL), F32),
        compiler_params=_cparams(("parallel",)),
        name="mix_ffn",
    )(a, b, c, x, lw["ga"], lw["gb"], lw["gc"], lw["w_o"], lw["gpm"], lw["gpf"], lw["w_ffn_in"],
      lw["w_ffn_out"], lw["gpo"])


def _head_expand(width, owner):
    return jnp.where(owner(_iota((HEAD_PAD, width), 1)) == _iota((HEAD_PAD, width), 0), 1.0, 0.0)


def _spread(x, expand):
    return _dot_exact(jnp.broadcast_to(x, (SUBLANES, HEAD_PAD)), expand)


def _fold_rows(x):
    n, w = x.shape
    return x.reshape(n // SUBLANES, SUBLANES, w).sum(axis=0)


def _col_softmax_step(s, m_prev):
    m_new = jnp.maximum(m_prev, jnp.max(s, axis=0, keepdims=True))
    return m_new, jnp.exp(m_prev - m_new), jnp.exp(s - m_new)


def _fox_dec_kernel(pt_ref, qbd_ref, *refs, npp):
    pages = refs[:npp]
    c_ref, new_ref, cnew_ref, o_ref, m_sc, l_sc, acc_sc = refs[npp:]
    j = pl.program_id(1)
    page = pages[0].shape[0]
    expand = _head_expand(A_W, lambda lane: _shr(lane, HEAD_DIM))
    expand_b = expand.astype(BF16)
    qbd = qbd_ref[...]

    @pl.when(j == 0)
    def _():
        m_sc[...] = jnp.full(m_sc.shape, M_INIT, F32)
        l_sc[...] = jnp.zeros(l_sc.shape, F32)
        acc_sc[...] = jnp.zeros(acc_sc.shape, F32)

    s = jnp.concatenate([_dot(pg[:, 0:A_W], qbd) for pg in pages], axis=0) - c_ref[...]
    m_new, alpha, p = _col_softmax_step(s, m_sc[...])
    l_sc[...] = alpha * l_sc[...] + jnp.sum(p, axis=0, keepdims=True)
    pb = p.astype(BF16)
    contrib = jnp.zeros((SUBLANES, A_W), F32)
    for i, pg in enumerate(pages):
        pexp = _dot(pb[i * page:(i + 1) * page], expand_b)
        contrib = contrib + _fold_rows(pexp * pg[:, A_W:2 * A_W].astype(F32))
    acc_sc[...] = _spread(alpha, expand) * acc_sc[...] + contrib
    m_sc[...] = m_new

    @pl.when(j == pl.num_programs(1) - 1)
    def _():
        kn = jnp.broadcast_to(new_ref[:, 0:A_W].astype(BF16), (SUBLANES, A_W))
        sn = _dot(kn, qbd)[0:1] - cnew_ref[...]
        m_new, alpha, pn = _col_softmax_step(sn, m_sc[...])
        l = alpha * l_sc[...] + pn
        first = _iota((SUBLANES, 1), 0) == 0
        acc = (_spread(alpha, expand) * acc_sc[...]
               + jnp.where(first, _spread(pn, expand) * new_ref[:, A_W:2 * A_W], 0.0))
        o_ref[...] = jnp.sum(acc, axis=0, keepdims=True) / _spread(l, expand)[0:1]


def _fox_decode(layer, page_table, qbd, pool, c_past, kva_new, c_new, npp):
    nb, npages = page_table.shape
    page = pool.shape[2]

    def page_spec(i):
        return pl.BlockSpec((None, None, page, 2 * A_W), lambda b, j, pt: (layer, pt[b, j * npp + i], 0, 0))

    grid_spec = pltpu.PrefetchScalarGridSpec(
        num_scalar_prefetch=1,
        grid=(nb, npages // npp),
        in_specs=[pl.BlockSpec((None, A_W, HEAD_PAD), lambda b, j, pt: (b, 0, 0))]
        + [page_spec(i) for i in range(npp)]
        + [pl.BlockSpec((None, npp * page, HEAD_PAD), lambda b, j, pt: (b, j, 0)),
           pl.BlockSpec((None, 1, 2 * A_W), lambda b, j, pt: (b, 0, 0)),
           pl.BlockSpec((None, 1, HEAD_PAD), lambda b, j, pt: (b, 0, 0))],
        out_specs=pl.BlockSpec((None, 1, A_W), lambda b, j, pt: (b, 0, 0)),
        scratch_shapes=[pltpu.VMEM((1, HEAD_PAD), F32), pltpu.VMEM((1, HEAD_PAD), F32),
                        pltpu.VMEM((SUBLANES, A_W), F32)],
    )
    return pl.pallas_call(
        functools.partial(_fox_dec_kernel, npp=npp),
        grid_spec=grid_spec,
        out_shape=jax.ShapeDtypeStruct((nb, 1, A_W), F32),
        compiler_params=_cparams(("parallel", "arbitrary")),
        name="fox_decode",
    )(page_table, qbd, *([pool] * npp), c_past, kva_new, c_new)


def _cmp_dec_kernel(pt_ref, *refs, npp, nc, nsb_all):
    pages = refs[:npp]
    w_ref, pe_ref, b1_ref, w2_ref, b2_ref, q_ref, ocmp_ref, sel_ref, x_sc = refs[npp:]
    j = pl.program_id(1)
    per_page = pages[0].shape[1]
    for i, pg in enumerate(pages):
        row0 = pl.multiple_of((j * npp + i) * per_page, per_page)
        for c in range(2 * C_KV):
            x_sc[c, pl.ds(row0, per_page), :] = pg[c]

    @pl.when(j == pl.num_programs(1) - 1)
    def _():
        kc = [[_compress_core(x_sc[kv * C_KV + g].astype(BF16), w_ref[kv], pe_ref[kv, 0], pe_ref[kv, 1],
                              b1_ref[kv], w2_ref[kv], b2_ref[kv]).astype(BF16)
               for g in range(C_KV)] for kv in range(2)]
        q = q_ref[...]
        grp0 = _iota((HEAD_PAD, 1), 0) < C_REP
        pick = lambda x0, x1: jnp.where(grp0, x0, x1)
        s = pick(_dot_nt(q, kc[0][0]), _dot_nt(q, kc[0][1]))
        valid = _iota((1, nc), 1) < nc - 1
        s = jnp.where(valid, s, NEG)
        e = jnp.exp(s - jnp.max(s, axis=-1, keepdims=True))
        prob = jnp.where(valid, e / jnp.sum(e, axis=-1, keepdims=True), 0.0)
        pb = prob.astype(BF16)
        ocmp_ref[...] = pick(_dot(pb, kc[1][0]), _dot(pb, kc[1][1]))

        per = SLC_BLOCK // CMP_STRIDE
        nsbp = sel_ref.shape[1]
        hrow = _iota((HEAD_PAD, 1), 0)
        psum = jnp.concatenate(
            [jnp.sum(jnp.where((hrow >= C_REP * g) & (hrow < C_REP * (g + 1)), prob, 0.0), axis=0, keepdims=True)
             for g in range(C_KV)] + [jnp.zeros((HEAD_PAD - C_KV, nc), F32)], axis=0)
        pool = jnp.where(_shr(_iota((nc, nsbp), 0), per) == _iota((nc, nsbp), 1), 1.0, 0.0)
        imp = _dot_exact(psum, pool)
        jl = _iota((1, nsbp), 1)
        cur = nsb_all - 1
        forced = (jl == 0) | (jl == cur) | (jl == cur - 1)
        score = jnp.where(forced, FORCE, jnp.where(jl <= cur, imp, -2.0))
        topn = min(SLC_TOPN, nsb_all)
        rank = jnp.zeros(score.shape, F32)
        for jp in range(nsb_all):
            col = score[:, jp:jp + 1]
            beats = (col > score) | ((col == score) & (jp < jl))
            rank = rank + jnp.where(beats, 1.0, 0.0)
        sel_ref[...] = jnp.where(rank < topn, 1.0, 0.0)


def _cmp_decode(layer, page_table, pool, cw, qh, npp):
    nb, npages = page_table.shape
    per_page, width = pool.shape[3], pool.shape[4]
    nc = npages * per_page
    nsb_all = nc * CMP_STRIDE // SLC_BLOCK + 1
    nsbp = -(-nsb_all // LANES) * LANES

    def page_spec(i):
        return pl.BlockSpec((None, None, 2 * C_KV, per_page, width),
                            lambda b, j, pt: (layer, pt[b, j * npp + i], 0, 0, 0))

    const = lambda a: pl.BlockSpec(a.shape, lambda b, j, pt: (0,) * a.ndim)
    grid_spec = pltpu.PrefetchScalarGridSpec(
        num_scalar_prefetch=1,
        grid=(nb, npages // npp),
        in_specs=[page_spec(i) for i in range(npp)]
        + [const(cw["w"]), const(cw["pe"]), const(cw["b1"]), const(cw["w2"]), const(cw["b2"]),
           pl.BlockSpec((None, HEAD_PAD, HEAD_DIM), lambda b, j, pt: (b, 0, 0))],
        out_specs=[pl.BlockSpec((None, HEAD_PAD, HEAD_DIM), lambda b, j, pt: (b, 0, 0)),
                   pl.BlockSpec((None, HEAD_PAD, nsbp), lambda b, j, pt: (b, 0, 0))],
        scratch_shapes=[pltpu.VMEM((2 * C_KV, nc, width), F32)],
    )
    return pl.pallas_call(
        functools.partial(_cmp_dec_kernel, npp=npp, nc=nc, nsb_all=nsb_all),
        grid_spec=grid_spec,
        out_shape=[jax.ShapeDtypeStruct((nb, HEAD_PAD, HEAD_DIM), F32),
                   jax.ShapeDtypeStruct((nb, HEAD_PAD, nsbp), F32)],
        compiler_params=_cparams(("parallel", "arbitrary")),
        name="cmp_decode",
    )(page_table, *([pool] * npp), cw["w"], cw["pe"], cw["b1"], cw["w2"], cw["b2"], qh)


def _slc_dec_kernel(plist_ref, lpage_ref, cnt_ref, flag_ref, qbd_ref, *refs, npp, nslot, nflag, last_page):
    pages = refs[:npp]
    (nbias_ref, fbias_ref, win_ref, wbias_ref, b0_ref, new_ref, gate_ref, ocmp_ref, o_ref,
     m_sc, l_sc, acc_sc) = refs[npp:]
    b = pl.program_id(0)
    j = pl.program_id(1)
    page = pages[0].shape[0]
    width = C_REP * LANES
    owner = lambda lane: C_REP * (_shr(lane, HEAD_DIM) & 1) + _shr(lane, LANES)
    expand = _head_expand(width, owner)
    expand_b = expand.astype(BF16)
    qbd = qbd_ref[...]
    hl = _iota((1, HEAD_PAD), 1)
    grp0 = hl < C_REP

    @pl.when(j == 0)
    def _():
        m_sc[...] = jnp.full(m_sc.shape, M_INIT, F32)
        l_sc[...] = jnp.zeros(l_sc.shape, F32)
        acc_sc[...] = jnp.zeros(acc_sc.shape, F32)

    def weighted_v(pb, v):
        pexp = _dot(pb, expand_b)
        vf = v.astype(F32)
        return jnp.concatenate([_fold_rows(pexp[:, r * LANES:(r + 1) * LANES] * vf) for r in range(C_REP)], axis=1)

    pos = _iota((page, 1), 0)
    nblk = page // SLC_BLOCK
    s_parts, masks = [], []
    for i, pg in enumerate(pages):
        slot = j * npp + i
        lp = lpage_ref[b * nslot + slot]
        live = slot < cnt_ref[b]
        mask = jnp.zeros((page, HEAD_PAD), jnp.int32)
        for blk in range(nblk):
            f = [flag_ref[(b * C_KV + g) * nflag + lp * nblk + blk] for g in range(C_KV)]
            mask = jnp.where(_shr(pos, SLC_BLOCK) == blk, jnp.where(grp0, f[0], f[1]), mask)
        mask = (mask > 0) & live
        bias = jnp.where(lp == last_page, nbias_ref[...], fbias_ref[...])
        s_parts.append(jnp.where(mask, _dot(pg[:, 0:KV_W], qbd) + bias, NEG))
        masks.append(mask)
    s = jnp.concatenate(s_parts, axis=0)
    m_new, alpha, p = _col_softmax_step(s, m_sc[...])
    l_sc[...] = alpha * l_sc[...] + jnp.sum(p, axis=0, keepdims=True)
    pb = p.astype(BF16)
    contrib = jnp.zeros((SUBLANES, width), F32)
    for i, pg in enumerate(pages):
        contrib = contrib + weighted_v(pb[i * page:(i + 1) * page], pg[:, KV_W:2 * KV_W])
    acc_sc[...] = _spread(alpha, expand) * acc_sc[...] + contrib
    m_sc[...] = m_new

    @pl.when(j == pl.num_programs(1) - 1)
    def _():
        first = _iota((SUBLANES, 1), 0) == 0
        tile3 = lambda row: jnp.concatenate([row] * C_REP, axis=1)
        new_logit = lambda row: _dot(jnp.broadcast_to(row.astype(BF16), (SUBLANES, KV_W)), qbd)[0:1] + b0_ref[...]
        m_new, alpha, pn = _col_softmax_step(new_logit(new_ref[0:1, :]), m_sc[...])
        l = alpha * l_sc[...] + pn
        acc = (_spread(alpha, expand) * acc_sc[...]
               + jnp.where(first, _spread(pn, expand) * tile3(new_ref[1:2, :]), 0.0))
        o_slc = jnp.sum(acc, axis=0, keepdims=True) / _spread(l, expand)[0:1]
        sw = _dot(win_ref[:, 0:KV_W], qbd) + wbias_ref[...]
        sn = new_logit(new_ref[2:3, :])
        m = jnp.maximum(jnp.max(sw, axis=0, keepdims=True), sn)
        pw = jnp.exp(sw - m)
        pn = jnp.exp(sn - m)
        l = jnp.sum(pw, axis=0, keepdims=True) + pn
        acc = (weighted_v(pw.astype(BF16), win_ref[:, KV_W:2 * KV_W])
               + jnp.where(first, _spread(pn, expand) * tile3(new_ref[3:4, :]), 0.0))
        o_win = jnp.sum(acc, axis=0, keepdims=True) / _spread(l, expand)[0:1]
        o_ref[...] = gate_ref[0:1, :] * ocmp_ref[...] + gate_ref[1:2, :] * o_slc + gate_ref[2:3, :] * o_win


def _slc_decode(layer, plist, lpage, cnt, flags, nflag, qbd, pool, nbias, fbias, win, wbias, b0, newrows, gates,
                ocmp, npp, nslot, last_page):
    nb = qbd.shape[0]
    page = pool.shape[2]
    wb = win.shape[2]
    width = C_REP * LANES

    def page_spec(i):
        return pl.BlockSpec((None, None, page, 2 * KV_W),
                            lambda b, j, pls, lps, cn, fl: (layer, pls[b * nslot + j * npp + i], 0, 0))

    fixed = lambda shape: pl.BlockSpec(shape, lambda b, j, pls, lps, cn, fl: (0,) * len(shape))
    per_b = lambda shape: pl.BlockSpec((None,) + shape, lambda b, j, pls, lps, cn, fl: (b,) + (0,) * len(shape))
    grid_spec = pltpu.PrefetchScalarGridSpec(
        num_scalar_prefetch=4,
        grid=(nb, nslot // npp),
        in_specs=[per_b((KV_W, HEAD_PAD))] + [page_spec(i) for i in range(npp)]
        + [fixed((page, HEAD_PAD)), fixed((1, HEAD_PAD)),
           pl.BlockSpec((None, None, wb, 2 * KV_W), lambda b, j, pls, lps, cn, fl: (layer, b, 0, 0)),
           fixed((wb, HEAD_PAD)), fixed((1, HEAD_PAD)), per_b((4, KV_W)), per_b((3, width)), per_b((1, width))],
        out_specs=per_b((1, width)),
        scratch_shapes=[pltpu.VMEM((1, HEAD_PAD), F32), pltpu.VMEM((1, HEAD_PAD), F32),
                        pltpu.VMEM((SUBLANES, width), F32)],
    )
    return pl.pallas_call(
        functools.partial(_slc_dec_kernel, npp=npp, nslot=nslot, nflag=nflag, last_page=last_page),
        grid_spec=grid_spec,
        out_shape=jax.ShapeDtypeStruct((nb, 1, width), F32),
        compiler_params=_cparams(("parallel", "arbitrary")),
        name="slc_win_decode",
    )(plist, lpage, cnt, flags, qbd, *([pool] * npp), nbias, fbias, win, wbias, b0, newrows, gates, ocmp)


def _t5_bucket(dist):
    n = jnp.maximum(dist, 0)
    max_exact = NUM_BUCKETS // 2
    nf = jnp.maximum(n, 1).astype(F32)
    large = max_exact + (jnp.log(nf / max_exact) / math.log(MAX_DISTANCE / max_exact)
                         * (NUM_BUCKETS - max_exact)).astype(jnp.int32)
    return jnp.where(n < max_exact, n, jnp.minimum(large, NUM_BUCKETS - 1))


def _prep_w_in(w_in):
    splits = [int(s) for s in np.cumsum(PROJ_WIDTHS)[:-1]]
    a_q, a_k, a_v, a_f, b_u, b_v, c_q, c_kc, c_vc, c_ks, c_vs, c_kw, c_vw, c_g = jnp.split(w_in, splits, axis=-1)
    pad = jnp.zeros(w_in.shape[:-1] + (LANES - A_HEADS - 3 * C_HEADS,), w_in.dtype)
    return jnp.concatenate([a_q, a_k, a_v, b_u, b_v, c_q, c_kc, c_vc, c_ks, c_vs, c_kw, c_vw, a_f, c_g, pad],
                           axis=-1).astype(BF16)


def _prep_compress(cmp_pe, cmp_w1, cmp_b1, cmp_w2, cmp_b2):
    depth = cmp_w1.shape[0]
    half = CMP_LEN // 2
    eye = jnp.eye(C_KV, dtype=F32)
    w1 = cmp_w1.reshape(depth, 2, 2, half, HEAD_DIM, CMP_HIDDEN)
    wab = jnp.einsum("zkhldj,gG->zklgdhGj", w1, eye).reshape(depth, 2, half * KV_W, 2 * C_KV * CMP_HIDDEN)
    pe = cmp_pe.reshape(depth, 2, 2, half, 1, HEAD_DIM)
    pe_bd = jnp.broadcast_to(pe, (depth, 2, 2, half, C_KV, HEAD_DIM)).reshape(depth, 2, 2, 1, half * KV_W)
    pe_bd = jnp.broadcast_to(pe_bd, (depth, 2, 2, SUBLANES, half * KV_W))
    w2bd = jnp.einsum("zkjd,gG->zkgjGd", cmp_w2, eye).reshape(depth, 2, C_KV * CMP_HIDDEN, KV_W)
    w_g = w1.transpose(0, 1, 3, 4, 2, 5).reshape(depth, 2, half * HEAD_DIM, 2 * CMP_HIDDEN)
    pe_g = jnp.broadcast_to(cmp_pe.reshape(depth, 2, 2, 1, half * HEAD_DIM), (depth, 2, 2, SUBLANES, half * HEAD_DIM))
    prompt = {
        "wab": wab.astype(BF16), "pe": pe_bd.astype(BF16),
        "b1t": jnp.tile(cmp_b1, (1, 1, C_KV))[:, :, None, :],
        "w2bd": w2bd.astype(BF16),
        "b2t": jnp.tile(cmp_b2, (1, 1, C_KV))[:, :, None, :],
    }
    sample = {"w": w_g.astype(BF16), "pe": pe_g.astype(BF16), "b1": cmp_b1[:, :, None, :],
              "w2": cmp_w2.astype(BF16), "b2": cmp_b2[:, :, None, :]}
    return prompt, sample


def _prompt_bias_tiles(table):
    tl = jnp.arange(Q_BLOCK)[:, None]
    sl = jnp.arange(Q_BLOCK)[None, :]
    d = tl - sl
    far = table[_t5_bucket(jnp.asarray(8 * MAX_DISTANCE))]
    tiles = []
    for delta in (0, Q_BLOCK):
        b = table[_t5_bucket(d + delta)] - far
        tiles.append(b.transpose(2, 0, 1).reshape(C_KV, C_REP * Q_BLOCK, Q_BLOCK))
    return jnp.stack(tiles, axis=1).astype(F32)


def _sample_bias(table, past, page, wb):
    pad = lambda x: jnp.pad(x, ((0, 0), (0, HEAD_PAD - C_HEADS))).astype(F32)
    nbias = pad(table[_t5_bucket(page - jnp.arange(page))])
    fbias = pad(table[_t5_bucket(jnp.asarray([8 * MAX_DISTANCE]))])
    wbias = pad(table[_t5_bucket(wb - jnp.arange(wb))])
    b0 = pad(table[0:1])
    return nbias, fbias, wbias, b0


def _block_diag_q(q, nrow_per_group, group_of_head):
    nb, nh, _ = q.shape
    ngrp = nrow_per_group
    grp = jnp.asarray([group_of_head(h) for h in range(nh)])
    onehot = (grp[:, None] == jnp.arange(ngrp)[None, :]).astype(q.dtype)
    out = jnp.einsum("bhd,hg->bgdh", q, onehot).reshape(nb, ngrp * HEAD_DIM, nh)
    return jnp.pad(out, ((0, 0), (0, 0), (0, HEAD_PAD - nh)))


def _to_rgd(x):
    lead = x.shape[:-2]
    x = x.reshape(lead + (C_KV, C_REP, HEAD_DIM))
    return jnp.swapaxes(x, -3, -2).reshape(lead + (C_W,))


def _from_rgd(x):
    lead = x.shape[:-1]
    x = x.reshape(lead + (C_REP, C_KV, HEAD_DIM))
    return jnp.swapaxes(x, -3, -2).reshape(lead + (C_W,))


def _layer_prompt(x, lw, cw, tb, tm_proj, tm_mix, tq_fox):
    B, S, _ = x.shape
    R = B * S
    x2 = x.reshape(R, D_MODEL)
    qa, kva, kvab, bu, vn, qc, cmp_r, slc_r, win_r, kvcb, misc = _proj(
        x2, lw["g_pre"], lw["w_in"], lw["bf"], lw["lng"], lw["lnb"], tm_proj)
    lf = misc[:, :A_HEADS].reshape(B, S, A_HEADS)
    crow = jnp.cumsum(lf, axis=1).transpose(0, 2, 1).reshape(B, A_HEADS // 2, 2, S)
    a_out = _fox_prompt(qa.reshape(B, S, A_W), kvab.reshape(B, S, 2 * A_W), crow, tq_fox)
    b_out = _gmlp(bu, vn, lw["ws"], lw["bst"], min(8, R // CHUNK))
    kcv = _compress_prompt(cmp_r.reshape(B, S, 2 * KV_W), cw)
    c_out = _nsa_prompt(qc.reshape(B, S, C_W), kvcb.reshape(B, S, 6 * KV_W), kcv,
                        misc.reshape(B, S, 128), tb)
    y = _mix_ffn(a_out.reshape(R, A_W), b_out, c_out.reshape(R, C_W), x2, lw, tm_mix)
    wb = min(WINDOW, S)
    kv5 = lambda r: r.reshape(B, S, 2, C_KV, HEAD_DIM)
    states = (kva.reshape(B, S, 2, A_HEADS, HEAD_DIM), lf, kv5(cmp_r), kv5(slc_r), kv5(win_r)[:, S - wb:])
    return y.reshape(B, S, D_MODEL), states


def _layer_sample(layer, x, lw, cw, sb, pools, win_state, page_table):
    nb = x.shape[0]
    fox_pool, logf_pool, cmp_pool, slc_pool, win_buf = pools
    npages = page_table.shape[1]
    page = fox_pool.shape[2]
    past = npages * page
    npp = min(PAGES_PER_STEP, npages)
    x2 = x.reshape(nb, D_MODEL)
    qa, kva, _, bu, vn, qc, cmp_r, slc_r, win_r, _, misc = _proj(
        x2, lw["g_pre"], lw["w_in"], lw["bf"], lw["lng"], lw["lnb"], nb)
    lf = misc[:, :A_HEADS]

    lf_past = logf_pool[layer][page_table].reshape(nb, past, A_HEADS)
    c = jnp.cumsum(jnp.concatenate([lf_past, lf[:, None, :]], axis=1), axis=1)
    c = jnp.pad(c, ((0, 0), (0, 0), (0, HEAD_PAD - A_HEADS)))
    qbd_a = _block_diag_q(qa.reshape(nb, A_HEADS, HEAD_DIM), A_HEADS, lambda h: h)
    a_out = _fox_decode(layer, page_table, qbd_a, fox_pool, c[:, :past], kva.reshape(nb, 1, 2 * A_W),
                        c[:, past:], npp).reshape(nb, A_W)

    pad_chunk = lambda t: jnp.zeros((nb, CHUNK, B_W), F32).at[:, 0].set(t).reshape(nb * CHUNK, B_W)
    b_out = _gmlp(pad_chunk(bu), pad_chunk(vn), lw["ws"], lw["bst"], min(8, nb)).reshape(nb, CHUNK, B_W)[:, 0]

    q6 = qc.reshape(nb, C_HEADS, HEAD_DIM)
    qh = jnp.pad(q6, ((0, 0), (0, HEAD_PAD - C_HEADS), (0, 0)))
    o_cmp, sel = _cmp_decode(layer, page_table, cmp_pool, cw, qh, npp)
    nsb_all = past // SLC_BLOCK + 1
    per_page = page // SLC_BLOCK
    nflag = (npages + 1) * per_page
    flags = jnp.pad(sel[:, :C_KV, :nsb_all] > 0.5, ((0, 0), (0, 0), (0, nflag - nsb_all)))
    need = flags[:, :, :npages * per_page].reshape(nb, C_KV, npages, per_page).any(axis=(1, 3))
    nslot = -(-min(npages, C_KV * SLC_TOPN) // npp) * npp
    pidx = jnp.arange(npages, dtype=jnp.int32)[None, :]
    order = jnp.argsort(jnp.where(need, pidx, pidx + npages), axis=1)[:, :nslot]
    cnt = jnp.sum(need, axis=1).astype(jnp.int32)
    last = jnp.take_along_axis(order, jnp.maximum(cnt - 1, 0)[:, None], axis=1)
    lpage = jnp.where(jnp.arange(nslot)[None, :] < cnt[:, None], order, last).astype(jnp.int32)
    plist = jnp.take_along_axis(page_table, lpage, axis=1)
    qbd_c = _block_diag_q(q6, C_KV, lambda h: h // C_REP)
    newrows = jnp.stack([slc_r[:, :KV_W], slc_r[:, KV_W:], win_r[:, :KV_W], win_r[:, KV_W:]], axis=1)
    g3 = misc[:, GATE_OFF:GATE_OFF + 3 * C_HEADS].reshape(nb, 3, C_HEADS, 1)
    gates = _to_rgd(jnp.broadcast_to(g3, (nb, 3, C_HEADS, HEAD_DIM)))
    nbias, fbias, wbias, b0 = sb
    c_rgd = _slc_decode(layer, plist.reshape(-1), lpage.reshape(-1), cnt, flags.reshape(-1).astype(jnp.int32), nflag,
                        qbd_c, slc_pool, nbias, fbias, win_buf, wbias, b0, newrows, gates,
                        _to_rgd(o_cmp[:, :C_HEADS])[:, None, :], npp, nslot, npages - 1)
    c_out = _from_rgd(c_rgd[:, 0])

    y = _mix_ffn(a_out, b_out, c_out, x2, lw, nb)
    kv5 = lambda r: r.reshape(nb, 1, 2, C_KV, HEAD_DIM)
    win_new = jnp.concatenate([win_state[layer][:, 1:], kv5(win_r)], axis=1)
    states = (kva.reshape(nb, 1, 2, A_HEADS, HEAD_DIM), lf[:, None, :], kv5(cmp_r), kv5(slc_r), win_new,
              vn.reshape(nb, 1, B_GROUPS, HEAD_DIM))
    return y.reshape(nb, 1, D_MODEL), states


def kernel(x_prompt, x_sample, cache_fox_kv, cache_fox_logf, cache_cmp_kv, cache_slc_kv, state_win_kv, page_table, rel_bias_table, norm_pre_mix, w_in, b_forget, gmlp_ln_g, gmlp_ln_b, gmlp_ws, gmlp_bs, cmp_pe, cmp_w1, cmp_b1, cmp_w2, cmp_b2, norm_group_a, norm_group_b, norm_group_c, w_o, norm_post_mix, norm_pre_ffn, w_ffn_in, w_ffn_out, norm_post_ffn):
    depth = w_in.shape[0]
    assert x_sample.shape[1] == 1
    B, S, _ = x_prompt.shape
    nb = x_sample.shape[0]
    n_pool, page = cache_fox_kv.shape[1], cache_fox_kv.shape[2]
    wb = state_win_kv.shape[2]

    w_in_p = _prep_w_in(w_in)
    bf = jnp.pad(b_forget, ((0, 0), (0, LANES - A_HEADS)))[:, None, :]
    cw_prompt, cw_sample = _prep_compress(cmp_pe, cmp_w1, cmp_b1, cmp_w2, cmp_b2)
    tb = _prompt_bias_tiles(rel_bias_table)
    sb = _sample_bias(rel_bias_table, page_table.shape[1] * page, page, wb)
    w_o_b = w_o.astype(BF16)
    w_fi_b = w_ffn_in.astype(BF16)
    w_fo_b = w_ffn_out.astype(BF16)
    bst = gmlp_bs.transpose(0, 2, 1)
    chunks = page // CMP_STRIDE
    cmp_x = cache_cmp_kv.reshape(depth, n_pool, chunks, CMP_STRIDE, 2, C_KV, HEAD_DIM)
    cmp_x = cmp_x.transpose(0, 1, 4, 5, 2, 3, 6).reshape(depth, n_pool, 2 * C_KV, chunks, CMP_STRIDE * HEAD_DIM)
    pools = (cache_fox_kv.reshape(depth, n_pool, page, 2 * A_W).astype(BF16), cache_fox_logf, cmp_x,
             cache_slc_kv.reshape(depth, n_pool, page, 2 * KV_W).astype(BF16),
             state_win_kv.reshape(depth, nb, wb, 2 * KV_W).astype(BF16))

    tm_proj = min(512, B * S)
    tm_mix = min(256, B * S)
    tq_fox = min(512, S)

    xp, xs = x_prompt, x_sample
    sp = [[] for _ in range(5)]
    ss = [[] for _ in range(6)]
    for l in range(depth):
        r1 = lambda a: a[l][None, :]
        lw = {
            "g_pre": r1(norm_pre_mix), "w_in": w_in_p[l], "bf": bf[l], "lng": r1(gmlp_ln_g), "lnb": r1(gmlp_ln_b),
            "ws": gmlp_ws[l], "bst": bst[l], "ga": r1(norm_group_a), "gb": r1(norm_group_b), "gc": r1(norm_group_c),
            "w_o": w_o_b[l], "gpm": r1(norm_post_mix), "gpf": r1(norm_pre_ffn), "w_ffn_in": w_fi_b[l],
            "w_ffn_out": w_fo_b[l], "gpo": r1(norm_post_ffn),
        }
        xp, st_p = _layer_prompt(xp, lw, {k: v[l] for k, v in cw_prompt.items()}, tb, tm_proj, tm_mix, tq_fox)
        xs, st_s = _layer_sample(l, xs, lw, {k: v[l] for k, v in cw_sample.items()}, sb, pools, state_win_kv,
                                 page_table)
        for lst, s in zip(sp, st_p):
            lst.append(s)
        for lst, s in zip(ss, st_s):
            lst.append(s)
    return tuple([xp, xs] + [jnp.stack(s) for s in sp] + [jnp.stack(s) for s in ss])
```

```python
import functools
import math

import numpy as np
import jax
import jax.numpy as jnp
from jax import lax
from jax.experimental import pallas as pl
from jax.experimental.pallas import tpu as pltpu

F32 = jnp.float32
BF16 = jnp.bfloat16
HIGHEST = lax.Precision.HIGHEST

D_MODEL = 1024
HEAD_DIM = 64
A_HEADS = 6
B_GROUPS = 4
C_HEADS = 6
C_KV = 2
C_REP = C_HEADS // C_KV
A_W = A_HEADS * HEAD_DIM
B_W = B_GROUPS * HEAD_DIM
C_W = C_HEADS * HEAD_DIM
KV_W = C_KV * HEAD_DIM
Q_BLOCK = 128
CHUNK = 128
CMP_LEN = 32
CMP_STRIDE = 16
CMP_HIDDEN = 256
SLC_BLOCK = 64
SLC_TOPN = 16
WINDOW = 512
NUM_BUCKETS = 32
MAX_DISTANCE = 128
D_FF = 2816
EPS = 1e-6
NEG = -1e30
M_INIT = 0.5 * NEG
FORCE = 1e4
SCALE = HEAD_DIM ** -0.5
PROJ_WIDTHS = (A_W, A_W, A_W, A_HEADS, B_W, B_W, C_W, KV_W, KV_W, KV_W, KV_W, KV_W, KV_W, 3 * C_HEADS)

LANES = 128
SUBLANES = 8
HEAD_PAD = 8
VMEM_LIMIT = 56 * 1024 * 1024
PAGES_PER_STEP = 8

P_QA = (0, 384)
P_KVA = (384, 1152)
P_BU = (1152, 1408)
P_BV = (1408, 1664)
P_QC = (1664, 2048)
P_CMP = (2048, 2304)
P_SLC = (2304, 2560)
P_WIN = (2560, 2816)
P_MISC = (2816, 2944)
N_PROJ = 2944
GATE_OFF = A_HEADS


def _dot(a, b):
    return jnp.dot(a, b, preferred_element_type=F32)


def _dot_exact(a, b):
    return jnp.dot(a, b, preferred_element_type=F32, precision=HIGHEST)


def _dot_nt(a, b, precision=None):
    return lax.dot_general(a, b, (((1,), (1,)), ((), ())), preferred_element_type=F32, precision=precision)


def _gelu(x):
    return 0.5 * x * (1.0 + jnp.tanh(math.sqrt(2.0 / math.pi) * (x + 0.044715 * (x * x * x))))


def _sigmoid(x):
    return 1.0 / (1.0 + jnp.exp(-x))


def _rms(x, g):
    return x * lax.rsqrt(jnp.mean(x * x, axis=-1, keepdims=True) + EPS) * g


def _iota(shape, dim):
    return lax.broadcasted_iota(jnp.int32, shape, dim)


def _shr(x, n):
    return jnp.right_shift(x, int(math.log2(n)))


def _cparams(sem):
    return pltpu.CompilerParams(dimension_semantics=sem, vmem_limit_bytes=VMEM_LIMIT)


def _const_spec(shape):
    nd = len(shape)
    return pl.BlockSpec(shape, lambda *_: (0,) * nd)


def _proj_kernel(x_ref, g_ref, w_ref, bf_ref, lng_ref, lnb_ref,
                 qa_ref, kva_ref, kvab_ref, bu_ref, vn_ref, qc_ref,
                 cmp_ref, slc_ref, win_ref, kvcb_ref, misc_ref):
    h = _rms(x_ref[...], g_ref[...]).astype(BF16)

    def mm(seg):
        return _dot(h, w_ref[:, seg[0]:seg[1]])

    qa_ref[...] = (mm(P_QA) * SCALE).astype(BF16)
    kva = mm(P_KVA)
    kva_ref[...] = kva
    kvab_ref[...] = kva.astype(BF16)
    bu_ref[...] = _gelu(mm(P_BU))
    v = _gelu(mm(P_BV))
    mu = jnp.mean(v, axis=-1, keepdims=True)
    var = jnp.mean(jnp.square(v - mu), axis=-1, keepdims=True)
    vn_ref[...] = (v - mu) * lax.rsqrt(var + EPS) * lng_ref[...] + lnb_ref[...]
    qc_ref[...] = (mm(P_QC) * SCALE).astype(BF16)
    c = mm(P_CMP)
    cmp_ref[...] = c
    kvcb_ref[:, 0:256] = c.astype(BF16)
    c = mm(P_SLC)
    slc_ref[...] = c
    kvcb_ref[:, 256:512] = c.astype(BF16)
    c = mm(P_WIN)
    win_ref[...] = c
    kvcb_ref[:, 512:768] = c.astype(BF16)
    m = mm(P_MISC) + bf_ref[...]
    lane = _iota(m.shape, 1)
    logsig = jnp.minimum(m, 0.0) - jnp.log1p(jnp.exp(-jnp.abs(m)))
    misc_ref[...] = jnp.where(lane < A_HEADS, logsig, _sigmoid(m))


def _proj(x2d, g, w, bf, lng, lnb, tm):
    R = x2d.shape[0]
    row = lambda n: pl.BlockSpec((tm, n), lambda i: (i, 0))
    outs = [(384, BF16), (768, F32), (768, BF16), (256, F32), (256, F32), (384, BF16),
            (256, F32), (256, F32), (256, F32), (768, BF16), (128, F32)]
    return pl.pallas_call(
        _proj_kernel,
        grid=(R // tm,),
        in_specs=[row(D_MODEL), _const_spec((1, D_MODEL)), _const_spec((D_MODEL, N_PROJ)),
                  _const_spec((1, 128)), _const_spec((1, B_W)), _const_spec((1, B_W))],
        out_specs=[row(n) for n, _ in outs],
        out_shape=[jax.ShapeDtypeStruct((R, n), dt) for n, dt in outs],
        compiler_params=_cparams(("parallel",)),
        name="proj",
    )(x2d, g, w, bf, lng, lnb)


def _fox_kernel(q_ref, k_ref, v_ref, c_ref, o_ref, *, tq):
    qi = pl.program_id(2)
    tk = tq
    q = q_ref[...]
    lane = _iota((1, LANES), 1)
    low = lane < HEAD_DIM
    zero = jnp.zeros_like(q)
    qm = (jnp.where(low, q, zero), jnp.where(low, zero, q))
    one_lane = (HEAD_DIM, 0)

    def v_aug(v, hh):
        mine = low if hh == 0 else jnp.logical_not(low)
        ones = jnp.where(lane == one_lane[hh], 1.0, 0.0).astype(BF16)
        return jnp.where(mine, v, jnp.broadcast_to(ones, v.shape))

    def tile(ki, carry, causal):
        base = pl.multiple_of(ki * tk, tk)
        k = k_ref[pl.ds(base, tk), :]
        v = v_ref[pl.ds(base, tk), :]
        out = []
        for hh in range(2):
            m, acc = carry[hh]
            s = _dot_nt(qm[hh], k) - c_ref[hh:hh + 1, pl.ds(base, tk)]
            if causal is not None:
                s = jnp.where(causal, s, NEG)
            m_new = jnp.maximum(m, jnp.max(s, axis=-1, keepdims=True))
            alpha = jnp.exp(m - m_new)
            p = jnp.exp(s - m_new)
            acc = alpha * acc + _dot(p.astype(BF16), v_aug(v, hh))
            out.append((m_new, acc))
        return tuple(out)

    init = tuple((jnp.full((tq, 1), M_INIT, F32), jnp.zeros((tq, LANES), F32)) for _ in range(2))
    carry = lax.fori_loop(0, qi, lambda ki, c: tile(ki, c, None), init)
    causal = _iota((1, tk), 1) <= _iota((tq, 1), 0)
    (_, acc0), (_, acc1) = tile(qi, carry, causal)
    l0 = acc0[:, one_lane[0]:one_lane[0] + 1]
    l1 = acc1[:, one_lane[1]:one_lane[1] + 1]
    o_ref[...] = jnp.where(low, acc0 * (1.0 / l0), acc1 * (1.0 / l1))


def _fox_prompt(qa, kvab, crow, tq):
    B, S, _ = qa.shape
    half = A_HEADS // 2
    return pl.pallas_call(
        functools.partial(_fox_kernel, tq=tq),
        grid=(B, half, S // tq),
        in_specs=[
            pl.BlockSpec((None, tq, LANES), lambda b, p, qi: (b, qi, p)),
            pl.BlockSpec((None, S, LANES), lambda b, p, qi: (b, 0, p)),
            pl.BlockSpec((None, S, LANES), lambda b, p, qi: (b, 0, half + p)),
            pl.BlockSpec((None, None, 2, S), lambda b, p, qi: (b, p, 0, 0)),
        ],
        out_specs=pl.BlockSpec((None, tq, LANES), lambda b, p, qi: (b, qi, p)),
        out_shape=jax.ShapeDtypeStruct((B, S, A_W), F32),
        compiler_params=_cparams(("parallel", "parallel", "arbitrary")),
        name="fox_prompt",
    )(qa, kvab, kvab, crow)


def _cumsum_kernel(x_ref, o_ref):
    upper = jnp.where(_iota((LANES, LANES), 0) <= _iota((LANES, LANES), 1), 1.0, 0.0)
    carry = jnp.zeros((x_ref.shape[0], 1), F32)
    for i in range(x_ref.shape[1] // LANES):
        lanes = slice(i * LANES, (i + 1) * LANES)
        c = _dot_exact(x_ref[:, lanes], upper) + carry
        o_ref[:, lanes] = c
        carry = c[:, LANES - 1:LANES]


def _cumsum_lanes(x):
    n, h, length = x.shape
    spec = pl.BlockSpec((None, h, length), lambda i: (i, 0, 0))
    return pl.pallas_call(
        _cumsum_kernel, grid=(n,), in_specs=[spec], out_specs=spec,
        out_shape=jax.ShapeDtypeStruct(x.shape, F32),
        compiler_params=_cparams(("parallel",)),
        name="cumsum_logf",
    )(x)


def _gmlp_kernel(u_ref, vn_ref, ws_ref, bst_ref, o_ref, *, nchunk):
    tril = _iota((CHUNK, CHUNK), 1) <= _iota((CHUNK, CHUNK), 0)
    lane = _iota((1, B_W), 1)
    ws = [jnp.where(tril, ws_ref[h], 0.0).astype(BF16) for h in range(B_GROUPS)]
    for c in range(nchunk):
        rows = slice(c * CHUNK, (c + 1) * CHUNK)
        vn = vn_ref[rows, :].astype(BF16)
        z = jnp.zeros((CHUNK, B_W), F32)
        for h in range(B_GROUPS):
            zh = _dot(ws[h], vn) + bst_ref[:, h:h + 1]
            z = jnp.where(_shr(lane, HEAD_DIM) == h, zh, z)
        o_ref[rows, :] = u_ref[rows, :] * z


def _gmlp(u, vn, ws, bst, nchunk):
    R = u.shape[0]
    tm = nchunk * CHUNK
    row = pl.BlockSpec((tm, B_W), lambda i: (i, 0))
    return pl.pallas_call(
        functools.partial(_gmlp_kernel, nchunk=nchunk),
        grid=(R // tm,),
        in_specs=[row, row, _const_spec((B_GROUPS, CHUNK, CHUNK)), _const_spec((CHUNK, B_GROUPS))],
        out_specs=row,
        out_shape=jax.ShapeDtypeStruct((R, B_W), F32),
        compiler_params=_cparams(("parallel",)),
        name="gmlp",
    )(u, vn, ws, bst)


def _compress_core(x, w, pe_a, pe_b, b1, w2, b2):
    nc = x.shape[0]
    half = w.shape[1] // 2
    h = _dot(x, w)
    const = _dot(pe_a, w[:, :half]) + _dot(pe_b, w[:, half:])
    second = pltpu.roll(h[:, half:], nc - 1, 0)
    hid = _gelu(h[:, :half] + second + const[0:1, :] + b1)
    return _dot(hid.astype(BF16), w2) + b2


def _cmp_prompt_kernel(krows_ref, vrows_ref, wab_ref, pe_ref, b1_ref, w2_ref, b2_ref, o_ref, *, nc):
    for kv, rows_ref in enumerate((krows_ref, vrows_ref)):
        x = jnp.concatenate(
            [rows_ref[pl.ds(l, nc, stride=CMP_STRIDE), :] for l in range(CMP_STRIDE)], axis=-1).astype(BF16)
        o_ref[:, kv * KV_W:(kv + 1) * KV_W] = _compress_core(
            x, wab_ref[kv], pe_ref[kv, 0], pe_ref[kv, 1], b1_ref[kv], w2_ref[kv], b2_ref[kv]).astype(BF16)


def _compress_prompt(cmp_rows, cw):
    B, S, _ = cmp_rows.shape
    nc = S // CMP_STRIDE
    return pl.pallas_call(
        functools.partial(_cmp_prompt_kernel, nc=nc),
        grid=(B,),
        in_specs=[pl.BlockSpec((None, S, KV_W), lambda b: (b, 0, 0)), pl.BlockSpec((None, S, KV_W), lambda b: (b, 0, 1)),
                  _const_spec(cw["wab"].shape), _const_spec(cw["pe"].shape), _const_spec(cw["b1t"].shape),
                  _const_spec(cw["w2bd"].shape), _const_spec(cw["b2t"].shape)],
        out_specs=pl.BlockSpec((None, nc, 2 * KV_W), lambda b: (b, 0, 0)),
        out_shape=jax.ShapeDtypeStruct((B, nc, 2 * KV_W), BF16),
        compiler_params=_cparams(("parallel",)),
        name="compress_prompt",
    )(cmp_rows, cmp_rows, cw["wab"], cw["pe"], cw["b1t"], cw["w2bd"], cw["b2t"])


def _rank_select_t(score_t, topn):
    n = score_t.shape[0]
    groups = [score_t[lo:lo + SUBLANES] for lo in range(0, n, SUBLANES)]
    sub = _iota((SUBLANES, 1), 0)
    ranks = [jnp.zeros(g.shape, F32) for g in groups]
    for jp in range(n):
        row = score_t[jp:jp + 1, :]
        for gi, sg in enumerate(groups):
            lo = gi * SUBLANES
            if lo + SUBLANES - 1 <= jp:
                beats = row > sg
            elif lo > jp:
                beats = row >= sg
            else:
                beats = (row > sg) | ((row == sg) & (sub + lo > jp))
            ranks[gi] = ranks[gi] + jnp.where(beats, 1.0, 0.0)
    return jnp.concatenate([jnp.where(r < topn, 1.0, 0.0) for r in ranks], axis=0)


def _attend_tiles(qg, tiles, carry):
    m, acc = carry
    s_parts = []
    for k, _, bias, mask in tiles:
        s = _dot_nt(qg, k)
        if bias is not None:
            s = s + bias
        s_parts.append(jnp.where(mask, s, NEG))
    m_new = m
    for s in s_parts:
        m_new = jnp.maximum(m_new, jnp.max(s, axis=-1, keepdims=True))
    acc = jnp.exp(m - m_new) * acc
    for s, (_, v, _, _) in zip(s_parts, tiles):
        acc = acc + _dot(jnp.exp(s - m_new).astype(BF16), v)
    return m_new, acc


def _nsa_kernel(q_ref, kv_ref, cmp_ref, misc_ref, tb_ref, o_ref, *, nbp, nsb):
    qi = pl.program_id(1)
    tq = Q_BLOCK
    nrow = C_REP * tq
    t0 = qi * tq
    trow = t0 + _iota((tq, 1), 0)
    trow3 = jnp.concatenate([trow] * C_REP, axis=0)
    tl3 = trow3 - t0
    sl = _iota((1, tq), 1)
    causal3 = sl <= tl3
    topn = min(SLC_TOPN, nsb)
    lane = _iota((1, LANES), 1)
    low = lane < HEAD_DIM
    per = SLC_BLOCK // CMP_STRIDE
    pool_t = jnp.where(_shr(_iota((nsb, nbp), 1), per) == _iota((nsb, nbp), 0), 1.0, 0.0)
    groups = range(C_KV)
    mine = [low, jnp.logical_not(low)]
    one_lane = [HEAD_DIM, 0]
    ones = [jnp.where(lane == one_lane[g], 1.0, 0.0).astype(BF16) for g in groups]
    ks_lo, vs_lo, kw_lo, vw_lo = 2 * KV_W, 3 * KV_W, 4 * KV_W, 5 * KV_W

    def kv_tile(base, width, k_lo, v_lo):
        k = kv_ref[pl.ds(base, width), k_lo:k_lo + KV_W]
        v = kv_ref[pl.ds(base, width), v_lo:v_lo + KV_W]
        return k, [jnp.where(mine[g], v, jnp.broadcast_to(ones[g], v.shape)) for g in groups]

    qg, o_cmp, sel = [], [], []
    for g in groups:
        parts = []
        for r in range(C_REP):
            h = C_REP * g + r
            src = q_ref[:, (h // 2) * LANES:(h // 2 + 1) * LANES].astype(F32)
            if h % 2 != g:
                src = pltpu.roll(src, HEAD_DIM, 1)
            parts.append(jnp.where(mine[g], src, 0.0))
        q = jnp.concatenate(parts, axis=0).astype(BF16)
        qg.append(q)

        s = _dot_nt(q, cmp_ref[:, 0:KV_W])
        valid = (_iota((1, nbp), 1) * CMP_STRIDE + (CMP_LEN - 1)) <= trow3
        s = jnp.where(valid, s, NEG)
        e = jnp.exp(s - jnp.max(s, axis=-1, keepdims=True))
        prob = jnp.where(valid, e * (1.0 / jnp.sum(e, axis=-1, keepdims=True)), 0.0)
        o_cmp.append(_dot(prob.astype(BF16), cmp_ref[:, KV_W:2 * KV_W]))

        psum = prob[0:tq] + prob[tq:2 * tq] + prob[2 * tq:3 * tq]
        imp_t = _dot_nt(pool_t, psum, HIGHEST)
        j = _iota((nsb, 1), 0)
        tcol = t0 + _iota((1, tq), 1)
        cur = _shr(tcol, SLC_BLOCK)
        forced = (j == 0) | (j == cur) | (j == cur - 1)
        score_t = jnp.where(forced, FORCE, jnp.where(j * SLC_BLOCK <= tcol, imp_t, -1.0))
        sel.append(_rank_select_t(score_t, topn).T.astype(BF16))

    def sel_masks(base, width):
        blk = base // SLC_BLOCK + _shr(_iota((nsb, width), 1), SLC_BLOCK)
        expand = jnp.where(_iota((nsb, width), 0) == blk, 1.0, 0.0).astype(BF16)
        return [jnp.concatenate([_dot(sel[g], expand)] * C_REP, axis=0) > 0.5 for g in groups]

    init = (jnp.full((nrow, 1), M_INIT, F32), jnp.zeros((nrow, LANES), F32))

    def far_step(base, width, carry):
        k, v = kv_tile(base, width, ks_lo, vs_lo)
        masks = sel_masks(base, width)
        return tuple(_attend_tiles(qg[g], [(k, v[g], None, masks[g])], carry[g]) for g in groups)

    n_far = jnp.maximum(qi - 1, 0)
    n_big = n_far // 4
    carry = lax.fori_loop(0, n_big, lambda i, c: far_step(pl.multiple_of(i * (4 * tq), 4 * tq), 4 * tq, c),
                          (init, init))
    carry = lax.fori_loop(0, n_far - 4 * n_big,
                          lambda i, c: far_step(pl.multiple_of((n_big * 4 + i) * tq, tq), tq, c), carry)

    pbase = pl.multiple_of(jnp.maximum(qi - 1, 0) * tq, tq)
    dbase = pl.multiple_of(t0, tq)
    has_prev = qi >= 1
    far_w = WINDOW - tq
    fbase = pl.multiple_of(jnp.maximum(t0 - WINDOW, 0), tq)
    kpos = fbase + _iota((1, far_w), 1)
    far_mask = (kpos < t0 - tq) & (trow3 - kpos <= WINDOW)
    mask_p = sel_masks(pbase, tq)
    mask_d = sel_masks(dbase, tq)
    ksp, vsp = kv_tile(pbase, tq, ks_lo, vs_lo)
    ksd, vsd = kv_tile(dbase, tq, ks_lo, vs_lo)
    kwf, vwf = kv_tile(fbase, far_w, kw_lo, vw_lo)
    kwp, vwp = kv_tile(pbase, tq, kw_lo, vw_lo)
    kwd, vwd = kv_tile(dbase, tq, kw_lo, vw_lo)
    heads = [None] * C_HEADS
    for g in groups:
        bias_diag = tb_ref[g, 0]
        bias_prev = tb_ref[g, 1]
        _, acc = _attend_tiles(qg[g], [(ksp, vsp[g], bias_prev, mask_p[g] & has_prev),
                                       (ksd, vsd[g], bias_diag, mask_d[g] & causal3)], carry[g])
        o_slc = acc * (1.0 / acc[:, one_lane[g]:one_lane[g] + 1])
        cw = _attend_tiles(qg[g], [(kwf, vwf[g], None, far_mask)], init)
        _, acc = _attend_tiles(qg[g], [(kwp, vwp[g], bias_prev, (tl3 >= 0) & has_prev),
                                       (kwd, vwd[g], bias_diag, causal3)], cw)
        o_win = acc * (1.0 / acc[:, one_lane[g]:one_lane[g] + 1])
        for r in range(C_REP):
            h = C_REP * g + r
            rows = slice(r * tq, (r + 1) * tq)
            gate = lambda br: misc_ref[:, GATE_OFF + br * C_HEADS + h:GATE_OFF + br * C_HEADS + h + 1]
            heads[h] = gate(0) * o_cmp[g][rows] + gate(1) * o_slc[rows] + gate(2) * o_win[rows]

    for pair in range(C_HEADS // 2):
        halves = []
        for h in (2 * pair, 2 * pair + 1):
            x = heads[h]
            if h // C_REP != h % 2:
                x = pltpu.roll(x, HEAD_DIM, 1)
            halves.append(x)
        o_ref[:, pair * LANES:(pair + 1) * LANES] = jnp.where(low, halves[0], halves[1])


def _nsa_prompt(qc, kvcb, kcv, misc, tb):
    B, S, _ = qc.shape
    nbp = kcv.shape[1]
    nsb = nbp * CMP_STRIDE // SLC_BLOCK
    return pl.pallas_call(
        functools.partial(_nsa_kernel, nbp=nbp, nsb=nsb),
        grid=(B, S // Q_BLOCK),
        in_specs=[pl.BlockSpec((None, Q_BLOCK, C_W), lambda b, i: (b, i, 0)),
                  pl.BlockSpec((None, S, 6 * KV_W), lambda b, i: (b, 0, 0)),
                  pl.BlockSpec((None, nbp, 2 * KV_W), lambda b, i: (b, 0, 0)),
                  pl.BlockSpec((None, Q_BLOCK, 128), lambda b, i: (b, i, 0)),
                  _const_spec(tb.shape)],
        out_specs=pl.BlockSpec((None, Q_BLOCK, C_W), lambda b, i: (b, i, 0)),
        out_shape=jax.ShapeDtypeStruct((B, S, C_W), F32),
        compiler_params=_cparams(("parallel", "arbitrary")),
        name="nsa_prompt",
    )(qc, kvcb, kcv, misc, tb)


def _mix_ffn_kernel(a_ref, b_ref, c_ref, x_ref, ga_ref, gb_ref, gc_ref, wo_ref, gpm_ref, gpf_ref,
                    win_ref, wout_ref, gpo_ref, o_ref, *, nsplit):
    an = _rms(a_ref[...], ga_ref[...]).astype(BF16)
    bn = _rms(b_ref[...], gb_ref[...]).astype(BF16)
    cn = _rms(c_ref[...], gc_ref[...]).astype(BF16)
    mix = (_dot(an, wo_ref[0:A_W, :]) + _dot(bn, wo_ref[A_W:A_W + B_W, :])
           + _dot(cn, wo_ref[A_W + B_W:A_W + B_W + C_W, :]))
    x1 = x_ref[...] + _rms(mix, gpm_ref[...])
    h = _rms(x1, gpf_ref[...]).astype(BF16)
    wid = D_FF // nsplit
    y = jnp.zeros(x1.shape, F32)
    for c in range(nsplit):
        lo = c * wid
        gate = _dot(h, win_ref[:, lo:lo + wid])
        up = _dot(h, win_ref[:, D_FF + lo:D_FF + lo + wid])
        act = gate * _sigmoid(gate) * up
        y = y + _dot(act.astype(BF16), wout_ref[lo:lo + wid, :])
    o_ref[...] = x1 + _rms(y, gpo_ref[...])


def _mix_ffn(a, b, c, x, lw, tm):
    R = x.shape[0]
    row = lambda n: pl.BlockSpec((tm, n), lambda i: (i, 0))
    once = lambda shape: pl.BlockSpec(shape, lambda i: (0, 0), pipeline_mode=pl.Buffered(1))
    return pl.pallas_call(
        functools.partial(_mix_ffn_kernel, nsplit=2),
        grid=(R // tm,),
        in_specs=[row(A_W), row(B_W), row(C_W), row(D_MODEL),
                  once((1, A_W)), once((1, B_W)), once((1, C_W)), once((D_MODEL, D_MODEL)),
                  once((1, D_MODEL)), once((1, D_MODEL)), once((D_MODEL, 2 * D_FF)), once((D_FF, D_MODEL)),
                  once((1, D_MODEL))],
        out_specs=row(D_MODEL),
        out_shape=jax.ShapeDtypeStruct((R, D_MODEL), F32),
        compiler_params=_cparams(("parallel",)),
        name="mix_ffn",
    )(a, b, c, x, lw["ga"], lw["gb"], lw["gc"], lw["w_o"], lw["gpm"], lw["gpf"], lw["w_ffn_in"],
      lw["w_ffn_out"], lw["gpo"])


def _head_expand(width, owner):
    return jnp.where(owner(_iota((HEAD_PAD, width), 1)) == _iota((HEAD_PAD, width), 0), 1.0, 0.0)


def _spread(x, expand):
    return _dot_exact(jnp.broadcast_to(x, (SUBLANES, HEAD_PAD)), expand)


def _fold_rows(x):
    n, w = x.shape
    return x.reshape(n // SUBLANES, SUBLANES, w).sum(axis=0)


def _col_softmax_step(s, m_prev):
    m_new = jnp.maximum(m_prev, jnp.max(s, axis=0, keepdims=True))
    return m_new, jnp.exp(m_prev - m_new), jnp.exp(s - m_new)


def _fox_dec_kernel(pt_ref, qall_ref, *refs, npp):
    pages = refs[:npp]
    c_ref, new_ref, cnew_ref, o_ref, m_sc, l_sc, acc_sc = refs[npp:]
    j = pl.program_id(1)
    page = pages[0].shape[0]
    slot = lambda x, h: x[:, h * LANES:h * LANES + HEAD_DIM]
    expand = _head_expand(HEAD_PAD * LANES, lambda lane: _shr(lane, LANES))
    expand_b = expand.astype(BF16)
    qall = qall_ref[...]
    col = _iota((page, HEAD_PAD), 1)

    @pl.when(j == 0)
    def _():
        m_sc[...] = jnp.full(m_sc.shape, M_INIT, F32)
        l_sc[...] = jnp.zeros(l_sc.shape, F32)
        acc_sc[...] = jnp.zeros(acc_sc.shape, F32)

    s_parts = []
    for pg in pages:
        k_heads = pltpu.einshape("shd->hsd", pg[:, 0])
        r = _dot(k_heads.reshape(A_HEADS * page, HEAD_DIM).astype(BF16), qall)
        s = jnp.zeros((page, HEAD_PAD), F32)
        for h in range(A_HEADS):
            s = s + jnp.where(col == h, r[h * page:(h + 1) * page], 0.0)
        s_parts.append(s)
    s = jnp.concatenate(s_parts, axis=0) - c_ref[...]
    m_new, alpha, p = _col_softmax_step(s, m_sc[...])
    l_sc[...] = alpha * l_sc[...] + jnp.sum(p, axis=0, keepdims=True)
    pb = p.astype(BF16)
    contrib = [jnp.zeros((SUBLANES, HEAD_DIM), F32) for _ in range(A_HEADS)]
    for i, pg in enumerate(pages):
        v_heads = pltpu.einshape("shd->hsd", pg[:, 1])
        pexp = _dot(pb[i * page:(i + 1) * page], expand_b)
        for h in range(A_HEADS):
            contrib[h] = contrib[h] + _fold_rows(slot(pexp, h) * v_heads[h])
    a_exp = _spread(alpha, expand)
    for h in range(A_HEADS):
        acc_sc[h] = slot(a_exp, h) * acc_sc[h] + contrib[h]
    m_sc[...] = m_new

    @pl.when(j == pl.num_programs(1) - 1)
    def _():
        r = _dot(new_ref[0].astype(BF16), qall)
        diag = _iota((HEAD_PAD, HEAD_PAD), 0) == _iota((HEAD_PAD, HEAD_PAD), 1)
        sn = jnp.sum(jnp.where(diag, r, 0.0), axis=0, keepdims=True) - cnew_ref[...]
        m_new, alpha, pn = _col_softmax_step(sn, m_sc[...])
        a_exp = _spread(alpha, expand)
        p_exp = _spread(pn, expand)
        l_inv = 1.0 / _spread(alpha * l_sc[...] + pn, expand)
        o_ref[...] = jnp.zeros(o_ref.shape, F32)
        for h in range(A_HEADS):
            num = (jnp.sum(slot(a_exp, h) * acc_sc[h], axis=0, keepdims=True)
                   + slot(p_exp, h)[0:1] * new_ref[1, h:h + 1, :])
            o_ref[h:h + 1, :] = num * slot(l_inv, h)[0:1]


def _fox_decode(layer, page_table, qall, pool, c_past, kv_new, c_new, npp):
    nb, npages = page_table.shape
    page = pool.shape[2]

    def page_spec(i):
        return pl.BlockSpec((None, None, page, 2, A_HEADS, HEAD_DIM),
                            lambda b, j, pt: (layer, pt[b, j * npp + i], 0, 0, 0, 0))

    grid_spec = pltpu.PrefetchScalarGridSpec(
        num_scalar_prefetch=1,
        grid=(nb, npages // npp),
        in_specs=[pl.BlockSpec((None, HEAD_DIM, HEAD_PAD), lambda b, j, pt: (b, 0, 0))]
        + [page_spec(i) for i in range(npp)]
        + [pl.BlockSpec((None, npp * page, HEAD_PAD), lambda b, j, pt: (b, j, 0)),
           pl.BlockSpec((None, 2, HEAD_PAD, HEAD_DIM), lambda b, j, pt: (b, 0, 0, 0)),
           pl.BlockSpec((None, 1, HEAD_PAD), lambda b, j, pt: (b, 0, 0))],
        out_specs=pl.BlockSpec((None, HEAD_PAD, HEAD_DIM), lambda b, j, pt: (b, 0, 0)),
        scratch_shapes=[pltpu.VMEM((1, HEAD_PAD), F32), pltpu.VMEM((1, HEAD_PAD), F32),
                        pltpu.VMEM((A_HEADS, SUBLANES, HEAD_DIM), F32)],
    )
    return pl.pallas_call(
        functools.partial(_fox_dec_kernel, npp=npp),
        grid_spec=grid_spec,
        out_shape=jax.ShapeDtypeStruct((nb, HEAD_PAD, HEAD_DIM), F32),
        compiler_params=_cparams(("parallel", "arbitrary")),
        name="fox_decode",
    )(page_table, qall, *([pool] * npp), c_past, kv_new, c_new)


def _cmp_dec_kernel(pt_ref, *refs, npp, nc, nsb_all):
    pages = refs[:npp]
    w_ref, pe_ref, b1_ref, w2_ref, b2_ref, q_ref, ocmp_ref, sel_ref, x_sc = refs[npp:]
    j = pl.program_id(1)
    per_page = pages[0].shape[1]
    for i, pg in enumerate(pages):
        row0 = pl.multiple_of((j * npp + i) * per_page, per_page)
        for c in range(2 * C_KV):
            x_sc[c, pl.ds(row0, per_page), :] = pg[c]

    @pl.when(j == pl.num_programs(1) - 1)
    def _():
        kc = [[_compress_core(x_sc[kv * C_KV + g].astype(BF16), w_ref[kv], pe_ref[kv, 0], pe_ref[kv, 1],
                              b1_ref[kv], w2_ref[kv], b2_ref[kv]).astype(BF16)
               for g in range(C_KV)] for kv in range(2)]
        q = q_ref[...]
        grp0 = _iota((HEAD_PAD, 1), 0) < C_REP
        pick = lambda x0, x1: jnp.where(grp0, x0, x1)
        s = pick(_dot_nt(q, kc[0][0]), _dot_nt(q, kc[0][1]))
        valid = _iota((1, nc), 1) < nc - 1
        s = jnp.where(valid, s, NEG)
        e = jnp.exp(s - jnp.max(s, axis=-1, keepdims=True))
        prob = jnp.where(valid, e / jnp.sum(e, axis=-1, keepdims=True), 0.0)
        pb = prob.astype(BF16)
        ocmp_ref[...] = pick(_dot(pb, kc[1][0]), _dot(pb, kc[1][1]))

        per = SLC_BLOCK // CMP_STRIDE
        nsbp = sel_ref.shape[1]
        hrow = _iota((HEAD_PAD, 1), 0)
        psum = jnp.concatenate(
            [jnp.sum(jnp.where((hrow >= C_REP * g) & (hrow < C_REP * (g + 1)), prob, 0.0), axis=0, keepdims=True)
             for g in range(C_KV)] + [jnp.zeros((HEAD_PAD - C_KV, nc), F32)], axis=0)
        pool = jnp.where(_shr(_iota((nc, nsbp), 0), per) == _iota((nc, nsbp), 1), 1.0, 0.0)
        imp = _dot_exact(psum, pool)
        jl = _iota((1, nsbp), 1)
        cur = nsb_all - 1
        forced = (jl == 0) | (jl == cur) | (jl == cur - 1)
        score = jnp.where(forced, FORCE, jnp.where(jl <= cur, imp, -2.0))
        topn = min(SLC_TOPN, nsb_all)
        rank = jnp.zeros(score.shape, F32)
        for jp in range(nsb_all):
            col = score[:, jp:jp + 1]
            beats = (col > score) | ((col == score) & (jp < jl))
            rank = rank + jnp.where(beats, 1.0, 0.0)
        sel_ref[...] = jnp.where(rank < topn, 1.0, 0.0)


def _cmp_decode(layer, page_table, pool, cw, qh, npp):
    nb, npages = page_table.shape
    per_page, width = pool.shape[3], pool.shape[4]
    nc = npages * per_page
    nsb_all = nc * CMP_STRIDE // SLC_BLOCK + 1
    nsbp = -(-nsb_all // LANES) * LANES

    def page_spec(i):
        return pl.BlockSpec((None, None, 2 * C_KV, per_page, width),
                            lambda b, j, pt: (layer, pt[b, j * npp + i], 0, 0, 0))

    const = lambda a: pl.BlockSpec(a.shape, lambda b, j, pt: (0,) * a.ndim)
    grid_spec = pltpu.PrefetchScalarGridSpec(
        num_scalar_prefetch=1,
        grid=(nb, npages // npp),
        in_specs=[page_spec(i) for i in range(npp)]
        + [const(cw["w"]), const(cw["pe"]), const(cw["b1"]), const(cw["w2"]), const(cw["b2"]),
           pl.BlockSpec((None, HEAD_PAD, HEAD_DIM), lambda b, j, pt: (b, 0, 0))],
        out_specs=[pl.BlockSpec((None, HEAD_PAD, HEAD_DIM), lambda b, j, pt: (b, 0, 0)),
                   pl.BlockSpec((None, HEAD_PAD, nsbp), lambda b, j, pt: (b, 0, 0))],
        scratch_shapes=[pltpu.VMEM((2 * C_KV, nc, width), F32)],
    )
    return pl.pallas_call(
        functools.partial(_cmp_dec_kernel, npp=npp, nc=nc, nsb_all=nsb_all),
        grid_spec=grid_spec,
        out_shape=[jax.ShapeDtypeStruct((nb, HEAD_PAD, HEAD_DIM), F32),
                   jax.ShapeDtypeStruct((nb, HEAD_PAD, nsbp), F32)],
        compiler_params=_cparams(("parallel", "arbitrary")),
        name="cmp_decode",
    )(page_table, *([pool] * npp), cw["w"], cw["pe"], cw["b1"], cw["w2"], cw["b2"], qh)


def _slc_dec_kernel(plist_ref, lpage_ref, cnt_ref, flag_ref, qbd_ref, *refs, npp, nslot, nflag, last_page):
    pages = refs[:npp]
    (nbias_ref, fbias_ref, win_ref, wbias_ref, b0_ref, new_ref, gate_ref, ocmp_ref, o_ref,
     m_sc, l_sc, acc_sc) = refs[npp:]
    b = pl.program_id(0)
    j = pl.program_id(1)
    page = pages[0].shape[0]
    width = C_REP * LANES
    owner = lambda lane: C_REP * (_shr(lane, HEAD_DIM) & 1) + _shr(lane, LANES)
    expand = _head_expand(width, owner)
    expand_b = expand.astype(BF16)
    qbd = qbd_ref[...]
    hl = _iota((1, HEAD_PAD), 1)
    grp0 = hl < C_REP

    @pl.when(j == 0)
    def _():
        m_sc[...] = jnp.full(m_sc.shape, M_INIT, F32)
        l_sc[...] = jnp.zeros(l_sc.shape, F32)
        acc_sc[...] = jnp.zeros(acc_sc.shape, F32)

    def weighted_v(pb, v):
        pexp = _dot(pb, expand_b)
        return jnp.concatenate([_fold_rows(pexp[:, r * LANES:(r + 1) * LANES] * v) for r in range(C_REP)], axis=1)

    pos = _iota((page, 1), 0)
    nblk = page // SLC_BLOCK
    s_parts = []
    for i, pg in enumerate(pages):
        slot = j * npp + i
        lp = lpage_ref[b * nslot + slot]
        live = slot < cnt_ref[b]
        mask = jnp.zeros((page, HEAD_PAD), jnp.int32)
        for blk in range(nblk):
            f = [flag_ref[(b * C_KV + g) * nflag + lp * nblk + blk] for g in range(C_KV)]
            mask = jnp.where(_shr(pos, SLC_BLOCK) == blk, jnp.where(grp0, f[0], f[1]), mask)
        mask = (mask > 0) & live
        bias = jnp.where(lp == last_page, nbias_ref[...], fbias_ref[...])
        s_parts.append(jnp.where(mask, _dot(pg[:, 0:KV_W].astype(BF16), qbd) + bias, NEG))
    s = jnp.concatenate(s_parts, axis=0)
    m_new, alpha, p = _col_softmax_step(s, m_sc[...])
    l_sc[...] = alpha * l_sc[...] + jnp.sum(p, axis=0, keepdims=True)
    pb = p.astype(BF16)
    contrib = jnp.zeros((SUBLANES, width), F32)
    for i, pg in enumerate(pages):
        contrib = contrib + weighted_v(pb[i * page:(i + 1) * page], pg[:, KV_W:2 * KV_W])
    acc_sc[...] = _spread(alpha, expand) * acc_sc[...] + contrib
    m_sc[...] = m_new

    @pl.when(j == pl.num_programs(1) - 1)
    def _():
        first = _iota((SUBLANES, 1), 0) == 0
        tile3 = lambda row: jnp.concatenate([row] * C_REP, axis=1)
        new_logit = lambda row: _dot(jnp.broadcast_to(row.astype(BF16), (SUBLANES, KV_W)), qbd)[0:1] + b0_ref[...]
        m_new, alpha, pn = _col_softmax_step(new_logit(new_ref[0:1, :]), m_sc[...])
        l = alpha * l_sc[...] + pn
        acc = (_spread(alpha, expand) * acc_sc[...]
               + jnp.where(first, _spread(pn, expand) * tile3(new_ref[1:2, :]), 0.0))
        o_slc = jnp.sum(acc, axis=0, keepdims=True) / _spread(l, expand)[0:1]
        sw = _dot(win_ref[:, 0:KV_W].astype(BF16), qbd) + wbias_ref[...]
        sn = new_logit(new_ref[2:3, :])
        m = jnp.maximum(jnp.max(sw, axis=0, keepdims=True), sn)
        pw = jnp.exp(sw - m)
        pn = jnp.exp(sn - m)
        l = jnp.sum(pw, axis=0, keepdims=True) + pn
        acc = (weighted_v(pw.astype(BF16), win_ref[:, KV_W:2 * KV_W])
               + jnp.where(first, _spread(pn, expand) * tile3(new_ref[3:4, :]), 0.0))
        o_win = jnp.sum(acc, axis=0, keepdims=True) / _spread(l, expand)[0:1]
        o_ref[...] = gate_ref[0:1, :] * ocmp_ref[...] + gate_ref[1:2, :] * o_slc + gate_ref[2:3, :] * o_win


def _slc_decode(layer, plist, lpage, cnt, flags, nflag, qbd, pool, nbias, fbias, win, wbias, b0, newrows, gates,
                ocmp, npp, nslot, last_page):
    nb = qbd.shape[0]
    page = pool.shape[2]
    wb = win.shape[2]
    width = C_REP * LANES

    def page_spec(i):
        return pl.BlockSpec((None, None, page, 2 * KV_W),
                            lambda b, j, pls, lps, cn, fl: (layer, pls[b * nslot + j * npp + i], 0, 0))

    fixed = lambda shape: pl.BlockSpec(shape, lambda b, j, pls, lps, cn, fl: (0,) * len(shape))
    per_b = lambda shape: pl.BlockSpec((None,) + shape, lambda b, j, pls, lps, cn, fl: (b,) + (0,) * len(shape))
    grid_spec = pltpu.PrefetchScalarGridSpec(
        num_scalar_prefetch=4,
        grid=(nb, nslot // npp),
        in_specs=[per_b((KV_W, HEAD_PAD))] + [page_spec(i) for i in range(npp)]
        + [fixed((page, HEAD_PAD)), fixed((1, HEAD_PAD)),
           pl.BlockSpec((None, None, wb, 2 * KV_W), lambda b, j, pls, lps, cn, fl: (layer, b, 0, 0)),
           fixed((wb, HEAD_PAD)), fixed((1, HEAD_PAD)), per_b((4, KV_W)), per_b((3, width)), per_b((1, width))],
        out_specs=per_b((1, width)),
        scratch_shapes=[pltpu.VMEM((1, HEAD_PAD), F32), pltpu.VMEM((1, HEAD_PAD), F32),
                        pltpu.VMEM((SUBLANES, width), F32)],
    )
    return pl.pallas_call(
        functools.partial(_slc_dec_kernel, npp=npp, nslot=nslot, nflag=nflag, last_page=last_page),
        grid_spec=grid_spec,
        out_shape=jax.ShapeDtypeStruct((nb, 1, width), F32),
        compiler_params=_cparams(("parallel", "arbitrary")),
        name="slc_win_decode",
    )(plist, lpage, cnt, flags, qbd, *([pool] * npp), nbias, fbias, win, wbias, b0, newrows, gates, ocmp)


def _t5_bucket(dist):
    n = jnp.maximum(dist, 0)
    max_exact = NUM_BUCKETS // 2
    nf = jnp.maximum(n, 1).astype(F32)
    large = max_exact + (jnp.log(nf / max_exact) / math.log(MAX_DISTANCE / max_exact)
                         * (NUM_BUCKETS - max_exact)).astype(jnp.int32)
    return jnp.where(n < max_exact, n, jnp.minimum(large, NUM_BUCKETS - 1))


def _prep_w_in(w_in):
    splits = [int(s) for s in np.cumsum(PROJ_WIDTHS)[:-1]]
    a_q, a_k, a_v, a_f, b_u, b_v, c_q, c_kc, c_vc, c_ks, c_vs, c_kw, c_vw, c_g = jnp.split(w_in, splits, axis=-1)
    pad = jnp.zeros(w_in.shape[:-1] + (LANES - A_HEADS - 3 * C_HEADS,), w_in.dtype)
    return jnp.concatenate([a_q, a_k, a_v, b_u, b_v, c_q, c_kc, c_vc, c_ks, c_vs, c_kw, c_vw, a_f, c_g, pad],
                           axis=-1).astype(BF16)


def _prep_compress(cmp_pe, cmp_w1, cmp_b1, cmp_w2, cmp_b2):
    depth = cmp_w1.shape[0]
    half = CMP_LEN // 2
    eye = jnp.eye(C_KV, dtype=F32)
    w1 = cmp_w1.reshape(depth, 2, 2, half, HEAD_DIM, CMP_HIDDEN)
    wab = jnp.einsum("zkhldj,gG->zklgdhGj", w1, eye).reshape(depth, 2, half * KV_W, 2 * C_KV * CMP_HIDDEN)
    pe = cmp_pe.reshape(depth, 2, 2, half, 1, HEAD_DIM)
    pe_bd = jnp.broadcast_to(pe, (depth, 2, 2, half, C_KV, HEAD_DIM)).reshape(depth, 2, 2, 1, half * KV_W)
    pe_bd = jnp.broadcast_to(pe_bd, (depth, 2, 2, SUBLANES, half * KV_W))
    w2bd = jnp.einsum("zkjd,gG->zkgjGd", cmp_w2, eye).reshape(depth, 2, C_KV * CMP_HIDDEN, KV_W)
    w_g = w1.transpose(0, 1, 3, 4, 2, 5).reshape(depth, 2, half * HEAD_DIM, 2 * CMP_HIDDEN)
    pe_g = jnp.broadcast_to(cmp_pe.reshape(depth, 2, 2, 1, half * HEAD_DIM), (depth, 2, 2, SUBLANES, half * HEAD_DIM))
    prompt = {
        "wab": wab.astype(BF16), "pe": pe_bd.astype(BF16),
        "b1t": jnp.tile(cmp_b1, (1, 1, C_KV))[:, :, None, :],
        "w2bd": w2bd.astype(BF16),
        "b2t": jnp.tile(cmp_b2, (1, 1, C_KV))[:, :, None, :],
    }
    sample = {"w": w_g.astype(BF16), "pe": pe_g.astype(BF16), "b1": cmp_b1[:, :, None, :],
              "w2": cmp_w2.astype(BF16), "b2": cmp_b2[:, :, None, :]}
    return prompt, sample


def _prompt_bias_tiles(table):
    tl = jnp.arange(Q_BLOCK)[:, None]
    sl = jnp.arange(Q_BLOCK)[None, :]
    d = tl - sl
    far = table[_t5_bucket(jnp.asarray(8 * MAX_DISTANCE))]
    tiles = []
    for delta in (0, Q_BLOCK):
        b = table[_t5_bucket(d + delta)] - far
        tiles.append(b.transpose(2, 0, 1).reshape(C_KV, C_REP * Q_BLOCK, Q_BLOCK))
    return jnp.stack(tiles, axis=1).astype(F32)


def _sample_bias(table, past, page, wb):
    pad = lambda x: jnp.pad(x, ((0, 0), (0, HEAD_PAD - C_HEADS))).astype(F32)
    nbias = pad(table[_t5_bucket(page - jnp.arange(page))])
    fbias = pad(table[_t5_bucket(jnp.asarray([8 * MAX_DISTANCE]))])
    wbias = pad(table[_t5_bucket(wb - jnp.arange(wb))])
    b0 = pad(table[0:1])
    return nbias, fbias, wbias, b0


def _block_diag_q(q, nrow_per_group, group_of_head):
    nb, nh, _ = q.shape
    ngrp = nrow_per_group
    grp = jnp.asarray([group_of_head(h) for h in range(nh)])
    onehot = (grp[:, None] == jnp.arange(ngrp)[None, :]).astype(q.dtype)
    out = jnp.einsum("bhd,hg->bgdh", q, onehot).reshape(nb, ngrp * HEAD_DIM, nh)
    return jnp.pad(out, ((0, 0), (0, 0), (0, HEAD_PAD - nh)))


def _to_rgd(x):
    lead = x.shape[:-2]
    x = x.reshape(lead + (C_KV, C_REP, HEAD_DIM))
    return jnp.swapaxes(x, -3, -2).reshape(lead + (C_W,))


def _from_rgd(x):
    lead = x.shape[:-1]
    x = x.reshape(lead + (C_REP, C_KV, HEAD_DIM))
    return jnp.swapaxes(x, -3, -2).reshape(lead + (C_W,))


def _layer_prompt(x, lw, cw, tb, tm_proj, tm_mix, tq_fox):
    B, S, _ = x.shape
    R = B * S
    x2 = x.reshape(R, D_MODEL)
    qa, kva, kvab, bu, vn, qc, cmp_r, slc_r, win_r, kvcb, misc = _proj(
        x2, lw["g_pre"], lw["w_in"], lw["bf"], lw["lng"], lw["lnb"], tm_proj)
    lf = misc[:, :A_HEADS].reshape(B, S, A_HEADS)
    lf_t = jnp.pad(lf, ((0, 0), (0, 0), (0, HEAD_PAD - A_HEADS))).transpose(0, 2, 1)
    crow = _cumsum_lanes(lf_t)[:, :A_HEADS].reshape(B, A_HEADS // 2, 2, S)
    a_out = _fox_prompt(qa.reshape(B, S, A_W), kvab.reshape(B, S, 2 * A_W), crow, tq_fox)
    b_out = _gmlp(bu, vn, lw["ws"], lw["bst"], min(8, R // CHUNK))
    kcv = _compress_prompt(cmp_r.reshape(B, S, 2 * KV_W), cw)
    c_out = _nsa_prompt(qc.reshape(B, S, C_W), kvcb.reshape(B, S, 6 * KV_W), kcv,
                        misc.reshape(B, S, 128), tb)
    y = _mix_ffn(a_out.reshape(R, A_W), b_out, c_out.reshape(R, C_W), x2, lw, tm_mix)
    wb = min(WINDOW, S)
    kv5 = lambda r: r.reshape(B, S, 2, C_KV, HEAD_DIM)
    states = (kva.reshape(B, S, 2, A_HEADS, HEAD_DIM), lf, kv5(cmp_r), kv5(slc_r), kv5(win_r)[:, S - wb:])
    return y.reshape(B, S, D_MODEL), states


def _layer_sample(layer, x, lw, cw, sb, pools, win_state, page_table):
    nb = x.shape[0]
    fox_pool, logf_pool, cmp_pool, slc_pool, win_buf = pools
    npages = page_table.shape[1]
    page = fox_pool.shape[2]
    past = npages * page
    npp = min(PAGES_PER_STEP, npages)
    x2 = x.reshape(nb, D_MODEL)
    qa, kva, _, bu, vn, qc, cmp_r, slc_r, win_r, _, misc = _proj(
        x2, lw["g_pre"], lw["w_in"], lw["bf"], lw["lng"], lw["lnb"], nb)
    lf = misc[:, :A_HEADS]

    lf_past = logf_pool[layer][page_table].reshape(nb, past, A_HEADS)
    lf_all = jnp.concatenate([lf_past, lf[:, None, :]], axis=1)
    lf_all = jnp.pad(lf_all, ((0, 0), (0, LANES - 1), (0, HEAD_PAD - A_HEADS)))
    c = _cumsum_lanes(lf_all.transpose(0, 2, 1)).transpose(0, 2, 1)
    head_pad = lambda t: jnp.pad(t, [(0, 0)] * (t.ndim - 2) + [(0, HEAD_PAD - A_HEADS), (0, 0)])
    q_all = jnp.swapaxes(head_pad(qa.reshape(nb, A_HEADS, HEAD_DIM)), 1, 2)
    a_out = _fox_decode(layer, page_table, q_all, fox_pool, c[:, :past],
                        head_pad(kva.reshape(nb, 2, A_HEADS, HEAD_DIM)), c[:, past:past + 1], npp)
    a_out = a_out[:, :A_HEADS].reshape(nb, A_W)

    pad_chunk = lambda t: jnp.zeros((nb, CHUNK, B_W), F32).at[:, 0].set(t).reshape(nb * CHUNK, B_W)
    b_out = _gmlp(pad_chunk(bu), pad_chunk(vn), lw["ws"], lw["bst"], min(8, nb)).reshape(nb, CHUNK, B_W)[:, 0]

    q6 = qc.reshape(nb, C_HEADS, HEAD_DIM)
    qh = jnp.pad(q6, ((0, 0), (0, HEAD_PAD - C_HEADS), (0, 0)))
    o_cmp, sel = _cmp_decode(layer, page_table, cmp_pool, cw, qh, npp)
    nsb_all = past // SLC_BLOCK + 1
    per_page = page // SLC_BLOCK
    nflag = (npages + 1) * per_page
    flags = jnp.pad(sel[:, :C_KV, :nsb_all] > 0.5, ((0, 0), (0, 0), (0, nflag - nsb_all)))
    need = flags[:, :, :npages * per_page].reshape(nb, C_KV, npages, per_page).any(axis=(1, 3))
    nslot = -(-min(npages, C_KV * SLC_TOPN) // npp) * npp
    pidx = jnp.arange(npages, dtype=jnp.int32)[None, :]
    order = jnp.argsort(jnp.where(need, pidx, pidx + npages), axis=1)[:, :nslot]
    cnt = jnp.sum(need, axis=1).astype(jnp.int32)
    last = jnp.take_along_axis(order, jnp.maximum(cnt - 1, 0)[:, None], axis=1)
    lpage = jnp.where(jnp.arange(nslot)[None, :] < cnt[:, None], order, last).astype(jnp.int32)
    plist = jnp.take_along_axis(page_table, lpage, axis=1)
    qbd_c = _block_diag_q(q6, C_KV, lambda h: h // C_REP)
    newrows = jnp.stack([slc_r[:, :KV_W], slc_r[:, KV_W:], win_r[:, :KV_W], win_r[:, KV_W:]], axis=1)
    g3 = misc[:, GATE_OFF:GATE_OFF + 3 * C_HEADS].reshape(nb, 3, C_HEADS, 1)
    gates = _to_rgd(jnp.broadcast_to(g3, (nb, 3, C_HEADS, HEAD_DIM)))
    nbias, fbias, wbias, b0 = sb
    c_rgd = _slc_decode(layer, plist.reshape(-1), lpage.reshape(-1), cnt, flags.reshape(-1).astype(jnp.int32), nflag,
                        qbd_c, slc_pool, nbias, fbias, win_buf, wbias, b0, newrows, gates,
                        _to_rgd(o_cmp[:, :C_HEADS])[:, None, :], npp, nslot, npages - 1)
    c_out = _from_rgd(c_rgd[:, 0])

    y = _mix_ffn(a_out, b_out, c_out, x2, lw, nb)
    kv5 = lambda r: r.reshape(nb, 1, 2, C_KV, HEAD_DIM)
    win_new = jnp.concatenate([win_state[layer][:, 1:], kv5(win_r)], axis=1)
    states = (kva.reshape(nb, 1, 2, A_HEADS, HEAD_DIM), lf[:, None, :], kv5(cmp_r), kv5(slc_r), win_new,
              vn.reshape(nb, 1, B_GROUPS, HEAD_DIM))
    return y.reshape(nb, 1, D_MODEL), states


def kernel(x_prompt, x_sample, cache_fox_kv, cache_fox_logf, cache_cmp_kv, cache_slc_kv, state_win_kv, page_table, rel_bias_table, norm_pre_mix, w_in, b_forget, gmlp_ln_g, gmlp_ln_b, gmlp_ws, gmlp_bs, cmp_pe, cmp_w1, cmp_b1, cmp_w2, cmp_b2, norm_group_a, norm_group_b, norm_group_c, w_o, norm_post_mix, norm_pre_ffn, w_ffn_in, w_ffn_out, norm_post_ffn):
    depth = w_in.shape[0]
    assert x_sample.shape[1] == 1
    B, S, _ = x_prompt.shape
    nb = x_sample.shape[0]
    n_pool, page = cache_fox_kv.shape[1], cache_fox_kv.shape[2]
    wb = state_win_kv.shape[2]

    w_in_p = _prep_w_in(w_in)
    bf = jnp.pad(b_forget, ((0, 0), (0, LANES - A_HEADS)))[:, None, :]
    cw_prompt, cw_sample = _prep_compress(cmp_pe, cmp_w1, cmp_b1, cmp_w2, cmp_b2)
    tb = _prompt_bias_tiles(rel_bias_table)
    sb = _sample_bias(rel_bias_table, page_table.shape[1] * page, page, wb)
    w_o_b = w_o.astype(BF16)
    w_fi_b = w_ffn_in.astype(BF16)
    w_fo_b = w_ffn_out.astype(BF16)
    bst = gmlp_bs.transpose(0, 2, 1)
    chunks = page // CMP_STRIDE
    cmp_x = cache_cmp_kv.reshape(depth, n_pool, chunks, CMP_STRIDE, 2, C_KV, HEAD_DIM)
    cmp_x = cmp_x.transpose(0, 1, 4, 5, 2, 3, 6).reshape(depth, n_pool, 2 * C_KV, chunks, CMP_STRIDE * HEAD_DIM)
    pools = (cache_fox_kv, cache_fox_logf, cmp_x,
             cache_slc_kv.reshape(depth, n_pool, page, 2 * KV_W), state_win_kv.reshape(depth, nb, wb, 2 * KV_W))

    tm_proj = min(512, B * S)
    tm_mix = min(256, B * S)
    tq_fox = min(512, S)

    xp, xs = x_prompt, x_sample
    sp = [[] for _ in range(5)]
    ss = [[] for _ in range(6)]
    for l in range(depth):
        r1 = lambda a: a[l][None, :]
        lw = {
            "g_pre": r1(norm_pre_mix), "w_in": w_in_p[l], "bf": bf[l], "lng": r1(gmlp_ln_g), "lnb": r1(gmlp_ln_b),
            "ws": gmlp_ws[l], "bst": bst[l], "ga": r1(norm_group_a), "gb": r1(norm_group_b), "gc": r1(norm_group_c),
            "w_o": w_o_b[l], "gpm": r1(norm_post_mix), "gpf": r1(norm_pre_ffn), "w_ffn_in": w_fi_b[l],
            "w_ffn_out": w_fo_b[l], "gpo": r1(norm_post_ffn),
        }
        xp, st_p = _layer_prompt(xp, lw, {k: v[l] for k, v in cw_prompt.items()}, tb, tm_proj, tm_mix, tq_fox)
        xs, st_s = _layer_sample(l, xs, lw, {k: v[l] for k, v in cw_sample.items()}, sb, pools, state_win_kv,
                                 page_table)
        for lst, s in zip(sp, st_p):
            lst.append(s)
        for lst, s in zip(ss, st_s):
            lst.append(s)
    return tuple([xp, xs] + [jnp.stack(s) for s in sp] + [jnp.stack(s) for s in ss])
```

```python
import functools
import math

import numpy as np
import jax
import jax.numpy as jnp
from jax import lax
from jax.experimental import pallas as pl
from jax.experimental.pallas import tpu as pltpu

F32 = jnp.float32
BF16 = jnp.bfloat16
HIGHEST = lax.Precision.HIGHEST

D_MODEL = 1024
HEAD_DIM = 64
A_HEADS = 6
B_GROUPS = 4
C_HEADS = 6
C_KV = 2
C_REP = C_HEADS // C_KV
A_W = A_HEADS * HEAD_DIM
B_W = B_GROUPS * HEAD_DIM
C_W = C_HEADS * HEAD_DIM
KV_W = C_KV * HEAD_DIM
Q_BLOCK = 128
CHUNK = 128
CMP_LEN = 32
CMP_STRIDE = 16
CMP_HIDDEN = 256
SLC_BLOCK = 64
SLC_TOPN = 16
WINDOW = 512
NUM_BUCKETS = 32
MAX_DISTANCE = 128
D_FF = 2816
EPS = 1e-6
NEG = -1e30
M_INIT = 0.5 * NEG
FORCE = 1e4
SCALE = HEAD_DIM ** -0.5
PROJ_WIDTHS = (A_W, A_W, A_W, A_HEADS, B_W, B_W, C_W, KV_W, KV_W, KV_W, KV_W, KV_W, KV_W, 3 * C_HEADS)

LANES = 128
SUBLANES = 8
HEAD_PAD = 8
VMEM_LIMIT = 56 * 1024 * 1024
PAGES_PER_STEP = 8

P_QA = (0, 384)
P_KVA = (384, 1152)
P_BU = (1152, 1408)
P_BV = (1408, 1664)
P_QC = (1664, 2048)
P_CMP = (2048, 2304)
P_SLC = (2304, 2560)
P_WIN = (2560, 2816)
P_MISC = (2816, 2944)
N_PROJ = 2944
GATE_OFF = A_HEADS


def _dot(a, b):
    return jnp.dot(a, b, preferred_element_type=F32)


def _dot_exact(a, b):
    return jnp.dot(a, b, preferred_element_type=F32, precision=HIGHEST)


def _dot_nt(a, b, precision=None):
    return lax.dot_general(a, b, (((1,), (1,)), ((), ())), preferred_element_type=F32, precision=precision)


def _gelu(x):
    return 0.5 * x * (1.0 + jnp.tanh(math.sqrt(2.0 / math.pi) * (x + 0.044715 * (x * x * x))))


def _sigmoid(x):
    return 1.0 / (1.0 + jnp.exp(-x))


def _rms(x, g):
    return x * lax.rsqrt(jnp.mean(x * x, axis=-1, keepdims=True) + EPS) * g


def _iota(shape, dim):
    return lax.broadcasted_iota(jnp.int32, shape, dim)


def _shr(x, n):
    return jnp.right_shift(x, int(math.log2(n)))


def _cparams(sem):
    return pltpu.CompilerParams(dimension_semantics=sem, vmem_limit_bytes=VMEM_LIMIT)


def _const_spec(shape):
    nd = len(shape)
    return pl.BlockSpec(shape, lambda *_: (0,) * nd)


def _proj_kernel(x_ref, g_ref, w_ref, bf_ref, lng_ref, lnb_ref,
                 qa_ref, kva_ref, kvab_ref, bu_ref, vn_ref, qc_ref,
                 cmp_ref, slc_ref, win_ref, kvcb_ref, misc_ref):
    h = _rms(x_ref[...], g_ref[...]).astype(BF16)

    def mm(seg):
        return _dot(h, w_ref[:, seg[0]:seg[1]])

    qa_ref[...] = (mm(P_QA) * SCALE).astype(BF16)
    kva = mm(P_KVA)
    kva_ref[...] = kva
    kvab_ref[...] = kva.astype(BF16)
    bu_ref[...] = _gelu(mm(P_BU))
    v = _gelu(mm(P_BV))
    mu = jnp.mean(v, axis=-1, keepdims=True)
    var = jnp.mean(jnp.square(v - mu), axis=-1, keepdims=True)
    vn_ref[...] = (v - mu) * lax.rsqrt(var + EPS) * lng_ref[...] + lnb_ref[...]
    qc_ref[...] = (mm(P_QC) * SCALE).astype(BF16)
    c = mm(P_CMP)
    cmp_ref[...] = c
    kvcb_ref[:, 0:256] = c.astype(BF16)
    c = mm(P_SLC)
    slc_ref[...] = c
    kvcb_ref[:, 256:512] = c.astype(BF16)
    c = mm(P_WIN)
    win_ref[...] = c
    kvcb_ref[:, 512:768] = c.astype(BF16)
    m = mm(P_MISC) + bf_ref[...]
    lane = _iota(m.shape, 1)
    logsig = jnp.minimum(m, 0.0) - jnp.log1p(jnp.exp(-jnp.abs(m)))
    misc_ref[...] = jnp.where(lane < A_HEADS, logsig, _sigmoid(m))


def _proj(x2d, g, w, bf, lng, lnb, tm):
    R = x2d.shape[0]
    row = lambda n: pl.BlockSpec((tm, n), lambda i: (i, 0))
    outs = [(384, BF16), (768, F32), (768, BF16), (256, F32), (256, F32), (384, BF16),
            (256, F32), (256, F32), (256, F32), (768, BF16), (128, F32)]
    return pl.pallas_call(
        _proj_kernel,
        grid=(R // tm,),
        in_specs=[row(D_MODEL), _const_spec((1, D_MODEL)), _const_spec((D_MODEL, N_PROJ)),
                  _const_spec((1, 128)), _const_spec((1, B_W)), _const_spec((1, B_W))],
        out_specs=[row(n) for n, _ in outs],
        out_shape=[jax.ShapeDtypeStruct((R, n), dt) for n, dt in outs],
        compiler_params=_cparams(("parallel",)),
        name="proj",
    )(x2d, g, w, bf, lng, lnb)


def _fox_kernel(q_ref, k_ref, v_ref, c_ref, o_ref, *, tq):
    qi = pl.program_id(2)
    tk = tq
    q = q_ref[...]
    lane = _iota((1, LANES), 1)
    low = lane < HEAD_DIM
    zero = jnp.zeros_like(q)
    qm = (jnp.where(low, q, zero), jnp.where(low, zero, q))
    one_lane = (HEAD_DIM, 0)

    def v_aug(v, hh):
        mine = low if hh == 0 else jnp.logical_not(low)
        ones = jnp.where(lane == one_lane[hh], 1.0, 0.0).astype(BF16)
        return jnp.where(mine, v, jnp.broadcast_to(ones, v.shape))

    def tile(ki, carry, causal):
        base = pl.multiple_of(ki * tk, tk)
        k = k_ref[pl.ds(base, tk), :]
        v = v_ref[pl.ds(base, tk), :]
        out = []
        for hh in range(2):
            m, acc = carry[hh]
            s = _dot_nt(qm[hh], k) - c_ref[hh:hh + 1, pl.ds(base, tk)]
            if causal is not None:
                s = jnp.where(causal, s, NEG)
            m_new = jnp.maximum(m, jnp.max(s, axis=-1, keepdims=True))
            alpha = jnp.exp(m - m_new)
            p = jnp.exp(s - m_new)
            acc = alpha * acc + _dot(p.astype(BF16), v_aug(v, hh))
            out.append((m_new, acc))
        return tuple(out)

    init = tuple((jnp.full((tq, 1), M_INIT, F32), jnp.zeros((tq, LANES), F32)) for _ in range(2))
    carry = lax.fori_loop(0, qi, lambda ki, c: tile(ki, c, None), init)
    causal = _iota((1, tk), 1) <= _iota((tq, 1), 0)
    (_, acc0), (_, acc1) = tile(qi, carry, causal)
    l0 = acc0[:, one_lane[0]:one_lane[0] + 1]
    l1 = acc1[:, one_lane[1]:one_lane[1] + 1]
    o_ref[...] = jnp.where(low, acc0 * (1.0 / l0), acc1 * (1.0 / l1))


def _fox_prompt(qa, kvab, crow, tq):
    B, S, _ = qa.shape
    half = A_HEADS // 2
    return pl.pallas_call(
        functools.partial(_fox_kernel, tq=tq),
        grid=(B, half, S // tq),
        in_specs=[
            pl.BlockSpec((None, tq, LANES), lambda b, p, qi: (b, qi, p)),
            pl.BlockSpec((None, S, LANES), lambda b, p, qi: (b, 0, p)),
            pl.BlockSpec((None, S, LANES), lambda b, p, qi: (b, 0, half + p)),
            pl.BlockSpec((None, None, 2, S), lambda b, p, qi: (b, p, 0, 0)),
        ],
        out_specs=pl.BlockSpec((None, tq, LANES), lambda b, p, qi: (b, qi, p)),
        out_shape=jax.ShapeDtypeStruct((B, S, A_W), F32),
        compiler_params=_cparams(("parallel", "parallel", "arbitrary")),
        name="fox_prompt",
    )(qa, kvab, kvab, crow)


def _cumsum_kernel(x_ref, o_ref):
    upper = jnp.where(_iota((LANES, LANES), 0) <= _iota((LANES, LANES), 1), 1.0, 0.0)
    carry = jnp.zeros((x_ref.shape[0], 1), F32)
    for i in range(x_ref.shape[1] // LANES):
        lanes = slice(i * LANES, (i + 1) * LANES)
        c = _dot_exact(x_ref[:, lanes], upper) + carry
        o_ref[:, lanes] = c
        carry = c[:, LANES - 1:LANES]


def _cumsum_lanes(x):
    n, h, length = x.shape
    spec = pl.BlockSpec((None, h, length), lambda i: (i, 0, 0))
    return pl.pallas_call(
        _cumsum_kernel, grid=(n,), in_specs=[spec], out_specs=spec,
        out_shape=jax.ShapeDtypeStruct(x.shape, F32),
        compiler_params=_cparams(("parallel",)),
        name="cumsum_logf",
    )(x)


def _gmlp_kernel(u_ref, vn_ref, ws_ref, bst_ref, o_ref, *, nchunk):
    tril = _iota((CHUNK, CHUNK), 1) <= _iota((CHUNK, CHUNK), 0)
    lane = _iota((1, B_W), 1)
    ws = [jnp.where(tril, ws_ref[h], 0.0).astype(BF16) for h in range(B_GROUPS)]
    for c in range(nchunk):
        rows = slice(c * CHUNK, (c + 1) * CHUNK)
        vn = vn_ref[rows, :].astype(BF16)
        z = jnp.zeros((CHUNK, B_W), F32)
        for h in range(B_GROUPS):
            zh = _dot(ws[h], vn) + bst_ref[:, h:h + 1]
            z = jnp.where(_shr(lane, HEAD_DIM) == h, zh, z)
        o_ref[rows, :] = u_ref[rows, :] * z


def _gmlp(u, vn, ws, bst, nchunk):
    R = u.shape[0]
    tm = nchunk * CHUNK
    row = pl.BlockSpec((tm, B_W), lambda i: (i, 0))
    return pl.pallas_call(
        functools.partial(_gmlp_kernel, nchunk=nchunk),
        grid=(R // tm,),
        in_specs=[row, row, _const_spec((B_GROUPS, CHUNK, CHUNK)), _const_spec((CHUNK, B_GROUPS))],
        out_specs=row,
        out_shape=jax.ShapeDtypeStruct((R, B_W), F32),
        compiler_params=_cparams(("parallel",)),
        name="gmlp",
    )(u, vn, ws, bst)


def _compress_core(x, w, pe_a, pe_b, b1, w2, b2):
    nc = x.shape[0]
    half = w.shape[1] // 2
    h = _dot(x, w)
    const = _dot(pe_a, w[:, :half]) + _dot(pe_b, w[:, half:])
    second = pltpu.roll(h[:, half:], nc - 1, 0)
    hid = _gelu(h[:, :half] + second + const[0:1, :] + b1)
    return _dot(hid.astype(BF16), w2) + b2


def _cmp_prompt_kernel(krows_ref, vrows_ref, wab_ref, pe_ref, b1_ref, w2_ref, b2_ref, o_ref, *, nc):
    for kv, rows_ref in enumerate((krows_ref, vrows_ref)):
        x = jnp.concatenate(
            [rows_ref[pl.ds(l, nc, stride=CMP_STRIDE), :] for l in range(CMP_STRIDE)], axis=-1).astype(BF16)
        o_ref[:, kv * KV_W:(kv + 1) * KV_W] = _compress_core(
            x, wab_ref[kv], pe_ref[kv, 0], pe_ref[kv, 1], b1_ref[kv], w2_ref[kv], b2_ref[kv]).astype(BF16)


def _compress_prompt(cmp_rows, cw):
    B, S, _ = cmp_rows.shape
    nc = S // CMP_STRIDE
    return pl.pallas_call(
        functools.partial(_cmp_prompt_kernel, nc=nc),
        grid=(B,),
        in_specs=[pl.BlockSpec((None, S, KV_W), lambda b: (b, 0, 0)), pl.BlockSpec((None, S, KV_W), lambda b: (b, 0, 1)),
                  _const_spec(cw["wab"].shape), _const_spec(cw["pe"].shape), _const_spec(cw["b1t"].shape),
                  _const_spec(cw["w2bd"].shape), _const_spec(cw["b2t"].shape)],
        out_specs=pl.BlockSpec((None, nc, 2 * KV_W), lambda b: (b, 0, 0)),
        out_shape=jax.ShapeDtypeStruct((B, nc, 2 * KV_W), BF16),
        compiler_params=_cparams(("parallel",)),
        name="compress_prompt",
    )(cmp_rows, cmp_rows, cw["wab"], cw["pe"], cw["b1t"], cw["w2bd"], cw["b2t"])


def _rank_select_t(score_t, topn):
    n = score_t.shape[0]
    groups = [score_t[lo:lo + SUBLANES] for lo in range(0, n, SUBLANES)]
    sub = _iota((SUBLANES, 1), 0)
    ranks = [jnp.zeros(g.shape, F32) for g in groups]
    for jp in range(n):
        row = score_t[jp:jp + 1, :]
        for gi, sg in enumerate(groups):
            lo = gi * SUBLANES
            if lo + SUBLANES - 1 <= jp:
                beats = row > sg
            elif lo > jp:
                beats = row >= sg
            else:
                beats = (row > sg) | ((row == sg) & (sub + lo > jp))
            ranks[gi] = ranks[gi] + jnp.where(beats, 1.0, 0.0)
    return jnp.concatenate([jnp.where(r < topn, 1.0, 0.0) for r in ranks], axis=0)


def _attend_tiles(qg, tiles, carry):
    m, acc = carry
    s_parts = []
    for k, _, bias, mask in tiles:
        s = _dot_nt(qg, k)
        if bias is not None:
            s = s + bias
        s_parts.append(jnp.where(mask, s, NEG))
    m_new = m
    for s in s_parts:
        m_new = jnp.maximum(m_new, jnp.max(s, axis=-1, keepdims=True))
    acc = jnp.exp(m - m_new) * acc
    for s, (_, v, _, _) in zip(s_parts, tiles):
        acc = acc + _dot(jnp.exp(s - m_new).astype(BF16), v)
    return m_new, acc


def _nsa_kernel(q_ref, kv_ref, cmp_ref, misc_ref, tb_ref, o_ref, *, nbp, nsb):
    qi = pl.program_id(1)
    tq = Q_BLOCK
    nrow = C_REP * tq
    t0 = qi * tq
    trow = t0 + _iota((tq, 1), 0)
    trow3 = jnp.concatenate([trow] * C_REP, axis=0)
    tl3 = trow3 - t0
    sl = _iota((1, tq), 1)
    causal3 = sl <= tl3
    topn = min(SLC_TOPN, nsb)
    lane = _iota((1, LANES), 1)
    low = lane < HEAD_DIM
    per = SLC_BLOCK // CMP_STRIDE
    pool_t = jnp.where(_shr(_iota((nsb, nbp), 1), per) == _iota((nsb, nbp), 0), 1.0, 0.0)
    groups = range(C_KV)
    mine = [low, jnp.logical_not(low)]
    one_lane = [HEAD_DIM, 0]
    ones = [jnp.where(lane == one_lane[g], 1.0, 0.0).astype(BF16) for g in groups]
    ks_lo, vs_lo, kw_lo, vw_lo = 2 * KV_W, 3 * KV_W, 4 * KV_W, 5 * KV_W

    def kv_tile(base, width, k_lo, v_lo):
        k = kv_ref[pl.ds(base, width), k_lo:k_lo + KV_W]
        v = kv_ref[pl.ds(base, width), v_lo:v_lo + KV_W]
        return k, [jnp.where(mine[g], v, jnp.broadcast_to(ones[g], v.shape)) for g in groups]

    qg, o_cmp, sel = [], [], []
    for g in groups:
        parts = []
        for r in range(C_REP):
            h = C_REP * g + r
            src = q_ref[:, (h // 2) * LANES:(h // 2 + 1) * LANES].astype(F32)
            if h % 2 != g:
                src = pltpu.roll(src, HEAD_DIM, 1)
            parts.append(jnp.where(mine[g], src, 0.0))
        q = jnp.concatenate(parts, axis=0).astype(BF16)
        qg.append(q)

        s = _dot_nt(q, cmp_ref[:, 0:KV_W])
        valid = (_iota((1, nbp), 1) * CMP_STRIDE + (CMP_LEN - 1)) <= trow3
        s = jnp.where(valid, s, NEG)
        e = jnp.exp(s - jnp.max(s, axis=-1, keepdims=True))
        prob = jnp.where(valid, e * (1.0 / jnp.sum(e, axis=-1, keepdims=True)), 0.0)
        o_cmp.append(_dot(prob.astype(BF16), cmp_ref[:, KV_W:2 * KV_W]))

        psum = prob[0:tq] + prob[tq:2 * tq] + prob[2 * tq:3 * tq]
        imp_t = _dot_nt(pool_t, psum, HIGHEST)
        j = _iota((nsb, 1), 0)
        tcol = t0 + _iota((1, tq), 1)
        cur = _shr(tcol, SLC_BLOCK)
        forced = (j == 0) | (j == cur) | (j == cur - 1)
        score_t = jnp.where(forced, FORCE, jnp.where(j * SLC_BLOCK <= tcol, imp_t, -1.0))
        sel.append(_rank_select_t(score_t, topn).T.astype(BF16))

    def sel_masks(base, width):
        blk = base // SLC_BLOCK + _shr(_iota((nsb, width), 1), SLC_BLOCK)
        expand = jnp.where(_iota((nsb, width), 0) == blk, 1.0, 0.0).astype(BF16)
        return [jnp.concatenate([_dot(sel[g], expand)] * C_REP, axis=0) > 0.5 for g in groups]

    init = (jnp.full((nrow, 1), M_INIT, F32), jnp.zeros((nrow, LANES), F32))

    def far_step(base, width, carry):
        k, v = kv_tile(base, width, ks_lo, vs_lo)
        masks = sel_masks(base, width)
        return tuple(_attend_tiles(qg[g], [(k, v[g], None, masks[g])], carry[g]) for g in groups)

    n_far = jnp.maximum(qi - 1, 0)
    n_big = n_far // 4
    carry = lax.fori_loop(0, n_big, lambda i, c: far_step(pl.multiple_of(i * (4 * tq), 4 * tq), 4 * tq, c),
                          (init, init))
    carry = lax.fori_loop(0, n_far - 4 * n_big,
                          lambda i, c: far_step(pl.multiple_of((n_big * 4 + i) * tq, tq), tq, c), carry)

    pbase = pl.multiple_of(jnp.maximum(qi - 1, 0) * tq, tq)
    dbase = pl.multiple_of(t0, tq)
    has_prev = qi >= 1
    far_w = WINDOW - tq
    fbase = pl.multiple_of(jnp.maximum(t0 - WINDOW, 0), tq)
    kpos = fbase + _iota((1, far_w), 1)
    far_mask = (kpos < t0 - tq) & (trow3 - kpos <= WINDOW)
    mask_p = sel_masks(pbase, tq)
    mask_d = sel_masks(dbase, tq)
    ksp, vsp = kv_tile(pbase, tq, ks_lo, vs_lo)
    ksd, vsd = kv_tile(dbase, tq, ks_lo, vs_lo)
    kwf, vwf = kv_tile(fbase, far_w, kw_lo, vw_lo)
    kwp, vwp = kv_tile(pbase, tq, kw_lo, vw_lo)
    kwd, vwd = kv_tile(dbase, tq, kw_lo, vw_lo)
    heads = [None] * C_HEADS
    for g in groups:
        bias_diag = tb_ref[g, 0]
        bias_prev = tb_ref[g, 1]
        _, acc = _attend_tiles(qg[g], [(ksp, vsp[g], bias_prev, mask_p[g] & has_prev),
                                       (ksd, vsd[g], bias_diag, mask_d[g] & causal3)], carry[g])
        o_slc = acc * (1.0 / acc[:, one_lane[g]:one_lane[g] + 1])
        cw = _attend_tiles(qg[g], [(kwf, vwf[g], None, far_mask)], init)
        _, acc = _attend_tiles(qg[g], [(kwp, vwp[g], bias_prev, (tl3 >= 0) & has_prev),
                                       (kwd, vwd[g], bias_diag, causal3)], cw)
        o_win = acc * (1.0 / acc[:, one_lane[g]:one_lane[g] + 1])
        for r in range(C_REP):
            h = C_REP * g + r
            rows = slice(r * tq, (r + 1) * tq)
            gate = lambda br: misc_ref[:, GATE_OFF + br * C_HEADS + h:GATE_OFF + br * C_HEADS + h + 1]
            heads[h] = gate(0) * o_cmp[g][rows] + gate(1) * o_slc[rows] + gate(2) * o_win[rows]

    for pair in range(C_HEADS // 2):
        halves = []
        for h in (2 * pair, 2 * pair + 1):
            x = heads[h]
            if h // C_REP != h % 2:
                x = pltpu.roll(x, HEAD_DIM, 1)
            halves.append(x)
        o_ref[:, pair * LANES:(pair + 1) * LANES] = jnp.where(low, halves[0], halves[1])


def _nsa_prompt(qc, kvcb, kcv, misc, tb):
    B, S, _ = qc.shape
    nbp = kcv.shape[1]
    nsb = nbp * CMP_STRIDE // SLC_BLOCK
    return pl.pallas_call(
        functools.partial(_nsa_kernel, nbp=nbp, nsb=nsb),
        grid=(B, S // Q_BLOCK),
        in_specs=[pl.BlockSpec((None, Q_BLOCK, C_W), lambda b, i: (b, i, 0)),
                  pl.BlockSpec((None, S, 6 * KV_W), lambda b, i: (b, 0, 0)),
                  pl.BlockSpec((None, nbp, 2 * KV_W), lambda b, i: (b, 0, 0)),
                  pl.BlockSpec((None, Q_BLOCK, 128), lambda b, i: (b, i, 0)),
                  _const_spec(tb.shape)],
        out_specs=pl.BlockSpec((None, Q_BLOCK, C_W), lambda b, i: (b, i, 0)),
        out_shape=jax.ShapeDtypeStruct((B, S, C_W), F32),
        compiler_params=_cparams(("parallel", "arbitrary")),
        name="nsa_prompt",
    )(qc, kvcb, kcv, misc, tb)


def _mix_ffn_kernel(a_ref, b_ref, c_ref, x_ref, ga_ref, gb_ref, gc_ref, wo_ref, gpm_ref, gpf_ref,
                    win_ref, wout_ref, gpo_ref, o_ref, *, nsplit):
    an = _rms(a_ref[...], ga_ref[...]).astype(BF16)
    bn = _rms(b_ref[...], gb_ref[...]).astype(BF16)
    cn = _rms(c_ref[...], gc_ref[...]).astype(BF16)
    mix = (_dot(an, wo_ref[0:A_W, :]) + _dot(bn, wo_ref[A_W:A_W + B_W, :])
           + _dot(cn, wo_ref[A_W + B_W:A_W + B_W + C_W, :]))
    x1 = x_ref[...] + _rms(mix, gpm_ref[...])
    h = _rms(x1, gpf_ref[...]).astype(BF16)
    wid = D_FF // nsplit
    y = jnp.zeros(x1.shape, F32)
    for c in range(nsplit):
        lo = c * wid
        gate = _dot(h, win_ref[:, lo:lo + wid])
        up = _dot(h, win_ref[:, D_FF + lo:D_FF + lo + wid])
        act = gate * _sigmoid(gate) * up
        y = y + _dot(act.astype(BF16), wout_ref[lo:lo + wid, :])
    o_ref[...] = x1 + _rms(y, gpo_ref[...])


def _mix_ffn(a, b, c, x, lw, tm):
    R = x.shape[0]
    row = lambda n: pl.BlockSpec((tm, n), lambda i: (i, 0))
    once = lambda shape: pl.BlockSpec(shape, lambda i: (0, 0), pipeline_mode=pl.Buffered(1))
    return pl.pallas_call(
        functools.partial(_mix_ffn_kernel, nsplit=2),
        grid=(R // tm,),
        in_specs=[row(A_W), row(B_W), row(C_W), row(D_MODEL),
                  once((1, A_W)), once((1, B_W)), once((1, C_W)), once((D_MODEL, D_MODEL)),
                  once((1, D_MODEL)), once((1, D_MODEL)), once((D_MODEL, 2 * D_FF)), once((D_FF, D_MODEL)),
                  once((1, D_MODEL))],
        out_specs=row(D_MODEL),
        out_shape=jax.ShapeDtypeStruct((R, D_MODEL), F32),
        compiler_params=_cparams(("parallel",)),
        name="mix_ffn",
    )(a, b, c, x, lw["ga"], lw["gb"], lw["gc"], lw["w_o"], lw["gpm"], lw["gpf"], lw["w_ffn_in"],
      lw["w_ffn_out"], lw["gpo"])


def _head_block(x, h):
    return x[h * HEAD_DIM:(h + 1) * HEAD_DIM]


def _pad_heads(x):
    return jnp.concatenate([x, jnp.zeros((HEAD_PAD - x.shape[0], x.shape[1]), F32)], axis=0)


def _row_softmax_step(s, m_prev):
    m_new = jnp.maximum(m_prev, jnp.max(s, axis=1, keepdims=True))
    return m_new, jnp.exp(m_prev - m_new), jnp.exp(s - m_new)


def _fox_dec_kernel(pt_ref, q_ref, *refs, npp):
    pages = refs[:npp]
    lfs = refs[npp:2 * npp]
    new_ref, newlf_ref, o_ref, m_sc, l_sc, c_sc, acc_sc = refs[2 * npp:]
    j = pl.program_id(1)
    page = pages[0].shape[1]
    q = q_ref[...]
    upper = jnp.where(_iota((page, page), 0) <= _iota((page, page), 1), 1.0, 0.0)

    @pl.when(j == 0)
    def _():
        m_sc[...] = jnp.full(m_sc.shape, M_INIT, F32)
        l_sc[...] = jnp.zeros(l_sc.shape, F32)
        c_sc[...] = jnp.zeros(c_sc.shape, F32)
        acc_sc[...] = jnp.zeros(acc_sc.shape, F32)

    def scores(kt):
        prod = kt * q
        return _pad_heads(jnp.concatenate(
            [jnp.sum(_head_block(prod, h), axis=0, keepdims=True) for h in range(A_HEADS)], axis=0))

    def accumulate(alpha, p_tiles, v_tiles):
        for h in range(A_HEADS):
            a = _head_block(acc_sc, h) * alpha[h:h + 1, :]
            for p, v in zip(p_tiles, v_tiles):
                a = a + p[h:h + 1, :] * v[A_W + h * HEAD_DIM:A_W + (h + 1) * HEAD_DIM, :]
            acc_sc[h * HEAD_DIM:(h + 1) * HEAD_DIM, :] = a

    carry = c_sc[...]
    s_parts = []
    for pg, lf in zip(pages, lfs):
        c_page = _dot_exact(lf[...], upper) + carry
        carry = c_page[:, page - 1:page]
        s_parts.append(scores(pg[0:A_W, :]) - c_page)
    c_sc[...] = carry
    m_new, alpha, p = _row_softmax_step(jnp.concatenate(s_parts, axis=1), m_sc[...])
    l_sc[...] = alpha * l_sc[...] + jnp.sum(p, axis=1, keepdims=True)
    m_sc[...] = m_new
    accumulate(alpha, [p[:, i * page:(i + 1) * page] for i in range(npp)], pages)

    @pl.when(j == pl.num_programs(1) - 1)
    def _():
        first = _iota((1, page), 1) == 0
        s_new = jnp.where(first, scores(new_ref[0:A_W, :]) - (c_sc[...] + newlf_ref[...]), NEG)
        m_new, alpha, p_new = _row_softmax_step(s_new, m_sc[...])
        l_inv = 1.0 / (alpha * l_sc[...] + jnp.sum(p_new, axis=1, keepdims=True))
        accumulate(alpha, [p_new], [new_ref])
        for h in range(A_HEADS):
            rows = slice(h * HEAD_DIM, (h + 1) * HEAD_DIM)
            o_ref[rows, :] = jnp.sum(acc_sc[rows, :], axis=1, keepdims=True) * l_inv[h:h + 1, :]


def _fox_decode(layer, page_table, q_rep, pool_t, logf_t, new_t, new_lf, npp):
    nb, npages = page_table.shape
    page = pool_t.shape[3]
    page_of = lambda b, j, i, pt: pt[b, j * npp + i]
    page_spec = lambda i: pl.BlockSpec((None, None, 2 * A_W, page), lambda b, j, pt: (layer, page_of(b, j, i, pt), 0, 0))
    logf_spec = lambda i: pl.BlockSpec((None, None, HEAD_PAD, page), lambda b, j, pt: (layer, page_of(b, j, i, pt), 0, 0))
    per_b = lambda shape: pl.BlockSpec((None,) + shape, lambda b, j, pt: (b,) + (0,) * len(shape))
    grid_spec = pltpu.PrefetchScalarGridSpec(
        num_scalar_prefetch=1,
        grid=(nb, npages // npp),
        in_specs=[per_b((A_W, page))] + [page_spec(i) for i in range(npp)] + [logf_spec(i) for i in range(npp)]
        + [per_b((2 * A_W, page)), per_b((HEAD_PAD, 1))],
        out_specs=per_b((A_W, 1)),
        scratch_shapes=[pltpu.VMEM((HEAD_PAD, 1), F32), pltpu.VMEM((HEAD_PAD, 1), F32), pltpu.VMEM((HEAD_PAD, 1), F32),
                        pltpu.VMEM((A_W, page), F32)],
    )
    return pl.pallas_call(
        functools.partial(_fox_dec_kernel, npp=npp),
        grid_spec=grid_spec,
        out_shape=jax.ShapeDtypeStruct((nb, A_W, 1), F32),
        compiler_params=_cparams(("parallel", "arbitrary")),
        name="fox_decode",
    )(page_table, q_rep, *([pool_t] * npp), *([logf_t] * npp), new_t, new_lf)


def _cmp_dec_kernel(pt_ref, *refs, npp, nc, nsb_all):
    pages = refs[:npp]
    w_ref, pe_ref, b1_ref, w2_ref, b2_ref, q_ref, ocmp_ref, sel_ref, x_sc = refs[npp:]
    j = pl.program_id(1)
    per_page = pages[0].shape[1]
    for i, pg in enumerate(pages):
        row0 = pl.multiple_of((j * npp + i) * per_page, per_page)
        for c in range(2 * C_KV):
            x_sc[c, pl.ds(row0, per_page), :] = pg[c]

    @pl.when(j == pl.num_programs(1) - 1)
    def _():
        kc = [[_compress_core(x_sc[kv * C_KV + g].astype(BF16), w_ref[kv], pe_ref[kv, 0], pe_ref[kv, 1],
                              b1_ref[kv], w2_ref[kv], b2_ref[kv]).astype(BF16)
               for g in range(C_KV)] for kv in range(2)]
        q = q_ref[...]
        grp0 = _iota((HEAD_PAD, 1), 0) < C_REP
        pick = lambda x0, x1: jnp.where(grp0, x0, x1)
        s = pick(_dot_nt(q, kc[0][0]), _dot_nt(q, kc[0][1]))
        valid = _iota((1, nc), 1) < nc - 1
        s = jnp.where(valid, s, NEG)
        e = jnp.exp(s - jnp.max(s, axis=-1, keepdims=True))
        prob = jnp.where(valid, e / jnp.sum(e, axis=-1, keepdims=True), 0.0)
        pb = prob.astype(BF16)
        ocmp_ref[...] = pick(_dot(pb, kc[1][0]), _dot(pb, kc[1][1]))

        per = SLC_BLOCK // CMP_STRIDE
        nsbp = sel_ref.shape[1]
        hrow = _iota((HEAD_PAD, 1), 0)
        psum = jnp.concatenate(
            [jnp.sum(jnp.where((hrow >= C_REP * g) & (hrow < C_REP * (g + 1)), prob, 0.0), axis=0, keepdims=True)
             for g in range(C_KV)] + [jnp.zeros((HEAD_PAD - C_KV, nc), F32)], axis=0)
        pool = jnp.where(_shr(_iota((nc, nsbp), 0), per) == _iota((nc, nsbp), 1), 1.0, 0.0)
        imp = _dot_exact(psum, pool)
        jl = _iota((1, nsbp), 1)
        cur = nsb_all - 1
        forced = (jl == 0) | (jl == cur) | (jl == cur - 1)
        score = jnp.where(forced, FORCE, jnp.where(jl <= cur, imp, -2.0))
        topn = min(SLC_TOPN, nsb_all)
        rank = jnp.zeros(score.shape, F32)
        for jp in range(nsb_all):
            col = score[:, jp:jp + 1]
            beats = (col > score) | ((col == score) & (jp < jl))
            rank = rank + jnp.where(beats, 1.0, 0.0)
        sel_ref[...] = jnp.where(rank < topn, 1.0, 0.0)


def _cmp_decode(layer, page_table, pool, cw, qh, npp):
    nb, npages = page_table.shape
    per_page, width = pool.shape[3], pool.shape[4]
    nc = npages * per_page
    nsb_all = nc * CMP_STRIDE // SLC_BLOCK + 1
    nsbp = -(-nsb_all // LANES) * LANES

    def page_spec(i):
        return pl.BlockSpec((None, None, 2 * C_KV, per_page, width),
                            lambda b, j, pt: (layer, pt[b, j * npp + i], 0, 0, 0))

    const = lambda a: pl.BlockSpec(a.shape, lambda b, j, pt: (0,) * a.ndim)
    grid_spec = pltpu.PrefetchScalarGridSpec(
        num_scalar_prefetch=1,
        grid=(nb, npages // npp),
        in_specs=[page_spec(i) for i in range(npp)]
        + [const(cw["w"]), const(cw["pe"]), const(cw["b1"]), const(cw["w2"]), const(cw["b2"]),
           pl.BlockSpec((None, HEAD_PAD, HEAD_DIM), lambda b, j, pt: (b, 0, 0))],
        out_specs=[pl.BlockSpec((None, HEAD_PAD, HEAD_DIM), lambda b, j, pt: (b, 0, 0)),
                   pl.BlockSpec((None, HEAD_PAD, nsbp), lambda b, j, pt: (b, 0, 0))],
        scratch_shapes=[pltpu.VMEM((2 * C_KV, nc, width), F32)],
    )
    return pl.pallas_call(
        functools.partial(_cmp_dec_kernel, npp=npp, nc=nc, nsb_all=nsb_all),
        grid_spec=grid_spec,
        out_shape=[jax.ShapeDtypeStruct((nb, HEAD_PAD, HEAD_DIM), F32),
                   jax.ShapeDtypeStruct((nb, HEAD_PAD, nsbp), F32)],
        compiler_params=_cparams(("parallel", "arbitrary")),
        name="cmp_decode",
    )(page_table, *([pool] * npp), cw["w"], cw["pe"], cw["b1"], cw["w2"], cw["b2"], qh)


def _slc_dec_kernel(plist_ref, lpage_ref, cnt_ref, flag_ref, q_ref, *refs, npp, nslot, nflag, last_page):
    pages = refs[:npp]
    (nbias_ref, fbias_ref, win_ref, wbias_ref, b0_ref, new_ref, gate_ref, ocmp_ref, o_ref,
     m_sc, l_sc, acc_sc) = refs[npp:]
    b = pl.program_id(0)
    j = pl.program_id(1)
    page = pages[0].shape[1]
    q = q_ref[...]
    grp0 = _iota((HEAD_PAD, 1), 0) < C_REP
    lane = _iota((1, page), 1)

    def scores(kt):
        return _pad_heads(jnp.concatenate(
            [jnp.sum(_head_block(kt, h // C_REP) * _head_block(q, h), axis=0, keepdims=True)
             for h in range(C_HEADS)], axis=0))

    def weighted(acc, alpha, p_tiles, v_tiles):
        out = []
        for h in range(C_HEADS):
            a = _head_block(acc, h) * alpha[h:h + 1, :]
            for p, v in zip(p_tiles, v_tiles):
                a = a + p[h:h + 1, :] * _head_block(v, h // C_REP)
            out.append(a)
        return jnp.concatenate(out, axis=0)

    def finish(acc, l):
        l_inv = 1.0 / l
        return jnp.concatenate(
            [jnp.sum(_head_block(acc, h), axis=1, keepdims=True) * l_inv[h:h + 1, :] for h in range(C_HEADS)], axis=0)

    @pl.when(j == 0)
    def _():
        m_sc[...] = jnp.full(m_sc.shape, M_INIT, F32)
        l_sc[...] = jnp.zeros(l_sc.shape, F32)
        acc_sc[...] = jnp.zeros(acc_sc.shape, F32)

    nblk = page // SLC_BLOCK
    s_parts = []
    for i, pg in enumerate(pages):
        slot = j * npp + i
        lp = lpage_ref[b * nslot + slot]
        live = slot < cnt_ref[b]
        mask = jnp.zeros((HEAD_PAD, page), jnp.int32)
        for blk in range(nblk):
            f = [flag_ref[(b * C_KV + g) * nflag + lp * nblk + blk] for g in range(C_KV)]
            mask = jnp.where(_shr(lane, SLC_BLOCK) == blk, jnp.where(grp0, f[0], f[1]), mask)
        mask = (mask > 0) & live
        bias = jnp.where(lp == last_page, nbias_ref[...], fbias_ref[...])
        s_parts.append(jnp.where(mask, scores(pg[0:KV_W, :]) + bias, NEG))
    m_new, alpha, p = _row_softmax_step(jnp.concatenate(s_parts, axis=1), m_sc[...])
    l_sc[...] = alpha * l_sc[...] + jnp.sum(p, axis=1, keepdims=True)
    m_sc[...] = m_new
    acc_sc[...] = weighted(acc_sc[...], alpha, [p[:, i * page:(i + 1) * page] for i in range(npp)],
                           [pg[KV_W:2 * KV_W, :] for pg in pages])

    @pl.when(j == pl.num_programs(1) - 1)
    def _():
        first = lane == 0
        s_new = jnp.where(first, scores(new_ref[0:KV_W, :]) + b0_ref[...], NEG)
        m_new, alpha, p_new = _row_softmax_step(s_new, m_sc[...])
        l = alpha * l_sc[...] + jnp.sum(p_new, axis=1, keepdims=True)
        o_slc = finish(weighted(acc_sc[...], alpha, [p_new], [new_ref[KV_W:2 * KV_W, :]]), l)
        nwin = win_ref.shape[1] // page
        tiles = lambda ref, lo: [ref[lo:lo + KV_W, c * page:(c + 1) * page] for c in range(nwin)]
        s_win = [scores(kt) + wbias_ref[:, c * page:(c + 1) * page] for c, kt in enumerate(tiles(win_ref, 0))]
        s_win.append(jnp.where(first, scores(new_ref[2 * KV_W:3 * KV_W, :]) + b0_ref[...], NEG))
        s = jnp.concatenate(s_win, axis=1)
        p = jnp.exp(s - jnp.max(s, axis=1, keepdims=True))
        acc = weighted(jnp.zeros((C_W, page), F32), jnp.zeros((HEAD_PAD, 1), F32),
                       [p[:, c * page:(c + 1) * page] for c in range(nwin + 1)],
                       tiles(win_ref, KV_W) + [new_ref[3 * KV_W:4 * KV_W, :]])
        o_win = finish(acc, jnp.sum(p, axis=1, keepdims=True))
        o_ref[...] = gate_ref[:, 0:1] * ocmp_ref[...] + gate_ref[:, 1:2] * o_slc + gate_ref[:, 2:3] * o_win


def _slc_decode(layer, plist, lpage, cnt, flags, nflag, q_rep, pool_t, nbias, fbias, win_t, wbias, b0, new_t, gates,
                ocmp, npp, nslot, last_page):
    nb = q_rep.shape[0]
    page = pool_t.shape[3]
    wb = win_t.shape[3]

    def page_spec(i):
        return pl.BlockSpec((None, None, 2 * KV_W, page),
                            lambda b, j, pls, lps, cn, fl: (layer, pls[b * nslot + j * npp + i], 0, 0))

    fixed = lambda shape: pl.BlockSpec(shape, lambda b, j, pls, lps, cn, fl: (0,) * len(shape))
    per_b = lambda shape: pl.BlockSpec((None,) + shape, lambda b, j, pls, lps, cn, fl: (b,) + (0,) * len(shape))
    grid_spec = pltpu.PrefetchScalarGridSpec(
        num_scalar_prefetch=4,
        grid=(nb, nslot // npp),
        in_specs=[per_b((C_W, page))] + [page_spec(i) for i in range(npp)]
        + [fixed((HEAD_PAD, page)), fixed((HEAD_PAD, 1)),
           pl.BlockSpec((None, None, 2 * KV_W, wb), lambda b, j, pls, lps, cn, fl: (layer, b, 0, 0)),
           fixed((HEAD_PAD, wb)), fixed((HEAD_PAD, 1)), per_b((4 * KV_W, page)), per_b((C_W, 3)), per_b((C_W, 1))],
        out_specs=per_b((C_W, 1)),
        scratch_shapes=[pltpu.VMEM((HEAD_PAD, 1), F32), pltpu.VMEM((HEAD_PAD, 1), F32), pltpu.VMEM((C_W, page), F32)],
    )
    return pl.pallas_call(
        functools.partial(_slc_dec_kernel, npp=npp, nslot=nslot, nflag=nflag, last_page=last_page),
        grid_spec=grid_spec,
        out_shape=jax.ShapeDtypeStruct((nb, C_W, 1), F32),
        compiler_params=_cparams(("parallel", "arbitrary")),
        name="slc_win_decode",
    )(plist, lpage, cnt, flags, q_rep, *([pool_t] * npp), nbias, fbias, win_t, wbias, b0, new_t, gates, ocmp)


def _t5_bucket(dist):
    n = jnp.maximum(dist, 0)
    max_exact = NUM_BUCKETS // 2
    nf = jnp.maximum(n, 1).astype(F32)
    large = max_exact + (jnp.log(nf / max_exact) / math.log(MAX_DISTANCE / max_exact)
                         * (NUM_BUCKETS - max_exact)).astype(jnp.int32)
    return jnp.where(n < max_exact, n, jnp.minimum(large, NUM_BUCKETS - 1))


def _prep_w_in(w_in):
    splits = [int(s) for s in np.cumsum(PROJ_WIDTHS)[:-1]]
    a_q, a_k, a_v, a_f, b_u, b_v, c_q, c_kc, c_vc, c_ks, c_vs, c_kw, c_vw, c_g = jnp.split(w_in, splits, axis=-1)
    pad = jnp.zeros(w_in.shape[:-1] + (LANES - A_HEADS - 3 * C_HEADS,), w_in.dtype)
    return jnp.concatenate([a_q, a_k, a_v, b_u, b_v, c_q, c_kc, c_vc, c_ks, c_vs, c_kw, c_vw, a_f, c_g, pad],
                           axis=-1).astype(BF16)


def _prep_compress(cmp_pe, cmp_w1, cmp_b1, cmp_w2, cmp_b2):
    depth = cmp_w1.shape[0]
    half = CMP_LEN // 2
    eye = jnp.eye(C_KV, dtype=F32)
    w1 = cmp_w1.reshape(depth, 2, 2, half, HEAD_DIM, CMP_HIDDEN)
    wab = jnp.einsum("zkhldj,gG->zklgdhGj", w1, eye).reshape(depth, 2, half * KV_W, 2 * C_KV * CMP_HIDDEN)
    pe = cmp_pe.reshape(depth, 2, 2, half, 1, HEAD_DIM)
    pe_bd = jnp.broadcast_to(pe, (depth, 2, 2, half, C_KV, HEAD_DIM)).reshape(depth, 2, 2, 1, half * KV_W)
    pe_bd = jnp.broadcast_to(pe_bd, (depth, 2, 2, SUBLANES, half * KV_W))
    w2bd = jnp.einsum("zkjd,gG->zkgjGd", cmp_w2, eye).reshape(depth, 2, C_KV * CMP_HIDDEN, KV_W)
    w_g = w1.transpose(0, 1, 3, 4, 2, 5).reshape(depth, 2, half * HEAD_DIM, 2 * CMP_HIDDEN)
    pe_g = jnp.broadcast_to(cmp_pe.reshape(depth, 2, 2, 1, half * HEAD_DIM), (depth, 2, 2, SUBLANES, half * HEAD_DIM))
    prompt = {
        "wab": wab.astype(BF16), "pe": pe_bd.astype(BF16),
        "b1t": jnp.tile(cmp_b1, (1, 1, C_KV))[:, :, None, :],
        "w2bd": w2bd.astype(BF16),
        "b2t": jnp.tile(cmp_b2, (1, 1, C_KV))[:, :, None, :],
    }
    sample = {"w": w_g.astype(BF16), "pe": pe_g.astype(BF16), "b1": cmp_b1[:, :, None, :],
              "w2": cmp_w2.astype(BF16), "b2": cmp_b2[:, :, None, :]}
    return prompt, sample


def _prompt_bias_tiles(table):
    tl = jnp.arange(Q_BLOCK)[:, None]
    sl = jnp.arange(Q_BLOCK)[None, :]
    d = tl - sl
    far = table[_t5_bucket(jnp.asarray(8 * MAX_DISTANCE))]
    tiles = []
    for delta in (0, Q_BLOCK):
        b = table[_t5_bucket(d + delta)] - far
        tiles.append(b.transpose(2, 0, 1).reshape(C_KV, C_REP * Q_BLOCK, Q_BLOCK))
    return jnp.stack(tiles, axis=1).astype(F32)


def _sample_bias(table, page, wb):
    rows = lambda x: jnp.pad(x.T, ((0, HEAD_PAD - C_HEADS), (0, 0))).astype(F32)
    nbias = rows(table[_t5_bucket(page - jnp.arange(page))])
    fbias = rows(table[_t5_bucket(jnp.asarray([8 * MAX_DISTANCE]))])
    wbias = rows(table[_t5_bucket(wb - jnp.arange(wb))])
    b0 = rows(table[0:1])
    return nbias, fbias, wbias, b0


def _lane_repeat(x, n):
    return jnp.broadcast_to(x.astype(F32)[:, :, None], x.shape + (n,))


def _lane_zero(x, n):
    return jnp.pad(x.astype(F32)[:, :, None], ((0, 0), (0, 0), (0, n - 1)))


def _layer_prompt(x, lw, cw, tb, tm_proj, tm_mix, tq_fox):
    B, S, _ = x.shape
    R = B * S
    x2 = x.reshape(R, D_MODEL)
    qa, kva, kvab, bu, vn, qc, cmp_r, slc_r, win_r, kvcb, misc = _proj(
        x2, lw["g_pre"], lw["w_in"], lw["bf"], lw["lng"], lw["lnb"], tm_proj)
    lf = misc[:, :A_HEADS].reshape(B, S, A_HEADS)
    lf_t = jnp.pad(lf, ((0, 0), (0, 0), (0, HEAD_PAD - A_HEADS))).transpose(0, 2, 1)
    crow = _cumsum_lanes(lf_t)[:, :A_HEADS].reshape(B, A_HEADS // 2, 2, S)
    a_out = _fox_prompt(qa.reshape(B, S, A_W), kvab.reshape(B, S, 2 * A_W), crow, tq_fox)
    b_out = _gmlp(bu, vn, lw["ws"], lw["bst"], min(8, R // CHUNK))
    kcv = _compress_prompt(cmp_r.reshape(B, S, 2 * KV_W), cw)
    c_out = _nsa_prompt(qc.reshape(B, S, C_W), kvcb.reshape(B, S, 6 * KV_W), kcv,
                        misc.reshape(B, S, 128), tb)
    y = _mix_ffn(a_out.reshape(R, A_W), b_out, c_out.reshape(R, C_W), x2, lw, tm_mix)
    wb = min(WINDOW, S)
    kv5 = lambda r: r.reshape(B, S, 2, C_KV, HEAD_DIM)
    states = (kva.reshape(B, S, 2, A_HEADS, HEAD_DIM), lf, kv5(cmp_r), kv5(slc_r), kv5(win_r)[:, S - wb:])
    return y.reshape(B, S, D_MODEL), states


def _layer_sample(layer, x, lw, cw, sb, pools, win_state, page_table):
    nb = x.shape[0]
    fox_t, logf_t, cmp_pool, slc_t, win_t = pools
    npages = page_table.shape[1]
    page = fox_t.shape[3]
    past = npages * page
    npp = min(PAGES_PER_STEP, npages)
    x2 = x.reshape(nb, D_MODEL)
    qa, kva, _, bu, vn, qc, cmp_r, slc_r, win_r, _, misc = _proj(
        x2, lw["g_pre"], lw["w_in"], lw["bf"], lw["lng"], lw["lnb"], nb)
    lf = misc[:, :A_HEADS]

    new_lf = jnp.pad(lf, ((0, 0), (0, HEAD_PAD - A_HEADS)))[:, :, None]
    a_out = _fox_decode(layer, page_table, _lane_repeat(qa, page), fox_t, logf_t, _lane_zero(kva, page), new_lf,
                        npp).reshape(nb, A_W)

    pad_chunk = lambda t: jnp.zeros((nb, CHUNK, B_W), F32).at[:, 0].set(t).reshape(nb * CHUNK, B_W)
    b_out = _gmlp(pad_chunk(bu), pad_chunk(vn), lw["ws"], lw["bst"], min(8, nb)).reshape(nb, CHUNK, B_W)[:, 0]

    qh = jnp.pad(qc.reshape(nb, C_HEADS, HEAD_DIM), ((0, 0), (0, HEAD_PAD - C_HEADS), (0, 0)))
    o_cmp, sel = _cmp_decode(layer, page_table, cmp_pool, cw, qh, npp)
    nsb_all = past // SLC_BLOCK + 1
    per_page = page // SLC_BLOCK
    nflag = (npages + 1) * per_page
    flags = jnp.pad(sel[:, :C_KV, :nsb_all] > 0.5, ((0, 0), (0, 0), (0, nflag - nsb_all)))
    need = flags[:, :, :npages * per_page].reshape(nb, C_KV, npages, per_page).any(axis=(1, 3))
    nslot = -(-min(npages, C_KV * SLC_TOPN) // npp) * npp
    pidx = jnp.arange(npages, dtype=jnp.int32)[None, :]
    order = jnp.argsort(jnp.where(need, pidx, pidx + npages), axis=1)[:, :nslot]
    cnt = jnp.sum(need, axis=1).astype(jnp.int32)
    last = jnp.take_along_axis(order, jnp.maximum(cnt - 1, 0)[:, None], axis=1)
    lpage = jnp.where(jnp.arange(nslot)[None, :] < cnt[:, None], order, last).astype(jnp.int32)
    plist = jnp.take_along_axis(page_table, lpage, axis=1)
    g3 = misc[:, GATE_OFF:GATE_OFF + 3 * C_HEADS].reshape(nb, 3, C_HEADS).transpose(0, 2, 1)
    gates = jnp.repeat(g3, HEAD_DIM, axis=1)
    nbias, fbias, wbias, b0 = sb
    new_t = _lane_zero(jnp.concatenate([slc_r, win_r], axis=1), page)
    c_out = _slc_decode(layer, plist.reshape(-1), lpage.reshape(-1), cnt, flags.reshape(-1).astype(jnp.int32), nflag,
                        _lane_repeat(qc, page), slc_t, nbias, fbias, win_t, wbias, b0, new_t, gates,
                        o_cmp[:, :C_HEADS].reshape(nb, C_W, 1), npp, nslot, npages - 1).reshape(nb, C_W)

    y = _mix_ffn(a_out, b_out, c_out, x2, lw, nb)
    kv5 = lambda r: r.reshape(nb, 1, 2, C_KV, HEAD_DIM)
    win_new = jnp.concatenate([win_state[layer][:, 1:], kv5(win_r)], axis=1)
    states = (kva.reshape(nb, 1, 2, A_HEADS, HEAD_DIM), lf[:, None, :], kv5(cmp_r), kv5(slc_r), win_new,
              vn.reshape(nb, 1, B_GROUPS, HEAD_DIM))
    return y.reshape(nb, 1, D_MODEL), states


def kernel(x_prompt, x_sample, cache_fox_kv, cache_fox_logf, cache_cmp_kv, cache_slc_kv, state_win_kv, page_table, rel_bias_table, norm_pre_mix, w_in, b_forget, gmlp_ln_g, gmlp_ln_b, gmlp_ws, gmlp_bs, cmp_pe, cmp_w1, cmp_b1, cmp_w2, cmp_b2, norm_group_a, norm_group_b, norm_group_c, w_o, norm_post_mix, norm_pre_ffn, w_ffn_in, w_ffn_out, norm_post_ffn):
    depth = w_in.shape[0]
    assert x_sample.shape[1] == 1
    B, S, _ = x_prompt.shape
    nb = x_sample.shape[0]
    n_pool, page = cache_fox_kv.shape[1], cache_fox_kv.shape[2]
    wb = state_win_kv.shape[2]

    w_in_p = _prep_w_in(w_in)
    bf = jnp.pad(b_forget, ((0, 0), (0, LANES - A_HEADS)))[:, None, :]
    cw_prompt, cw_sample = _prep_compress(cmp_pe, cmp_w1, cmp_b1, cmp_w2, cmp_b2)
    tb = _prompt_bias_tiles(rel_bias_table)
    sb = _sample_bias(rel_bias_table, page, wb)
    w_o_b = w_o.astype(BF16)
    w_fi_b = w_ffn_in.astype(BF16)
    w_fo_b = w_ffn_out.astype(BF16)
    bst = gmlp_bs.transpose(0, 2, 1)
    feat_pos = lambda a: jnp.moveaxis(a, 2, -1).reshape(a.shape[0], a.shape[1], -1, a.shape[2])
    chunks = page // CMP_STRIDE
    cmp_x = cache_cmp_kv.reshape(depth, n_pool, chunks, CMP_STRIDE, 2, C_KV, HEAD_DIM)
    cmp_x = cmp_x.transpose(0, 1, 4, 5, 2, 3, 6).reshape(depth, n_pool, 2 * C_KV, chunks, CMP_STRIDE * HEAD_DIM)
    logf_t = jnp.pad(cache_fox_logf.transpose(0, 1, 3, 2), ((0, 0), (0, 0), (0, HEAD_PAD - A_HEADS), (0, 0)))
    pools = (feat_pos(cache_fox_kv), logf_t, cmp_x, feat_pos(cache_slc_kv), feat_pos(state_win_kv))

    tm_proj = min(512, B * S)
    tm_mix = min(256, B * S)
    tq_fox = min(512, S)

    xp, xs = x_prompt, x_sample
    sp = [[] for _ in range(5)]
    ss = [[] for _ in range(6)]
    for l in range(depth):
        r1 = lambda a: a[l][None, :]
        lw = {
            "g_pre": r1(norm_pre_mix), "w_in": w_in_p[l], "bf": bf[l], "lng": r1(gmlp_ln_g), "lnb": r1(gmlp_ln_b),
            "ws": gmlp_ws[l], "bst": bst[l], "ga": r1(norm_group_a), "gb": r1(norm_group_b), "gc": r1(norm_group_c),
            "w_o": w_o_b[l], "gpm": r1(norm_post_mix), "gpf": r1(norm_pre_ffn), "w_ffn_in": w_fi_b[l],
            "w_ffn_out": w_fo_b[l], "gpo": r1(norm_post_ffn),
        }
        xp, st_p = _layer_prompt(xp, lw, {k: v[l] for k, v in cw_prompt.items()}, tb, tm_proj, tm_mix, tq_fox)
        xs, st_s = _layer_sample(l, xs, lw, {k: v[l] for k, v in cw_sample.items()}, sb, pools, state_win_kv,
                                 page_table)
        for lst, s in zip(sp, st_p):
            lst.append(s)
        for lst, s in zip(ss, st_s):
            lst.append(s)
    return tuple([xp, xs] + [jnp.stack(s) for s in sp] + [jnp.stack(s) for s in ss])
```

```python
import functools
import math

import numpy as np
import jax
import jax.numpy as jnp
from jax import lax
from jax.experimental import pallas as pl
from jax.experimental.pallas import tpu as pltpu

F32 = jnp.float32
BF16 = jnp.bfloat16
HIGHEST = lax.Precision.HIGHEST

D_MODEL = 1024
HEAD_DIM = 64
A_HEADS = 6
B_GROUPS = 4
C_HEADS = 6
C_KV = 2
C_REP = C_HEADS // C_KV
A_W = A_HEADS * HEAD_DIM
B_W = B_GROUPS * HEAD_DIM
C_W = C_HEADS * HEAD_DIM
KV_W = C_KV * HEAD_DIM
Q_BLOCK = 128
CHUNK = 128
CMP_LEN = 32
CMP_STRIDE = 16
CMP_HIDDEN = 256
SLC_BLOCK = 64
SLC_TOPN = 16
WINDOW = 512
NUM_BUCKETS = 32
MAX_DISTANCE = 128
D_FF = 2816
EPS = 1e-6
NEG = -1e30
M_INIT = 0.5 * NEG
FORCE = 1e4
SCALE = HEAD_DIM ** -0.5
PROJ_WIDTHS = (A_W, A_W, A_W, A_HEADS, B_W, B_W, C_W, KV_W, KV_W, KV_W, KV_W, KV_W, KV_W, 3 * C_HEADS)

LANES = 128
SUBLANES = 8
HEAD_PAD = 8
VMEM_LIMIT = 56 * 1024 * 1024
PAGES_PER_STEP = 8

P_QA = (0, 384)
P_KVA = (384, 1152)
P_BU = (1152, 1408)
P_BV = (1408, 1664)
P_QC = (1664, 2048)
P_CMP = (2048, 2304)
P_SLC = (2304, 2560)
P_WIN = (2560, 2816)
P_MISC = (2816, 2944)
N_PROJ = 2944
GATE_OFF = A_HEADS


def _dot(a, b):
    return jnp.dot(a, b, preferred_element_type=F32)


def _dot_exact(a, b):
    return jnp.dot(a, b, preferred_element_type=F32, precision=HIGHEST)


def _dot_nt(a, b, precision=None):
    return lax.dot_general(a, b, (((1,), (1,)), ((), ())), preferred_element_type=F32, precision=precision)


def _gelu(x):
    return 0.5 * x * (1.0 + jnp.tanh(math.sqrt(2.0 / math.pi) * (x + 0.044715 * (x * x * x))))


def _sigmoid(x):
    return 1.0 / (1.0 + jnp.exp(-x))


def _rms(x, g):
    return x * lax.rsqrt(jnp.mean(x * x, axis=-1, keepdims=True) + EPS) * g


def _iota(shape, dim):
    return lax.broadcasted_iota(jnp.int32, shape, dim)


def _shr(x, n):
    return jnp.right_shift(x, int(math.log2(n)))


def _cparams(sem):
    return pltpu.CompilerParams(dimension_semantics=sem, vmem_limit_bytes=VMEM_LIMIT)


def _const_spec(shape):
    nd = len(shape)
    return pl.BlockSpec(shape, lambda *_: (0,) * nd)


def _proj_kernel(x_ref, g_ref, w_ref, bf_ref, lng_ref, lnb_ref,
                 qa_ref, kvab_ref, bu_ref, vn_ref, qc_ref, cmp_ref, kvcb_ref, misc_ref,
                 kva_ref=None, slc_ref=None, win_ref=None):
    h = _rms(x_ref[...], g_ref[...]).astype(BF16)

    def mm(seg):
        return _dot(h, w_ref[:, seg[0]:seg[1]])

    qa_ref[...] = (mm(P_QA) * SCALE).astype(BF16)
    kva = mm(P_KVA)
    if kva_ref is not None:
        kva_ref[...] = kva
    kvab_ref[...] = kva.astype(BF16)
    bu_ref[...] = _gelu(mm(P_BU))
    v = _gelu(mm(P_BV))
    mu = jnp.mean(v, axis=-1, keepdims=True)
    var = jnp.mean(jnp.square(v - mu), axis=-1, keepdims=True)
    vn_ref[...] = (v - mu) * lax.rsqrt(var + EPS) * lng_ref[...] + lnb_ref[...]
    qc_ref[...] = (mm(P_QC) * SCALE).astype(BF16)
    c = mm(P_CMP)
    cmp_ref[...] = c
    kvcb_ref[:, 0:256] = c.astype(BF16)
    c = mm(P_SLC)
    if slc_ref is not None:
        slc_ref[...] = c
    kvcb_ref[:, 256:512] = c.astype(BF16)
    c = mm(P_WIN)
    if win_ref is not None:
        win_ref[...] = c
    kvcb_ref[:, 512:768] = c.astype(BF16)
    m = mm(P_MISC) + bf_ref[...]
    lane = _iota(m.shape, 1)
    logsig = jnp.minimum(m, 0.0) - jnp.log1p(jnp.exp(-jnp.abs(m)))
    misc_ref[...] = jnp.where(lane < A_HEADS, logsig, _sigmoid(m))


def _proj(x2d, g, w, bf, lng, lnb, tm, state_rows):
    R = x2d.shape[0]
    row = lambda n: pl.BlockSpec((tm, n), lambda i: (i, 0))
    outs = [("qa", 384, BF16), ("kvab", 768, BF16), ("bu", 256, F32), ("vn", 256, F32), ("qc", 384, BF16),
            ("cmp", 256, F32), ("kvcb", 768, BF16), ("misc", 128, F32)]
    if state_rows:
        outs += [("kva", 768, F32), ("slc", 256, F32), ("win", 256, F32)]
    res = pl.pallas_call(
        _proj_kernel,
        grid=(R // tm,),
        in_specs=[row(D_MODEL), _const_spec((1, D_MODEL)), _const_spec((D_MODEL, N_PROJ)),
                  _const_spec((1, 128)), _const_spec((1, B_W)), _const_spec((1, B_W))],
        out_specs=[row(n) for _, n, _ in outs],
        out_shape=[jax.ShapeDtypeStruct((R, n), dt) for _, n, dt in outs],
        compiler_params=_cparams(("parallel",)),
        name="proj",
    )(x2d, g, w, bf, lng, lnb)
    return {name: r for (name, _, _), r in zip(outs, res)}


STATE_FEATS = (2 * A_W, 2 * KV_W, 2 * KV_W, 2 * KV_W)


def _state_kernel(x_ref, g_ref, wt_ref, *refs):
    outs = refs[-len(STATE_FEATS):]
    h = _rms(x_ref[...], g_ref[...]).astype(BF16)
    lo = 0
    for o_ref, n in zip(outs, STATE_FEATS):
        o_ref[...] = _dot_nt(wt_ref[lo:lo + n, :], h)
        lo += n


def _state_proj(layer, x2d, g, wt, prev, depth, B, S, tm):
    nblk = S // tm
    shapes = [jax.ShapeDtypeStruct((depth, B, n, S), F32) for n in STATE_FEATS]
    out_specs = [pl.BlockSpec((None, None, n, tm), lambda i: (layer, i // nblk, 0, i % nblk)) for n in STATE_FEATS]
    in_specs = [pl.BlockSpec((tm, D_MODEL), lambda i: (i, 0)), _const_spec((1, D_MODEL)), _const_spec(wt.shape)]
    args = [x2d, g, wt]
    aliases = {}
    if prev is not None:
        in_specs += [pl.BlockSpec(memory_space=pl.ANY)] * len(prev)
        aliases = {len(args) + k: k for k in range(len(prev))}
        args += list(prev)
    return pl.pallas_call(
        _state_kernel,
        grid=(B * nblk,),
        in_specs=in_specs,
        out_specs=out_specs,
        out_shape=shapes,
        input_output_aliases=aliases,
        compiler_params=_cparams(("arbitrary",)),
        name="state_proj",
    )(*args)


def _fox_kernel(q_ref, k_ref, v_ref, c_ref, o_ref, *, tq):
    qi = pl.program_id(2)
    tk = tq
    q = q_ref[...]
    lane = _iota((1, LANES), 1)
    low = lane < HEAD_DIM
    zero = jnp.zeros_like(q)
    qm = (jnp.where(low, q, zero), jnp.where(low, zero, q))
    one_lane = (HEAD_DIM, 0)

    def v_aug(v, hh):
        mine = low if hh == 0 else jnp.logical_not(low)
        ones = jnp.where(lane == one_lane[hh], 1.0, 0.0).astype(BF16)
        return jnp.where(mine, v, jnp.broadcast_to(ones, v.shape))

    def tile(ki, carry, causal):
        base = pl.multiple_of(ki * tk, tk)
        k = k_ref[pl.ds(base, tk), :]
        v = v_ref[pl.ds(base, tk), :]
        out = []
        for hh in range(2):
            m, acc = carry[hh]
            s = _dot_nt(qm[hh], k) - c_ref[hh:hh + 1, pl.ds(base, tk)]
            if causal is not None:
                s = jnp.where(causal, s, NEG)
            m_new = jnp.maximum(m, jnp.max(s, axis=-1, keepdims=True))
            alpha = jnp.exp(m - m_new)
            p = jnp.exp(s - m_new)
            acc = alpha * acc + _dot(p.astype(BF16), v_aug(v, hh))
            out.append((m_new, acc))
        return tuple(out)

    init = tuple((jnp.full((tq, 1), M_INIT, F32), jnp.zeros((tq, LANES), F32)) for _ in range(2))
    carry = lax.fori_loop(0, qi, lambda ki, c: tile(ki, c, None), init)
    causal = _iota((1, tk), 1) <= _iota((tq, 1), 0)
    (_, acc0), (_, acc1) = tile(qi, carry, causal)
    l0 = acc0[:, one_lane[0]:one_lane[0] + 1]
    l1 = acc1[:, one_lane[1]:one_lane[1] + 1]
    o_ref[...] = jnp.where(low, acc0 * (1.0 / l0), acc1 * (1.0 / l1))


def _fox_prompt(qa, kvab, crow, tq):
    B, S, _ = qa.shape
    half = A_HEADS // 2
    return pl.pallas_call(
        functools.partial(_fox_kernel, tq=tq),
        grid=(B, half, S // tq),
        in_specs=[
            pl.BlockSpec((None, tq, LANES), lambda b, p, qi: (b, qi, p)),
            pl.BlockSpec((None, S, LANES), lambda b, p, qi: (b, 0, p)),
            pl.BlockSpec((None, S, LANES), lambda b, p, qi: (b, 0, half + p)),
            pl.BlockSpec((None, None, 2, S), lambda b, p, qi: (b, p, 0, 0)),
        ],
        out_specs=pl.BlockSpec((None, tq, LANES), lambda b, p, qi: (b, qi, p)),
        out_shape=jax.ShapeDtypeStruct((B, S, A_W), F32),
        compiler_params=_cparams(("parallel", "parallel", "arbitrary")),
        name="fox_prompt",
    )(qa, kvab, kvab, crow)


def _cumsum_kernel(x_ref, o_ref):
    upper = jnp.where(_iota((LANES, LANES), 0) <= _iota((LANES, LANES), 1), 1.0, 0.0)
    carry = jnp.zeros((x_ref.shape[0], 1), F32)
    for i in range(x_ref.shape[1] // LANES):
        lanes = slice(i * LANES, (i + 1) * LANES)
        c = _dot_exact(x_ref[:, lanes], upper) + carry
        o_ref[:, lanes] = c
        carry = c[:, LANES - 1:LANES]


def _cumsum_lanes(x):
    n, h, length = x.shape
    spec = pl.BlockSpec((None, h, length), lambda i: (i, 0, 0))
    return pl.pallas_call(
        _cumsum_kernel, grid=(n,), in_specs=[spec], out_specs=spec,
        out_shape=jax.ShapeDtypeStruct(x.shape, F32),
        compiler_params=_cparams(("parallel",)),
        name="cumsum_logf",
    )(x)


def _gmlp_kernel(u_ref, vn_ref, ws_ref, bst_ref, o_ref, *, nchunk):
    tril = _iota((CHUNK, CHUNK), 1) <= _iota((CHUNK, CHUNK), 0)
    lane = _iota((1, B_W), 1)
    ws = [jnp.where(tril, ws_ref[h], 0.0).astype(BF16) for h in range(B_GROUPS)]
    for c in range(nchunk):
        rows = slice(c * CHUNK, (c + 1) * CHUNK)
        vn = vn_ref[rows, :].astype(BF16)
        z = jnp.zeros((CHUNK, B_W), F32)
        for h in range(B_GROUPS):
            zh = _dot(ws[h], vn) + bst_ref[:, h:h + 1]
            z = jnp.where(_shr(lane, HEAD_DIM) == h, zh, z)
        o_ref[rows, :] = u_ref[rows, :] * z


def _gmlp(u, vn, ws, bst, nchunk):
    R = u.shape[0]
    tm = nchunk * CHUNK
    row = pl.BlockSpec((tm, B_W), lambda i: (i, 0))
    return pl.pallas_call(
        functools.partial(_gmlp_kernel, nchunk=nchunk),
        grid=(R // tm,),
        in_specs=[row, row, _const_spec((B_GROUPS, CHUNK, CHUNK)), _const_spec((CHUNK, B_GROUPS))],
        out_specs=row,
        out_shape=jax.ShapeDtypeStruct((R, B_W), F32),
        compiler_params=_cparams(("parallel",)),
        name="gmlp",
    )(u, vn, ws, bst)


def _compress_core(x, w, pe_a, pe_b, b1, w2, b2):
    nc = x.shape[0]
    half = w.shape[1] // 2
    h = _dot(x, w)
    const = _dot(pe_a, w[:, :half]) + _dot(pe_b, w[:, half:])
    second = pltpu.roll(h[:, half:], nc - 1, 0)
    hid = _gelu(h[:, :half] + second + const[0:1, :] + b1)
    return _dot(hid.astype(BF16), w2) + b2


def _cmp_prompt_kernel(krows_ref, vrows_ref, wab_ref, pe_ref, b1_ref, w2_ref, b2_ref, o_ref, *, nc):
    for kv, rows_ref in enumerate((krows_ref, vrows_ref)):
        x = jnp.concatenate(
            [rows_ref[pl.ds(l, nc, stride=CMP_STRIDE), :] for l in range(CMP_STRIDE)], axis=-1).astype(BF16)
        o_ref[:, kv * KV_W:(kv + 1) * KV_W] = _compress_core(
            x, wab_ref[kv], pe_ref[kv, 0], pe_ref[kv, 1], b1_ref[kv], w2_ref[kv], b2_ref[kv]).astype(BF16)


def _compress_prompt(cmp_rows, cw):
    B, S, _ = cmp_rows.shape
    nc = S // CMP_STRIDE
    return pl.pallas_call(
        functools.partial(_cmp_prompt_kernel, nc=nc),
        grid=(B,),
        in_specs=[pl.BlockSpec((None, S, KV_W), lambda b: (b, 0, 0)), pl.BlockSpec((None, S, KV_W), lambda b: (b, 0, 1)),
                  _const_spec(cw["wab"].shape), _const_spec(cw["pe"].shape), _const_spec(cw["b1t"].shape),
                  _const_spec(cw["w2bd"].shape), _const_spec(cw["b2t"].shape)],
        out_specs=pl.BlockSpec((None, nc, 2 * KV_W), lambda b: (b, 0, 0)),
        out_shape=jax.ShapeDtypeStruct((B, nc, 2 * KV_W), BF16),
        compiler_params=_cparams(("parallel",)),
        name="compress_prompt",
    )(cmp_rows, cmp_rows, cw["wab"], cw["pe"], cw["b1t"], cw["w2bd"], cw["b2t"])


def _rank_select_t(score_t, topn):
    n = score_t.shape[0]
    groups = [score_t[lo:lo + SUBLANES] for lo in range(0, n, SUBLANES)]
    sub = _iota((SUBLANES, 1), 0)
    ranks = [jnp.zeros(g.shape, F32) for g in groups]
    for jp in range(n):
        row = score_t[jp:jp + 1, :]
        for gi, sg in enumerate(groups):
            lo = gi * SUBLANES
            if lo + SUBLANES - 1 <= jp:
                beats = row > sg
            elif lo > jp:
                beats = row >= sg
            else:
                beats = (row > sg) | ((row == sg) & (sub + lo > jp))
            ranks[gi] = ranks[gi] + jnp.where(beats, 1.0, 0.0)
    return jnp.concatenate([jnp.where(r < topn, 1.0, 0.0) for r in ranks], axis=0)


def _attend_tiles(qg, tiles, carry):
    m, acc = carry
    s_parts = []
    for k, _, bias, mask in tiles:
        s = _dot_nt(qg, k)
        if bias is not None:
            s = s + bias
        s_parts.append(jnp.where(mask, s, NEG))
    m_new = m
    for s in s_parts:
        m_new = jnp.maximum(m_new, jnp.max(s, axis=-1, keepdims=True))
    acc = jnp.exp(m - m_new) * acc
    for s, (_, v, _, _) in zip(s_parts, tiles):
        acc = acc + _dot(jnp.exp(s - m_new).astype(BF16), v)
    return m_new, acc


def _nsa_kernel(q_ref, kv_ref, cmp_ref, misc_ref, tb_ref, o_ref, *, nbp, nsb):
    qi = pl.program_id(1)
    tq = Q_BLOCK
    nrow = C_REP * tq
    t0 = qi * tq
    trow = t0 + _iota((tq, 1), 0)
    trow3 = jnp.concatenate([trow] * C_REP, axis=0)
    tl3 = trow3 - t0
    sl = _iota((1, tq), 1)
    causal3 = sl <= tl3
    topn = min(SLC_TOPN, nsb)
    lane = _iota((1, LANES), 1)
    low = lane < HEAD_DIM
    per = SLC_BLOCK // CMP_STRIDE
    pool_t = jnp.where(_shr(_iota((nsb, nbp), 1), per) == _iota((nsb, nbp), 0), 1.0, 0.0)
    groups = range(C_KV)
    mine = [low, jnp.logical_not(low)]
    one_lane = [HEAD_DIM, 0]
    ones = [jnp.where(lane == one_lane[g], 1.0, 0.0).astype(BF16) for g in groups]
    ks_lo, vs_lo, kw_lo, vw_lo = 2 * KV_W, 3 * KV_W, 4 * KV_W, 5 * KV_W

    def kv_tile(base, width, k_lo, v_lo):
        k = kv_ref[pl.ds(base, width), k_lo:k_lo + KV_W]
        v = kv_ref[pl.ds(base, width), v_lo:v_lo + KV_W]
        return k, [jnp.where(mine[g], v, jnp.broadcast_to(ones[g], v.shape)) for g in groups]

    qg, o_cmp, sel = [], [], []
    for g in groups:
        parts = []
        for r in range(C_REP):
            h = C_REP * g + r
            src = q_ref[:, (h // 2) * LANES:(h // 2 + 1) * LANES].astype(F32)
            if h % 2 != g:
                src = pltpu.roll(src, HEAD_DIM, 1)
            parts.append(jnp.where(mine[g], src, 0.0))
        q = jnp.concatenate(parts, axis=0).astype(BF16)
        qg.append(q)

        s = _dot_nt(q, cmp_ref[:, 0:KV_W])
        valid = (_iota((1, nbp), 1) * CMP_STRIDE + (CMP_LEN - 1)) <= trow3
        s = jnp.where(valid, s, NEG)
        e = jnp.exp(s - jnp.max(s, axis=-1, keepdims=True))
        prob = jnp.where(valid, e * (1.0 / jnp.sum(e, axis=-1, keepdims=True)), 0.0)
        o_cmp.append(_dot(prob.astype(BF16), cmp_ref[:, KV_W:2 * KV_W]))

        psum = prob[0:tq] + prob[tq:2 * tq] + prob[2 * tq:3 * tq]
        imp_t = _dot_nt(pool_t, psum, HIGHEST)
        j = _iota((nsb, 1), 0)
        tcol = t0 + _iota((1, tq), 1)
        cur = _shr(tcol, SLC_BLOCK)
        forced = (j == 0) | (j == cur) | (j == cur - 1)
        score_t = jnp.where(forced, FORCE, jnp.where(j * SLC_BLOCK <= tcol, imp_t, -1.0))
        sel.append(_rank_select_t(score_t, topn).T.astype(BF16))

    def sel_masks(base, width):
        blk = base // SLC_BLOCK + _shr(_iota((nsb, width), 1), SLC_BLOCK)
        expand = jnp.where(_iota((nsb, width), 0) == blk, 1.0, 0.0).astype(BF16)
        return [jnp.concatenate([_dot(sel[g], expand)] * C_REP, axis=0) > 0.5 for g in groups]

    init = (jnp.full((nrow, 1), M_INIT, F32), jnp.zeros((nrow, LANES), F32))

    def far_step(base, width, carry):
        k, v = kv_tile(base, width, ks_lo, vs_lo)
        masks = sel_masks(base, width)
        return tuple(_attend_tiles(qg[g], [(k, v[g], None, masks[g])], carry[g]) for g in groups)

    n_far = jnp.maximum(qi - 1, 0)
    n_big = n_far // 4
    carry = lax.fori_loop(0, n_big, lambda i, c: far_step(pl.multiple_of(i * (4 * tq), 4 * tq), 4 * tq, c),
                          (init, init))
    carry = lax.fori_loop(0, n_far - 4 * n_big,
                          lambda i, c: far_step(pl.multiple_of((n_big * 4 + i) * tq, tq), tq, c), carry)

    pbase = pl.multiple_of(jnp.maximum(qi - 1, 0) * tq, tq)
    dbase = pl.multiple_of(t0, tq)
    has_prev = qi >= 1
    far_w = WINDOW - tq
    fbase = pl.multiple_of(jnp.maximum(t0 - WINDOW, 0), tq)
    kpos = fbase + _iota((1, far_w), 1)
    far_mask = (kpos < t0 - tq) & (trow3 - kpos <= WINDOW)
    mask_p = sel_masks(pbase, tq)
    mask_d = sel_masks(dbase, tq)
    ksp, vsp = kv_tile(pbase, tq, ks_lo, vs_lo)
    ksd, vsd = kv_tile(dbase, tq, ks_lo, vs_lo)
    kwf, vwf = kv_tile(fbase, far_w, kw_lo, vw_lo)
    kwp, vwp = kv_tile(pbase, tq, kw_lo, vw_lo)
    kwd, vwd = kv_tile(dbase, tq, kw_lo, vw_lo)
    heads = [None] * C_HEADS
    for g in groups:
        bias_diag = tb_ref[g, 0]
        bias_prev = tb_ref[g, 1]
        _, acc = _attend_tiles(qg[g], [(ksp, vsp[g], bias_prev, mask_p[g] & has_prev),
                                       (ksd, vsd[g], bias_diag, mask_d[g] & causal3)], carry[g])
        o_slc = acc * (1.0 / acc[:, one_lane[g]:one_lane[g] + 1])
        cw = _attend_tiles(qg[g], [(kwf, vwf[g], None, far_mask)], init)
        _, acc = _attend_tiles(qg[g], [(kwp, vwp[g], bias_prev, (tl3 >= 0) & has_prev),
                                       (kwd, vwd[g], bias_diag, causal3)], cw)
        o_win = acc * (1.0 / acc[:, one_lane[g]:one_lane[g] + 1])
        for r in range(C_REP):
            h = C_REP * g + r
            rows = slice(r * tq, (r + 1) * tq)
            gate = lambda br: misc_ref[:, GATE_OFF + br * C_HEADS + h:GATE_OFF + br * C_HEADS + h + 1]
            heads[h] = gate(0) * o_cmp[g][rows] + gate(1) * o_slc[rows] + gate(2) * o_win[rows]

    for pair in range(C_HEADS // 2):
        halves = []
        for h in (2 * pair, 2 * pair + 1):
            x = heads[h]
            if h // C_REP != h % 2:
                x = pltpu.roll(x, HEAD_DIM, 1)
            halves.append(x)
        o_ref[:, pair * LANES:(pair + 1) * LANES] = jnp.where(low, halves[0], halves[1])


def _nsa_prompt(qc, kvcb, kcv, misc, tb):
    B, S, _ = qc.shape
    nbp = kcv.shape[1]
    nsb = nbp * CMP_STRIDE // SLC_BLOCK
    return pl.pallas_call(
        functools.partial(_nsa_kernel, nbp=nbp, nsb=nsb),
        grid=(B, S // Q_BLOCK),
        in_specs=[pl.BlockSpec((None, Q_BLOCK, C_W), lambda b, i: (b, i, 0)),
                  pl.BlockSpec((None, S, 6 * KV_W), lambda b, i: (b, 0, 0)),
                  pl.BlockSpec((None, nbp, 2 * KV_W), lambda b, i: (b, 0, 0)),
                  pl.BlockSpec((None, Q_BLOCK, 128), lambda b, i: (b, i, 0)),
                  _const_spec(tb.shape)],
        out_specs=pl.BlockSpec((None, Q_BLOCK, C_W), lambda b, i: (b, i, 0)),
        out_shape=jax.ShapeDtypeStruct((B, S, C_W), F32),
        compiler_params=_cparams(("parallel", "arbitrary")),
        name="nsa_prompt",
    )(qc, kvcb, kcv, misc, tb)


def _mix_ffn_kernel(a_ref, b_ref, c_ref, x_ref, ga_ref, gb_ref, gc_ref, wo_ref, gpm_ref, gpf_ref,
                    win_ref, wout_ref, gpo_ref, o_ref, *, nsplit):
    an = _rms(a_ref[...], ga_ref[...]).astype(BF16)
    bn = _rms(b_ref[...], gb_ref[...]).astype(BF16)
    cn = _rms(c_ref[...], gc_ref[...]).astype(BF16)
    mix = (_dot(an, wo_ref[0:A_W, :]) + _dot(bn, wo_ref[A_W:A_W + B_W, :])
           + _dot(cn, wo_ref[A_W + B_W:A_W + B_W + C_W, :]))
    x1 = x_ref[...] + _rms(mix, gpm_ref[...])
    h = _rms(x1, gpf_ref[...]).astype(BF16)
    wid = D_FF // nsplit
    y = jnp.zeros(x1.shape, F32)
    for c in range(nsplit):
        lo = c * wid
        gate = _dot(h, win_ref[:, lo:lo + wid])
        up = _dot(h, win_ref[:, D_FF + lo:D_FF + lo + wid])
        act = gate * _sigmoid(gate) * up
        y = y + _dot(act.astype(BF16), wout_ref[lo:lo + wid, :])
    o_ref[...] = x1 + _rms(y, gpo_ref[...])


def _mix_ffn(a, b, c, x, lw, tm):
    R = x.shape[0]
    row = lambda n: pl.BlockSpec((tm, n), lambda i: (i, 0))
    once = lambda shape: pl.BlockSpec(shape, lambda i: (0, 0), pipeline_mode=pl.Buffered(1))
    return pl.pallas_call(
        functools.partial(_mix_ffn_kernel, nsplit=2),
        grid=(R // tm,),
        in_specs=[row(A_W), row(B_W), row(C_W), row(D_MODEL),
                  once((1, A_W)), once((1, B_W)), once((1, C_W)), once((D_MODEL, D_MODEL)),
                  once((1, D_MODEL)), once((1, D_MODEL)), once((D_MODEL, 2 * D_FF)), once((D_FF, D_MODEL)),
                  once((1, D_MODEL))],
        out_specs=row(D_MODEL),
        out_shape=jax.ShapeDtypeStruct((R, D_MODEL), F32),
        compiler_params=_cparams(("parallel",)),
        name="mix_ffn",
    )(a, b, c, x, lw["ga"], lw["gb"], lw["gc"], lw["w_o"], lw["gpm"], lw["gpf"], lw["w_ffn_in"],
      lw["w_ffn_out"], lw["gpo"])


def _head_block(x, h):
    return x[h * HEAD_DIM:(h + 1) * HEAD_DIM]


def _pad_heads(x):
    return jnp.concatenate([x, jnp.zeros((HEAD_PAD - x.shape[0], x.shape[1]), F32)], axis=0)


def _row_softmax_step(s, m_prev):
    m_new = jnp.maximum(m_prev, jnp.max(s, axis=1, keepdims=True))
    return m_new, jnp.exp(m_prev - m_new), jnp.exp(s - m_new)


def _fox_dec_kernel(pt_ref, q_ref, *refs, npp):
    pages = refs[:npp]
    lfs = refs[npp:2 * npp]
    new_ref, newlf_ref, o_ref, m_sc, l_sc, c_sc, acc_sc = refs[2 * npp:]
    j = pl.program_id(1)
    page = pages[0].shape[1]
    q = q_ref[...]
    upper = jnp.where(_iota((page, page), 0) <= _iota((page, page), 1), 1.0, 0.0)

    @pl.when(j == 0)
    def _():
        m_sc[...] = jnp.full(m_sc.shape, M_INIT, F32)
        l_sc[...] = jnp.zeros(l_sc.shape, F32)
        c_sc[...] = jnp.zeros(c_sc.shape, F32)
        acc_sc[...] = jnp.zeros(acc_sc.shape, F32)

    def scores(kt):
        prod = kt * q
        return _pad_heads(jnp.concatenate(
            [jnp.sum(_head_block(prod, h), axis=0, keepdims=True) for h in range(A_HEADS)], axis=0))

    def accumulate(alpha, p_tiles, v_tiles):
        for h in range(A_HEADS):
            a = _head_block(acc_sc, h) * alpha[h:h + 1, :]
            for p, v in zip(p_tiles, v_tiles):
                a = a + p[h:h + 1, :] * v[A_W + h * HEAD_DIM:A_W + (h + 1) * HEAD_DIM, :]
            acc_sc[h * HEAD_DIM:(h + 1) * HEAD_DIM, :] = a

    c_local = _dot_exact(jnp.concatenate([lf[...] for lf in lfs], axis=0), upper)
    totals = [jnp.broadcast_to(c_local[i * HEAD_PAD:(i + 1) * HEAD_PAD, page - 1:page], (HEAD_PAD, page))
              for i in range(npp)]
    before = jnp.broadcast_to(c_sc[...], (HEAD_PAD, page))
    s_parts = []
    for i, pg in enumerate(pages):
        s_parts.append(scores(pg[0:A_W, :]) - (c_local[i * HEAD_PAD:(i + 1) * HEAD_PAD] + before))
        before = before + totals[i]
    c_sc[...] = before[:, 0:1]
    m_new, alpha, p = _row_softmax_step(jnp.concatenate(s_parts, axis=1), m_sc[...])
    l_sc[...] = alpha * l_sc[...] + jnp.sum(p, axis=1, keepdims=True)
    m_sc[...] = m_new
    accumulate(alpha, [p[:, i * page:(i + 1) * page] for i in range(npp)], pages)

    @pl.when(j == pl.num_programs(1) - 1)
    def _():
        first = _iota((1, page), 1) == 0
        s_new = jnp.where(first, scores(new_ref[0:A_W, :]) - (c_sc[...] + newlf_ref[...]), NEG)
        m_new, alpha, p_new = _row_softmax_step(s_new, m_sc[...])
        l_inv = 1.0 / (alpha * l_sc[...] + jnp.sum(p_new, axis=1, keepdims=True))
        accumulate(alpha, [p_new], [new_ref])
        for h in range(A_HEADS):
            rows = slice(h * HEAD_DIM, (h + 1) * HEAD_DIM)
            o_ref[rows, :] = jnp.sum(acc_sc[rows, :], axis=1, keepdims=True) * l_inv[h:h + 1, :]


def _fox_decode(layer, page_table, q_rep, pool_t, logf_t, new_t, new_lf, npp):
    nb, npages = page_table.shape
    page = pool_t.shape[3]
    page_of = lambda b, j, i, pt: pt[b, j * npp + i]
    page_spec = lambda i: pl.BlockSpec((None, None, 2 * A_W, page), lambda b, j, pt: (layer, page_of(b, j, i, pt), 0, 0))
    logf_spec = lambda i: pl.BlockSpec((None, None, HEAD_PAD, page), lambda b, j, pt: (layer, page_of(b, j, i, pt), 0, 0))
    per_b = lambda shape: pl.BlockSpec((None,) + shape, lambda b, j, pt: (b,) + (0,) * len(shape))
    grid_spec = pltpu.PrefetchScalarGridSpec(
        num_scalar_prefetch=1,
        grid=(nb, npages // npp),
        in_specs=[per_b((A_W, page))] + [page_spec(i) for i in range(npp)] + [logf_spec(i) for i in range(npp)]
        + [per_b((2 * A_W, page)), per_b((HEAD_PAD, 1))],
        out_specs=per_b((A_W, 1)),
        scratch_shapes=[pltpu.VMEM((HEAD_PAD, 1), F32), pltpu.VMEM((HEAD_PAD, 1), F32), pltpu.VMEM((HEAD_PAD, 1), F32),
                        pltpu.VMEM((A_W, page), F32)],
    )
    return pl.pallas_call(
        functools.partial(_fox_dec_kernel, npp=npp),
        grid_spec=grid_spec,
        out_shape=jax.ShapeDtypeStruct((nb, A_W, 1), F32),
        compiler_params=_cparams(("parallel", "arbitrary")),
        name="fox_decode",
    )(page_table, q_rep, *([pool_t] * npp), *([logf_t] * npp), new_t, new_lf)


def _cmp_dec_kernel(pt_ref, *refs, npp, nc, nsb_all):
    pages = refs[:npp]
    w_ref, pe_ref, b1_ref, w2_ref, b2_ref, q_ref, ocmp_ref, sel_ref, x_sc = refs[npp:]
    j = pl.program_id(1)
    per_page = pages[0].shape[1]
    for i, pg in enumerate(pages):
        row0 = pl.multiple_of((j * npp + i) * per_page, per_page)
        for c in range(2 * C_KV):
            x_sc[c, pl.ds(row0, per_page), :] = pg[c]

    @pl.when(j == pl.num_programs(1) - 1)
    def _():
        kc = [[_compress_core(x_sc[kv * C_KV + g].astype(BF16), w_ref[kv], pe_ref[kv, 0], pe_ref[kv, 1],
                              b1_ref[kv], w2_ref[kv], b2_ref[kv]).astype(BF16)
               for g in range(C_KV)] for kv in range(2)]
        q = q_ref[...]
        grp0 = _iota((HEAD_PAD, 1), 0) < C_REP
        pick = lambda x0, x1: jnp.where(grp0, x0, x1)
        s = pick(_dot_nt(q, kc[0][0]), _dot_nt(q, kc[0][1]))
        valid = _iota((1, nc), 1) < nc - 1
        s = jnp.where(valid, s, NEG)
        e = jnp.exp(s - jnp.max(s, axis=-1, keepdims=True))
        prob = jnp.where(valid, e / jnp.sum(e, axis=-1, keepdims=True), 0.0)
        pb = prob.astype(BF16)
        ocmp_ref[...] = pick(_dot(pb, kc[1][0]), _dot(pb, kc[1][1]))

        per = SLC_BLOCK // CMP_STRIDE
        nsbp = sel_ref.shape[1]
        hrow = _iota((HEAD_PAD, 1), 0)
        psum = jnp.concatenate(
            [jnp.sum(jnp.where((hrow >= C_REP * g) & (hrow < C_REP * (g + 1)), prob, 0.0), axis=0, keepdims=True)
             for g in range(C_KV)] + [jnp.zeros((HEAD_PAD - C_KV, nc), F32)], axis=0)
        pool = jnp.where(_shr(_iota((nc, nsbp), 0), per) == _iota((nc, nsbp), 1), 1.0, 0.0)
        imp = _dot_exact(psum, pool)
        jl = _iota((1, nsbp), 1)
        cur = nsb_all - 1
        forced = (jl == 0) | (jl == cur) | (jl == cur - 1)
        score = jnp.where(forced, FORCE, jnp.where(jl <= cur, imp, -2.0))
        topn = min(SLC_TOPN, nsb_all)
        rank = jnp.zeros(score.shape, F32)
        for jp in range(nsb_all):
            col = score[:, jp:jp + 1]
            beats = (col > score) | ((col == score) & (jp < jl))
            rank = rank + jnp.where(beats, 1.0, 0.0)
        sel_ref[...] = jnp.where(rank < topn, 1.0, 0.0)


def _cmp_decode(layer, page_table, pool, cw, qh, npp):
    nb, npages = page_table.shape
    per_page, width = pool.shape[3], pool.shape[4]
    nc = npages * per_page
    nsb_all = nc * CMP_STRIDE // SLC_BLOCK + 1
    nsbp = -(-nsb_all // LANES) * LANES

    def page_spec(i):
        return pl.BlockSpec((None, None, 2 * C_KV, per_page, width),
                            lambda b, j, pt: (layer, pt[b, j * npp + i], 0, 0, 0))

    const = lambda a: pl.BlockSpec(a.shape, lambda b, j, pt: (0,) * a.ndim)
    grid_spec = pltpu.PrefetchScalarGridSpec(
        num_scalar_prefetch=1,
        grid=(nb, npages // npp),
        in_specs=[page_spec(i) for i in range(npp)]
        + [const(cw["w"]), const(cw["pe"]), const(cw["b1"]), const(cw["w2"]), const(cw["b2"]),
           pl.BlockSpec((None, HEAD_PAD, HEAD_DIM), lambda b, j, pt: (b, 0, 0))],
        out_specs=[pl.BlockSpec((None, HEAD_PAD, HEAD_DIM), lambda b, j, pt: (b, 0, 0)),
                   pl.BlockSpec((None, HEAD_PAD, nsbp), lambda b, j, pt: (b, 0, 0))],
        scratch_shapes=[pltpu.VMEM((2 * C_KV, nc, width), F32)],
    )
    return pl.pallas_call(
        functools.partial(_cmp_dec_kernel, npp=npp, nc=nc, nsb_all=nsb_all),
        grid_spec=grid_spec,
        out_shape=[jax.ShapeDtypeStruct((nb, HEAD_PAD, HEAD_DIM), F32),
                   jax.ShapeDtypeStruct((nb, HEAD_PAD, nsbp), F32)],
        compiler_params=_cparams(("parallel", "arbitrary")),
        name="cmp_decode",
    )(page_table, *([pool] * npp), cw["w"], cw["pe"], cw["b1"], cw["w2"], cw["b2"], qh)


def _slc_dec_kernel(plist_ref, lpage_ref, cnt_ref, flag_ref, q_ref, *refs, npp, nslot, nflag, last_page):
    pages = refs[:npp]
    (nbias_ref, fbias_ref, win_ref, wbias_ref, b0_ref, new_ref, gate_ref, ocmp_ref, o_ref,
     m_sc, l_sc, acc_sc) = refs[npp:]
    b = pl.program_id(0)
    j = pl.program_id(1)
    page = pages[0].shape[1]
    q = q_ref[...]
    grp0 = _iota((HEAD_PAD, 1), 0) < C_REP
    lane = _iota((1, page), 1)

    def scores(kt):
        return _pad_heads(jnp.concatenate(
            [jnp.sum(_head_block(kt, h // C_REP) * _head_block(q, h), axis=0, keepdims=True)
             for h in range(C_HEADS)], axis=0))

    def weighted(acc, alpha, p_tiles, v_tiles):
        out = []
        for h in range(C_HEADS):
            a = _head_block(acc, h) * alpha[h:h + 1, :]
            for p, v in zip(p_tiles, v_tiles):
                a = a + p[h:h + 1, :] * _head_block(v, h // C_REP)
            out.append(a)
        return jnp.concatenate(out, axis=0)

    def finish(acc, l):
        l_inv = 1.0 / l
        return jnp.concatenate(
            [jnp.sum(_head_block(acc, h), axis=1, keepdims=True) * l_inv[h:h + 1, :] for h in range(C_HEADS)], axis=0)

    @pl.when(j == 0)
    def _():
        m_sc[...] = jnp.full(m_sc.shape, M_INIT, F32)
        l_sc[...] = jnp.zeros(l_sc.shape, F32)
        acc_sc[...] = jnp.zeros(acc_sc.shape, F32)

    nblk = page // SLC_BLOCK
    s_parts = []
    for i, pg in enumerate(pages):
        slot = j * npp + i
        lp = lpage_ref[b * nslot + slot]
        live = slot < cnt_ref[b]
        mask = jnp.zeros((HEAD_PAD, page), jnp.int32)
        for blk in range(nblk):
            f = [flag_ref[(b * C_KV + g) * nflag + lp * nblk + blk] for g in range(C_KV)]
            mask = jnp.where(_shr(lane, SLC_BLOCK) == blk, jnp.where(grp0, f[0], f[1]), mask)
        mask = (mask > 0) & live
        bias = jnp.where(lp == last_page, nbias_ref[...], fbias_ref[...])
        s_parts.append(jnp.where(mask, scores(pg[0:KV_W, :]) + bias, NEG))
    m_new, alpha, p = _row_softmax_step(jnp.concatenate(s_parts, axis=1), m_sc[...])
    l_sc[...] = alpha * l_sc[...] + jnp.sum(p, axis=1, keepdims=True)
    m_sc[...] = m_new
    acc_sc[...] = weighted(acc_sc[...], alpha, [p[:, i * page:(i + 1) * page] for i in range(npp)],
                           [pg[KV_W:2 * KV_W, :] for pg in pages])

    @pl.when(j == pl.num_programs(1) - 1)
    def _():
        first = lane == 0
        s_new = jnp.where(first, scores(new_ref[0:KV_W, :]) + b0_ref[...], NEG)
        m_new, alpha, p_new = _row_softmax_step(s_new, m_sc[...])
        l = alpha * l_sc[...] + jnp.sum(p_new, axis=1, keepdims=True)
        o_slc = finish(weighted(acc_sc[...], alpha, [p_new], [new_ref[KV_W:2 * KV_W, :]]), l)
        nwin = win_ref.shape[1] // page
        tiles = lambda ref, lo: [ref[lo:lo + KV_W, c * page:(c + 1) * page] for c in range(nwin)]
        s_win = [scores(kt) + wbias_ref[:, c * page:(c + 1) * page] for c, kt in enumerate(tiles(win_ref, 0))]
        s_win.append(jnp.where(first, scores(new_ref[2 * KV_W:3 * KV_W, :]) + b0_ref[...], NEG))
        s = jnp.concatenate(s_win, axis=1)
        p = jnp.exp(s - jnp.max(s, axis=1, keepdims=True))
        acc = weighted(jnp.zeros((C_W, page), F32), jnp.zeros((HEAD_PAD, 1), F32),
                       [p[:, c * page:(c + 1) * page] for c in range(nwin + 1)],
                       tiles(win_ref, KV_W) + [new_ref[3 * KV_W:4 * KV_W, :]])
        o_win = finish(acc, jnp.sum(p, axis=1, keepdims=True))
        o_ref[...] = gate_ref[:, 0:1] * ocmp_ref[...] + gate_ref[:, 1:2] * o_slc + gate_ref[:, 2:3] * o_win


def _slc_decode(layer, plist, lpage, cnt, flags, nflag, q_rep, pool_t, nbias, fbias, win_t, wbias, b0, new_t, gates,
                ocmp, npp, nslot, last_page):
    nb = q_rep.shape[0]
    page = pool_t.shape[3]
    wb = win_t.shape[3]

    def page_spec(i):
        return pl.BlockSpec((None, None, 2 * KV_W, page),
                            lambda b, j, pls, lps, cn, fl: (layer, pls[b * nslot + j * npp + i], 0, 0))

    fixed = lambda shape: pl.BlockSpec(shape, lambda b, j, pls, lps, cn, fl: (0,) * len(shape))
    per_b = lambda shape: pl.BlockSpec((None,) + shape, lambda b, j, pls, lps, cn, fl: (b,) + (0,) * len(shape))
    grid_spec = pltpu.PrefetchScalarGridSpec(
        num_scalar_prefetch=4,
        grid=(nb, nslot // npp),
        in_specs=[per_b((C_W, page))] + [page_spec(i) for i in range(npp)]
        + [fixed((HEAD_PAD, page)), fixed((HEAD_PAD, 1)),
           pl.BlockSpec((None, None, 2 * KV_W, wb), lambda b, j, pls, lps, cn, fl: (layer, b, 0, 0)),
           fixed((HEAD_PAD, wb)), fixed((HEAD_PAD, 1)), per_b((4 * KV_W, page)), per_b((C_W, 3)), per_b((C_W, 1))],
        out_specs=per_b((C_W, 1)),
        scratch_shapes=[pltpu.VMEM((HEAD_PAD, 1), F32), pltpu.VMEM((HEAD_PAD, 1), F32), pltpu.VMEM((C_W, page), F32)],
    )
    return pl.pallas_call(
        functools.partial(_slc_dec_kernel, npp=npp, nslot=nslot, nflag=nflag, last_page=last_page),
        grid_spec=grid_spec,
        out_shape=jax.ShapeDtypeStruct((nb, C_W, 1), F32),
        compiler_params=_cparams(("parallel", "arbitrary")),
        name="slc_win_decode",
    )(plist, lpage, cnt, flags, q_rep, *([pool_t] * npp), nbias, fbias, win_t, wbias, b0, new_t, gates, ocmp)


def _t5_bucket(dist):
    n = jnp.maximum(dist, 0)
    max_exact = NUM_BUCKETS // 2
    nf = jnp.maximum(n, 1).astype(F32)
    large = max_exact + (jnp.log(nf / max_exact) / math.log(MAX_DISTANCE / max_exact)
                         * (NUM_BUCKETS - max_exact)).astype(jnp.int32)
    return jnp.where(n < max_exact, n, jnp.minimum(large, NUM_BUCKETS - 1))


def _prep_w_in(w_in):
    splits = [int(s) for s in np.cumsum(PROJ_WIDTHS)[:-1]]
    a_q, a_k, a_v, a_f, b_u, b_v, c_q, c_kc, c_vc, c_ks, c_vs, c_kw, c_vw, c_g = jnp.split(w_in, splits, axis=-1)
    pad = jnp.zeros(w_in.shape[:-1] + (LANES - A_HEADS - 3 * C_HEADS,), w_in.dtype)
    return jnp.concatenate([a_q, a_k, a_v, b_u, b_v, c_q, c_kc, c_vc, c_ks, c_vs, c_kw, c_vw, a_f, c_g, pad],
                           axis=-1).astype(BF16)


def _prep_w_state(w_in):
    splits = [int(s) for s in np.cumsum(PROJ_WIDTHS)[:-1]]
    _, a_k, a_v, _, _, _, _, c_kc, c_vc, c_ks, c_vs, c_kw, c_vw, _ = jnp.split(w_in, splits, axis=-1)
    return jnp.concatenate([a_k, a_v, c_kc, c_vc, c_ks, c_vs, c_kw, c_vw], axis=-1).transpose(0, 2, 1).astype(BF16)


def _prep_compress(cmp_pe, cmp_w1, cmp_b1, cmp_w2, cmp_b2):
    depth = cmp_w1.shape[0]
    half = CMP_LEN // 2
    eye = jnp.eye(C_KV, dtype=F32)
    w1 = cmp_w1.reshape(depth, 2, 2, half, HEAD_DIM, CMP_HIDDEN)
    wab = jnp.einsum("zkhldj,gG->zklgdhGj", w1, eye).reshape(depth, 2, half * KV_W, 2 * C_KV * CMP_HIDDEN)
    pe = cmp_pe.reshape(depth, 2, 2, half, 1, HEAD_DIM)
    pe_bd = jnp.broadcast_to(pe, (depth, 2, 2, half, C_KV, HEAD_DIM)).reshape(depth, 2, 2, 1, half * KV_W)
    pe_bd = jnp.broadcast_to(pe_bd, (depth, 2, 2, SUBLANES, half * KV_W))
    w2bd = jnp.einsum("zkjd,gG->zkgjGd", cmp_w2, eye).reshape(depth, 2, C_KV * CMP_HIDDEN, KV_W)
    w_g = w1.transpose(0, 1, 3, 4, 2, 5).reshape(depth, 2, half * HEAD_DIM, 2 * CMP_HIDDEN)
    pe_g = jnp.broadcast_to(cmp_pe.reshape(depth, 2, 2, 1, half * HEAD_DIM), (depth, 2, 2, SUBLANES, half * HEAD_DIM))
    prompt = {
        "wab": wab.astype(BF16), "pe": pe_bd.astype(BF16),
        "b1t": jnp.tile(cmp_b1, (1, 1, C_KV))[:, :, None, :],
        "w2bd": w2bd.astype(BF16),
        "b2t": jnp.tile(cmp_b2, (1, 1, C_KV))[:, :, None, :],
    }
    sample = {"w": w_g.astype(BF16), "pe": pe_g.astype(BF16), "b1": cmp_b1[:, :, None, :],
              "w2": cmp_w2.astype(BF16), "b2": cmp_b2[:, :, None, :]}
    return prompt, sample


def _prompt_bias_tiles(table):
    tl = jnp.arange(Q_BLOCK)[:, None]
    sl = jnp.arange(Q_BLOCK)[None, :]
    d = tl - sl
    far = table[_t5_bucket(jnp.asarray(8 * MAX_DISTANCE))]
    tiles = []
    for delta in (0, Q_BLOCK):
        b = table[_t5_bucket(d + delta)] - far
        tiles.append(b.transpose(2, 0, 1).reshape(C_KV, C_REP * Q_BLOCK, Q_BLOCK))
    return jnp.stack(tiles, axis=1).astype(F32)


def _sample_bias(table, page, wb):
    rows = lambda x: jnp.pad(x.T, ((0, HEAD_PAD - C_HEADS), (0, 0))).astype(F32)
    nbias = rows(table[_t5_bucket(page - jnp.arange(page))])
    fbias = rows(table[_t5_bucket(jnp.asarray([8 * MAX_DISTANCE]))])
    wbias = rows(table[_t5_bucket(wb - jnp.arange(wb))])
    b0 = rows(table[0:1])
    return nbias, fbias, wbias, b0


def _lane_repeat(x, n):
    return jnp.broadcast_to(x.astype(F32)[:, :, None], x.shape + (n,))


def _lane_zero(x, n):
    return jnp.pad(x.astype(F32)[:, :, None], ((0, 0), (0, 0), (0, n - 1)))


def _layer_prompt(layer, depth, x, lw, cw, tb, states, tm_proj, tm_mix, tq_fox):
    B, S, _ = x.shape
    R = B * S
    x2 = x.reshape(R, D_MODEL)
    p = _proj(x2, lw["g_pre"], lw["w_in"], lw["bf"], lw["lng"], lw["lnb"], tm_proj, state_rows=False)
    states = _state_proj(layer, x2, lw["g_pre"], lw["w_state"], states, depth, B, S, tm_proj)
    lf = p["misc"][:, :A_HEADS].reshape(B, S, A_HEADS)
    lf_t = jnp.pad(lf, ((0, 0), (0, 0), (0, HEAD_PAD - A_HEADS))).transpose(0, 2, 1)
    crow = _cumsum_lanes(lf_t)[:, :A_HEADS].reshape(B, A_HEADS // 2, 2, S)
    a_out = _fox_prompt(p["qa"].reshape(B, S, A_W), p["kvab"].reshape(B, S, 2 * A_W), crow, tq_fox)
    b_out = _gmlp(p["bu"], p["vn"], lw["ws"], lw["bst"], min(8, R // CHUNK))
    kcv = _compress_prompt(p["cmp"].reshape(B, S, 2 * KV_W), cw)
    c_out = _nsa_prompt(p["qc"].reshape(B, S, C_W), p["kvcb"].reshape(B, S, 6 * KV_W), kcv,
                        p["misc"].reshape(B, S, 128), tb)
    y = _mix_ffn(a_out.reshape(R, A_W), b_out, c_out.reshape(R, C_W), x2, lw, tm_mix)
    return y.reshape(B, S, D_MODEL), lf, states


def _layer_sample(layer, x, lw, cw, sb, pools, win_state, page_table):
    nb = x.shape[0]
    fox_t, logf_t, cmp_pool, slc_t, win_t = pools
    npages = page_table.shape[1]
    page = fox_t.shape[3]
    past = npages * page
    npp = min(PAGES_PER_STEP, npages)
    x2 = x.reshape(nb, D_MODEL)
    p = _proj(x2, lw["g_pre"], lw["w_in"], lw["bf"], lw["lng"], lw["lnb"], nb, state_rows=True)
    qa, kva, bu, vn, qc, misc = p["qa"], p["kva"], p["bu"], p["vn"], p["qc"], p["misc"]
    cmp_r, slc_r, win_r = p["cmp"], p["slc"], p["win"]
    lf = misc[:, :A_HEADS]

    new_lf = jnp.pad(lf, ((0, 0), (0, HEAD_PAD - A_HEADS)))[:, :, None]
    a_out = _fox_decode(layer, page_table, _lane_repeat(qa, page), fox_t, logf_t, _lane_zero(kva, page), new_lf,
                        npp).reshape(nb, A_W)

    pad_chunk = lambda t: jnp.zeros((nb, CHUNK, B_W), F32).at[:, 0].set(t).reshape(nb * CHUNK, B_W)
    b_out = _gmlp(pad_chunk(bu), pad_chunk(vn), lw["ws"], lw["bst"], min(8, nb)).reshape(nb, CHUNK, B_W)[:, 0]

    qh = jnp.pad(qc.reshape(nb, C_HEADS, HEAD_DIM), ((0, 0), (0, HEAD_PAD - C_HEADS), (0, 0)))
    o_cmp, sel = _cmp_decode(layer, page_table, cmp_pool, cw, qh, npp)
    nsb_all = past // SLC_BLOCK + 1
    per_page = page // SLC_BLOCK
    nflag = (npages + 1) * per_page
    flags = jnp.pad(sel[:, :C_KV, :nsb_all] > 0.5, ((0, 0), (0, 0), (0, nflag - nsb_all)))
    need = flags[:, :, :npages * per_page].reshape(nb, C_KV, npages, per_page).any(axis=(1, 3))
    nslot = -(-min(npages, C_KV * SLC_TOPN) // npp) * npp
    pidx = jnp.arange(npages, dtype=jnp.int32)[None, :]
    order = jnp.argsort(jnp.where(need, pidx, pidx + npages), axis=1)[:, :nslot]
    cnt = jnp.sum(need, axis=1).astype(jnp.int32)
    last = jnp.take_along_axis(order, jnp.maximum(cnt - 1, 0)[:, None], axis=1)
    lpage = jnp.where(jnp.arange(nslot)[None, :] < cnt[:, None], order, last).astype(jnp.int32)
    plist = jnp.take_along_axis(page_table, lpage, axis=1)
    g3 = misc[:, GATE_OFF:GATE_OFF + 3 * C_HEADS].reshape(nb, 3, C_HEADS).transpose(0, 2, 1)
    gates = jnp.repeat(g3, HEAD_DIM, axis=1)
    nbias, fbias, wbias, b0 = sb
    new_t = _lane_zero(jnp.concatenate([slc_r, win_r], axis=1), page)
    c_out = _slc_decode(layer, plist.reshape(-1), lpage.reshape(-1), cnt, flags.reshape(-1).astype(jnp.int32), nflag,
                        _lane_repeat(qc, page), slc_t, nbias, fbias, win_t, wbias, b0, new_t, gates,
                        o_cmp[:, :C_HEADS].reshape(nb, C_W, 1), npp, nslot, npages - 1).reshape(nb, C_W)

    y = _mix_ffn(a_out, b_out, c_out, x2, lw, nb)
    kv5 = lambda r: r.reshape(nb, 1, 2, C_KV, HEAD_DIM)
    win_new = jnp.concatenate([win_state[layer][:, 1:], kv5(win_r)], axis=1)
    states = (kva.reshape(nb, 1, 2, A_HEADS, HEAD_DIM), lf[:, None, :], kv5(cmp_r), kv5(slc_r), win_new,
              vn.reshape(nb, 1, B_GROUPS, HEAD_DIM))
    return y.reshape(nb, 1, D_MODEL), states


def kernel(x_prompt, x_sample, cache_fox_kv, cache_fox_logf, cache_cmp_kv, cache_slc_kv, state_win_kv, page_table, rel_bias_table, norm_pre_mix, w_in, b_forget, gmlp_ln_g, gmlp_ln_b, gmlp_ws, gmlp_bs, cmp_pe, cmp_w1, cmp_b1, cmp_w2, cmp_b2, norm_group_a, norm_group_b, norm_group_c, w_o, norm_post_mix, norm_pre_ffn, w_ffn_in, w_ffn_out, norm_post_ffn):
    depth = w_in.shape[0]
    assert x_sample.shape[1] == 1
    B, S, _ = x_prompt.shape
    nb = x_sample.shape[0]
    n_pool, page = cache_fox_kv.shape[1], cache_fox_kv.shape[2]
    wb = state_win_kv.shape[2]

    w_in_p = _prep_w_in(w_in)
    bf = jnp.pad(b_forget, ((0, 0), (0, LANES - A_HEADS)))[:, None, :]
    cw_prompt, cw_sample = _prep_compress(cmp_pe, cmp_w1, cmp_b1, cmp_w2, cmp_b2)
    tb = _prompt_bias_tiles(rel_bias_table)
    sb = _sample_bias(rel_bias_table, page, wb)
    w_o_b = w_o.astype(BF16)
    w_fi_b = w_ffn_in.astype(BF16)
    w_fo_b = w_ffn_out.astype(BF16)
    bst = gmlp_bs.transpose(0, 2, 1)
    feat_pos = lambda a: jnp.moveaxis(a, 2, -1).reshape(a.shape[0], a.shape[1], -1, a.shape[2])
    chunks = page // CMP_STRIDE
    cmp_x = cache_cmp_kv.reshape(depth, n_pool, chunks, CMP_STRIDE, 2, C_KV, HEAD_DIM)
    cmp_x = cmp_x.transpose(0, 1, 4, 5, 2, 3, 6).reshape(depth, n_pool, 2 * C_KV, chunks, CMP_STRIDE * HEAD_DIM)
    logf_t = jnp.pad(cache_fox_logf.transpose(0, 1, 3, 2), ((0, 0), (0, 0), (0, HEAD_PAD - A_HEADS), (0, 0)))
    pools = (feat_pos(cache_fox_kv), logf_t, cmp_x, feat_pos(cache_slc_kv), feat_pos(state_win_kv))

    tm_proj = min(512, B * S)
    tm_mix = min(256, B * S)
    tq_fox = min(512, S)

    w_state = _prep_w_state(w_in)
    xp, xs = x_prompt, x_sample
    prompt_states = None
    logf_p = []
    ss = [[] for _ in range(6)]
    for l in range(depth):
        r1 = lambda a: a[l][None, :]
        lw = {
            "g_pre": r1(norm_pre_mix), "w_in": w_in_p[l], "w_state": w_state[l], "bf": bf[l], "lng": r1(gmlp_ln_g),
            "lnb": r1(gmlp_ln_b), "ws": gmlp_ws[l], "bst": bst[l], "ga": r1(norm_group_a), "gb": r1(norm_group_b),
            "gc": r1(norm_group_c), "w_o": w_o_b[l], "gpm": r1(norm_post_mix), "gpf": r1(norm_pre_ffn),
            "w_ffn_in": w_fi_b[l], "w_ffn_out": w_fo_b[l], "gpo": r1(norm_post_ffn),
        }
        xp, lf, prompt_states = _layer_prompt(l, depth, xp, lw, {k: v[l] for k, v in cw_prompt.items()}, tb,
                                              prompt_states, tm_proj, tm_mix, tq_fox)
        logf_p.append(lf)
        xs, st_s = _layer_sample(l, xs, lw, {k: v[l] for k, v in cw_sample.items()}, sb, pools, state_win_kv,
                                 page_table)
        for lst, st in zip(ss, st_s):
            lst.append(st)
    pos_major = lambda a, heads: jnp.moveaxis(a.reshape(depth, B, 2, heads, HEAD_DIM, a.shape[-1]), -1, 2)
    fox_t, cmp_t, slc_t, win_t = prompt_states
    wbp = min(WINDOW, S)
    sp = [pos_major(fox_t, A_HEADS), jnp.stack(logf_p), pos_major(cmp_t, C_KV), pos_major(slc_t, C_KV),
          pos_major(win_t[..., S - wbp:], C_KV)]
    return tuple([xp, xs] + sp + [jnp.stack(st) for st in ss])
```

```python
import functools
import math

import numpy as np
import jax
import jax.numpy as jnp
from jax import lax
from jax.experimental import pallas as pl
from jax.experimental.pallas import tpu as pltpu

F32 = jnp.float32
BF16 = jnp.bfloat16
HIGHEST = lax.Precision.HIGHEST

D_MODEL = 1024
HEAD_DIM = 64
A_HEADS = 6
B_GROUPS = 4
C_HEADS = 6
C_KV = 2
C_REP = C_HEADS // C_KV
A_W = A_HEADS * HEAD_DIM
B_W = B_GROUPS * HEAD_DIM
C_W = C_HEADS * HEAD_DIM
KV_W = C_KV * HEAD_DIM
Q_BLOCK = 128
CHUNK = 128
CMP_LEN = 32
CMP_STRIDE = 16
CMP_HIDDEN = 256
SLC_BLOCK = 64
SLC_TOPN = 16
WINDOW = 512
NUM_BUCKETS = 32
MAX_DISTANCE = 128
D_FF = 2816
EPS = 1e-6
NEG = -1e30
M_INIT = 0.5 * NEG
FORCE = 1e4
SCALE = HEAD_DIM ** -0.5
PROJ_WIDTHS = (A_W, A_W, A_W, A_HEADS, B_W, B_W, C_W, KV_W, KV_W, KV_W, KV_W, KV_W, KV_W, 3 * C_HEADS)

LANES = 128
SUBLANES = 8
HEAD_PAD = 8
VMEM_LIMIT = 56 * 1024 * 1024
PAGES_PER_STEP = 16

P_QA = (0, 384)
P_KVA = (384, 1152)
P_BU = (1152, 1408)
P_BV = (1408, 1664)
P_QC = (1664, 2048)
P_CMP = (2048, 2304)
P_SLC = (2304, 2560)
P_WIN = (2560, 2816)
P_MISC = (2816, 2944)
N_PROJ = 2944
GATE_OFF = A_HEADS


def _dot(a, b):
    return jnp.dot(a, b, preferred_element_type=F32)


def _dot_exact(a, b):
    return jnp.dot(a, b, preferred_element_type=F32, precision=HIGHEST)


def _dot_nt(a, b, precision=None):
    return lax.dot_general(a, b, (((1,), (1,)), ((), ())), preferred_element_type=F32, precision=precision)


def _gelu(x):
    return 0.5 * x * (1.0 + jnp.tanh(math.sqrt(2.0 / math.pi) * (x + 0.044715 * (x * x * x))))


def _sigmoid(x):
    return 1.0 / (1.0 + jnp.exp(-x))


def _rms(x, g):
    return x * lax.rsqrt(jnp.mean(x * x, axis=-1, keepdims=True) + EPS) * g


def _iota(shape, dim):
    return lax.broadcasted_iota(jnp.int32, shape, dim)


def _shr(x, n):
    return jnp.right_shift(x, int(math.log2(n)))


def _cparams(sem):
    return pltpu.CompilerParams(dimension_semantics=sem, vmem_limit_bytes=VMEM_LIMIT)


def _const_spec(shape):
    nd = len(shape)
    return pl.BlockSpec(shape, lambda *_: (0,) * nd)


def _proj_kernel(x_ref, g_ref, w_ref, bf_ref, lng_ref, lnb_ref,
                 qa_ref, kvab_ref, bu_ref, vn_ref, qc_ref, cmp_ref, kvcb_ref, misc_ref,
                 kva_ref=None, slc_ref=None, win_ref=None):
    h = _rms(x_ref[...], g_ref[...]).astype(BF16)

    def mm(seg):
        return _dot(h, w_ref[:, seg[0]:seg[1]])

    qa_ref[...] = (mm(P_QA) * SCALE).astype(BF16)
    kva = mm(P_KVA)
    if kva_ref is not None:
        kva_ref[...] = kva
    kvab_ref[...] = kva.astype(BF16)
    bu_ref[...] = _gelu(mm(P_BU))
    v = _gelu(mm(P_BV))
    mu = jnp.mean(v, axis=-1, keepdims=True)
    var = jnp.mean(jnp.square(v - mu), axis=-1, keepdims=True)
    vn_ref[...] = (v - mu) * lax.rsqrt(var + EPS) * lng_ref[...] + lnb_ref[...]
    qc_ref[...] = (mm(P_QC) * SCALE).astype(BF16)
    c = mm(P_CMP)
    cmp_ref[...] = c
    kvcb_ref[:, 0:256] = c.astype(BF16)
    c = mm(P_SLC)
    if slc_ref is not None:
        slc_ref[...] = c
    kvcb_ref[:, 256:512] = c.astype(BF16)
    c = mm(P_WIN)
    if win_ref is not None:
        win_ref[...] = c
    kvcb_ref[:, 512:768] = c.astype(BF16)
    m = mm(P_MISC) + bf_ref[...]
    lane = _iota(m.shape, 1)
    logsig = jnp.minimum(m, 0.0) - jnp.log1p(jnp.exp(-jnp.abs(m)))
    misc_ref[...] = jnp.where(lane < A_HEADS, logsig, _sigmoid(m))


def _proj(x2d, g, w, bf, lng, lnb, tm, state_rows):
    R = x2d.shape[0]
    row = lambda n: pl.BlockSpec((tm, n), lambda i: (i, 0))
    outs = [("qa", 384, BF16), ("kvab", 768, BF16), ("bu", 256, F32), ("vn", 256, F32), ("qc", 384, BF16),
            ("cmp", 256, F32), ("kvcb", 768, BF16), ("misc", 128, F32)]
    if state_rows:
        outs += [("kva", 768, F32), ("slc", 256, F32), ("win", 256, F32)]
    res = pl.pallas_call(
        _proj_kernel,
        grid=(R // tm,),
        in_specs=[row(D_MODEL), _const_spec((1, D_MODEL)), _const_spec((D_MODEL, N_PROJ)),
                  _const_spec((1, 128)), _const_spec((1, B_W)), _const_spec((1, B_W))],
        out_specs=[row(n) for _, n, _ in outs],
        out_shape=[jax.ShapeDtypeStruct((R, n), dt) for _, n, dt in outs],
        compiler_params=_cparams(("parallel",)),
        name="proj",
    )(x2d, g, w, bf, lng, lnb)
    return {name: r for (name, _, _), r in zip(outs, res)}


STATE_FEATS = (2 * A_W, 2 * KV_W, 2 * KV_W, 2 * KV_W)


def _state_kernel(x_ref, g_ref, wt_ref, *refs):
    outs = refs[-len(STATE_FEATS):]
    h = _rms(x_ref[...], g_ref[...]).astype(BF16)
    lo = 0
    for o_ref, n in zip(outs, STATE_FEATS):
        o_ref[...] = _dot_nt(wt_ref[lo:lo + n, :], h)
        lo += n


def _state_proj(layer, x2d, g, wt, prev, depth, B, S, tm):
    nblk = S // tm
    shapes = [jax.ShapeDtypeStruct((depth, B, n, S), F32) for n in STATE_FEATS]
    out_specs = [pl.BlockSpec((None, None, n, tm), lambda i: (layer, i // nblk, 0, i % nblk)) for n in STATE_FEATS]
    in_specs = [pl.BlockSpec((tm, D_MODEL), lambda i: (i, 0)), _const_spec((1, D_MODEL)), _const_spec(wt.shape)]
    args = [x2d, g, wt]
    aliases = {}
    if prev is not None:
        in_specs += [pl.BlockSpec(memory_space=pl.ANY)] * len(prev)
        aliases = {len(args) + k: k for k in range(len(prev))}
        args += list(prev)
    return pl.pallas_call(
        _state_kernel,
        grid=(B * nblk,),
        in_specs=in_specs,
        out_specs=out_specs,
        out_shape=shapes,
        input_output_aliases=aliases,
        compiler_params=_cparams(("arbitrary",)),
        name="state_proj",
    )(*args)


def _fox_kernel(q_ref, k_ref, v_ref, c_ref, o_ref, *, tq):
    qi = pl.program_id(2)
    tk = tq
    q = q_ref[...]
    lane = _iota((1, LANES), 1)
    low = lane < HEAD_DIM
    zero = jnp.zeros_like(q)
    qm = (jnp.where(low, q, zero), jnp.where(low, zero, q))
    one_lane = (HEAD_DIM, 0)

    def v_aug(v, hh):
        mine = low if hh == 0 else jnp.logical_not(low)
        ones = jnp.where(lane == one_lane[hh], 1.0, 0.0).astype(BF16)
        return jnp.where(mine, v, jnp.broadcast_to(ones, v.shape))

    def tile(ki, carry, causal):
        base = pl.multiple_of(ki * tk, tk)
        k = k_ref[pl.ds(base, tk), :]
        v = v_ref[pl.ds(base, tk), :]
        out = []
        for hh in range(2):
            m, acc = carry[hh]
            s = _dot_nt(qm[hh], k) - c_ref[hh:hh + 1, pl.ds(base, tk)]
            if causal is not None:
                s = jnp.where(causal, s, NEG)
            m_new = jnp.maximum(m, jnp.max(s, axis=-1, keepdims=True))
            alpha = jnp.exp(m - m_new)
            p = jnp.exp(s - m_new)
            acc = alpha * acc + _dot(p.astype(BF16), v_aug(v, hh))
            out.append((m_new, acc))
        return tuple(out)

    init = tuple((jnp.full((tq, 1), M_INIT, F32), jnp.zeros((tq, LANES), F32)) for _ in range(2))
    carry = lax.fori_loop(0, qi, lambda ki, c: tile(ki, c, None), init)
    causal = _iota((1, tk), 1) <= _iota((tq, 1), 0)
    (_, acc0), (_, acc1) = tile(qi, carry, causal)
    l0 = acc0[:, one_lane[0]:one_lane[0] + 1]
    l1 = acc1[:, one_lane[1]:one_lane[1] + 1]
    o_ref[...] = jnp.where(low, acc0 * (1.0 / l0), acc1 * (1.0 / l1))


def _fox_prompt(qa, kvab, crow, tq):
    B, S, _ = qa.shape
    half = A_HEADS // 2
    return pl.pallas_call(
        functools.partial(_fox_kernel, tq=tq),
        grid=(B, half, S // tq),
        in_specs=[
            pl.BlockSpec((None, tq, LANES), lambda b, p, qi: (b, qi, p)),
            pl.BlockSpec((None, S, LANES), lambda b, p, qi: (b, 0, p)),
            pl.BlockSpec((None, S, LANES), lambda b, p, qi: (b, 0, half + p)),
            pl.BlockSpec((None, None, 2, S), lambda b, p, qi: (b, p, 0, 0)),
        ],
        out_specs=pl.BlockSpec((None, tq, LANES), lambda b, p, qi: (b, qi, p)),
        out_shape=jax.ShapeDtypeStruct((B, S, A_W), F32),
        compiler_params=_cparams(("parallel", "parallel", "arbitrary")),
        name="fox_prompt",
    )(qa, kvab, kvab, crow)


def _cumsum_kernel(x_ref, o_ref):
    upper = jnp.where(_iota((LANES, LANES), 0) <= _iota((LANES, LANES), 1), 1.0, 0.0)
    carry = jnp.zeros((x_ref.shape[0], 1), F32)
    for i in range(x_ref.shape[1] // LANES):
        lanes = slice(i * LANES, (i + 1) * LANES)
        c = _dot_exact(x_ref[:, lanes], upper) + carry
        o_ref[:, lanes] = c
        carry = c[:, LANES - 1:LANES]


def _cumsum_lanes(x):
    n, h, length = x.shape
    spec = pl.BlockSpec((None, h, length), lambda i: (i, 0, 0))
    return pl.pallas_call(
        _cumsum_kernel, grid=(n,), in_specs=[spec], out_specs=spec,
        out_shape=jax.ShapeDtypeStruct(x.shape, F32),
        compiler_params=_cparams(("parallel",)),
        name="cumsum_logf",
    )(x)


def _gmlp_kernel(u_ref, vn_ref, ws_ref, bst_ref, o_ref, *, nchunk):
    tril = _iota((CHUNK, CHUNK), 1) <= _iota((CHUNK, CHUNK), 0)
    lane = _iota((1, B_W), 1)
    ws = [jnp.where(tril, ws_ref[h], 0.0).astype(BF16) for h in range(B_GROUPS)]
    for c in range(nchunk):
        rows = slice(c * CHUNK, (c + 1) * CHUNK)
        vn = vn_ref[rows, :].astype(BF16)
        z = jnp.zeros((CHUNK, B_W), F32)
        for h in range(B_GROUPS):
            zh = _dot(ws[h], vn) + bst_ref[:, h:h + 1]
            z = jnp.where(_shr(lane, HEAD_DIM) == h, zh, z)
        o_ref[rows, :] = u_ref[rows, :] * z


def _gmlp(u, vn, ws, bst, nchunk):
    R = u.shape[0]
    tm = nchunk * CHUNK
    row = pl.BlockSpec((tm, B_W), lambda i: (i, 0))
    return pl.pallas_call(
        functools.partial(_gmlp_kernel, nchunk=nchunk),
        grid=(R // tm,),
        in_specs=[row, row, _const_spec((B_GROUPS, CHUNK, CHUNK)), _const_spec((CHUNK, B_GROUPS))],
        out_specs=row,
        out_shape=jax.ShapeDtypeStruct((R, B_W), F32),
        compiler_params=_cparams(("parallel",)),
        name="gmlp",
    )(u, vn, ws, bst)


def _compress_core(x, w, pe_a, pe_b, b1, w2, b2):
    nc = x.shape[0]
    half = w.shape[1] // 2
    h = _dot(x, w)
    const = _dot(pe_a, w[:, :half]) + _dot(pe_b, w[:, half:])
    second = pltpu.roll(h[:, half:], nc - 1, 0)
    hid = _gelu(h[:, :half] + second + const[0:1, :] + b1)
    return _dot(hid.astype(BF16), w2) + b2


def _cmp_prompt_kernel(krows_ref, vrows_ref, wab_ref, pe_ref, b1_ref, w2_ref, b2_ref, o_ref, *, nc):
    for kv, rows_ref in enumerate((krows_ref, vrows_ref)):
        x = jnp.concatenate(
            [rows_ref[pl.ds(l, nc, stride=CMP_STRIDE), :] for l in range(CMP_STRIDE)], axis=-1).astype(BF16)
        o_ref[:, kv * KV_W:(kv + 1) * KV_W] = _compress_core(
            x, wab_ref[kv], pe_ref[kv, 0], pe_ref[kv, 1], b1_ref[kv], w2_ref[kv], b2_ref[kv]).astype(BF16)


def _compress_prompt(cmp_rows, cw):
    B, S, _ = cmp_rows.shape
    nc = S // CMP_STRIDE
    return pl.pallas_call(
        functools.partial(_cmp_prompt_kernel, nc=nc),
        grid=(B,),
        in_specs=[pl.BlockSpec((None, S, KV_W), lambda b: (b, 0, 0)), pl.BlockSpec((None, S, KV_W), lambda b: (b, 0, 1)),
                  _const_spec(cw["wab"].shape), _const_spec(cw["pe"].shape), _const_spec(cw["b1t"].shape),
                  _const_spec(cw["w2bd"].shape), _const_spec(cw["b2t"].shape)],
        out_specs=pl.BlockSpec((None, nc, 2 * KV_W), lambda b: (b, 0, 0)),
        out_shape=jax.ShapeDtypeStruct((B, nc, 2 * KV_W), BF16),
        compiler_params=_cparams(("parallel",)),
        name="compress_prompt",
    )(cmp_rows, cmp_rows, cw["wab"], cw["pe"], cw["b1t"], cw["w2bd"], cw["b2t"])


def _rank_select_t(score_t, topn):
    n = score_t.shape[0]
    groups = [score_t[lo:lo + SUBLANES] for lo in range(0, n, SUBLANES)]
    sub = _iota((SUBLANES, 1), 0)
    ranks = [jnp.zeros(g.shape, F32) for g in groups]
    for jp in range(n):
        row = score_t[jp:jp + 1, :]
        for gi, sg in enumerate(groups):
            lo = gi * SUBLANES
            if lo + SUBLANES - 1 <= jp:
                beats = row > sg
            elif lo > jp:
                beats = row >= sg
            else:
                beats = (row > sg) | ((row == sg) & (sub + lo > jp))
            ranks[gi] = ranks[gi] + jnp.where(beats, 1.0, 0.0)
    return jnp.concatenate([jnp.where(r < topn, 1.0, 0.0) for r in ranks], axis=0)


def _attend_tiles(qg, tiles, carry):
    m, acc = carry
    s_parts = []
    for k, _, bias, mask in tiles:
        s = _dot_nt(qg, k)
        if bias is not None:
            s = s + bias
        s_parts.append(jnp.where(mask, s, NEG))
    m_new = m
    for s in s_parts:
        m_new = jnp.maximum(m_new, jnp.max(s, axis=-1, keepdims=True))
    acc = jnp.exp(m - m_new) * acc
    for s, (_, v, _, _) in zip(s_parts, tiles):
        acc = acc + _dot(jnp.exp(s - m_new).astype(BF16), v)
    return m_new, acc


def _nsa_kernel(q_ref, kv_ref, cmp_ref, misc_ref, tb_ref, o_ref, *, nbp, nsb):
    qi = pl.program_id(1)
    tq = Q_BLOCK
    nrow = C_REP * tq
    t0 = qi * tq
    trow = t0 + _iota((tq, 1), 0)
    trow3 = jnp.concatenate([trow] * C_REP, axis=0)
    tl3 = trow3 - t0
    sl = _iota((1, tq), 1)
    causal3 = sl <= tl3
    topn = min(SLC_TOPN, nsb)
    lane = _iota((1, LANES), 1)
    low = lane < HEAD_DIM
    per = SLC_BLOCK // CMP_STRIDE
    pool_t = jnp.where(_shr(_iota((nsb, nbp), 1), per) == _iota((nsb, nbp), 0), 1.0, 0.0)
    groups = range(C_KV)
    mine = [low, jnp.logical_not(low)]
    one_lane = [HEAD_DIM, 0]
    ones = [jnp.where(lane == one_lane[g], 1.0, 0.0).astype(BF16) for g in groups]
    ks_lo, vs_lo, kw_lo, vw_lo = 2 * KV_W, 3 * KV_W, 4 * KV_W, 5 * KV_W

    def kv_tile(base, width, k_lo, v_lo):
        k = kv_ref[pl.ds(base, width), k_lo:k_lo + KV_W]
        v = kv_ref[pl.ds(base, width), v_lo:v_lo + KV_W]
        return k, [jnp.where(mine[g], v, jnp.broadcast_to(ones[g], v.shape)) for g in groups]

    qg, o_cmp, sel = [], [], []
    for g in groups:
        parts = []
        for r in range(C_REP):
            h = C_REP * g + r
            src = q_ref[:, (h // 2) * LANES:(h // 2 + 1) * LANES].astype(F32)
            if h % 2 != g:
                src = pltpu.roll(src, HEAD_DIM, 1)
            parts.append(jnp.where(mine[g], src, 0.0))
        q = jnp.concatenate(parts, axis=0).astype(BF16)
        qg.append(q)

        s = _dot_nt(q, cmp_ref[:, 0:KV_W])
        valid = (_iota((1, nbp), 1) * CMP_STRIDE + (CMP_LEN - 1)) <= trow3
        s = jnp.where(valid, s, NEG)
        e = jnp.exp(s - jnp.max(s, axis=-1, keepdims=True))
        prob = jnp.where(valid, e * (1.0 / jnp.sum(e, axis=-1, keepdims=True)), 0.0)
        o_cmp.append(_dot(prob.astype(BF16), cmp_ref[:, KV_W:2 * KV_W]))

        psum = prob[0:tq] + prob[tq:2 * tq] + prob[2 * tq:3 * tq]
        imp_t = _dot_nt(pool_t, psum, HIGHEST)
        j = _iota((nsb, 1), 0)
        tcol = t0 + _iota((1, tq), 1)
        cur = _shr(tcol, SLC_BLOCK)
        forced = (j == 0) | (j == cur) | (j == cur - 1)
        score_t = jnp.where(forced, FORCE, jnp.where(j * SLC_BLOCK <= tcol, imp_t, -1.0))
        sel.append(_rank_select_t(score_t, topn).T.astype(BF16))

    def sel_masks(base, width):
        blk = base // SLC_BLOCK + _shr(_iota((nsb, width), 1), SLC_BLOCK)
        expand = jnp.where(_iota((nsb, width), 0) == blk, 1.0, 0.0).astype(BF16)
        return [jnp.concatenate([_dot(sel[g], expand)] * C_REP, axis=0) > 0.5 for g in groups]

    init = (jnp.full((nrow, 1), M_INIT, F32), jnp.zeros((nrow, LANES), F32))

    def far_step(base, width, carry):
        k, v = kv_tile(base, width, ks_lo, vs_lo)
        masks = sel_masks(base, width)
        return tuple(_attend_tiles(qg[g], [(k, v[g], None, masks[g])], carry[g]) for g in groups)

    n_far = jnp.maximum(qi - 1, 0)
    n_big = n_far // 4
    carry = lax.fori_loop(0, n_big, lambda i, c: far_step(pl.multiple_of(i * (4 * tq), 4 * tq), 4 * tq, c),
                          (init, init))
    carry = lax.fori_loop(0, n_far - 4 * n_big,
                          lambda i, c: far_step(pl.multiple_of((n_big * 4 + i) * tq, tq), tq, c), carry)

    pbase = pl.multiple_of(jnp.maximum(qi - 1, 0) * tq, tq)
    dbase = pl.multiple_of(t0, tq)
    has_prev = qi >= 1
    far_w = WINDOW - tq
    fbase = pl.multiple_of(jnp.maximum(t0 - WINDOW, 0), tq)
    kpos = fbase + _iota((1, far_w), 1)
    far_mask = (kpos < t0 - tq) & (trow3 - kpos <= WINDOW)
    mask_p = sel_masks(pbase, tq)
    mask_d = sel_masks(dbase, tq)
    ksp, vsp = kv_tile(pbase, tq, ks_lo, vs_lo)
    ksd, vsd = kv_tile(dbase, tq, ks_lo, vs_lo)
    kwf, vwf = kv_tile(fbase, far_w, kw_lo, vw_lo)
    kwp, vwp = kv_tile(pbase, tq, kw_lo, vw_lo)
    kwd, vwd = kv_tile(dbase, tq, kw_lo, vw_lo)
    heads = [None] * C_HEADS
    for g in groups:
        bias_diag = tb_ref[g, 0]
        bias_prev = tb_ref[g, 1]
        _, acc = _attend_tiles(qg[g], [(ksp, vsp[g], bias_prev, mask_p[g] & has_prev),
                                       (ksd, vsd[g], bias_diag, mask_d[g] & causal3)], carry[g])
        o_slc = acc * (1.0 / acc[:, one_lane[g]:one_lane[g] + 1])
        cw = _attend_tiles(qg[g], [(kwf, vwf[g], None, far_mask)], init)
        _, acc = _attend_tiles(qg[g], [(kwp, vwp[g], bias_prev, (tl3 >= 0) & has_prev),
                                       (kwd, vwd[g], bias_diag, causal3)], cw)
        o_win = acc * (1.0 / acc[:, one_lane[g]:one_lane[g] + 1])
        for r in range(C_REP):
            h = C_REP * g + r
            rows = slice(r * tq, (r + 1) * tq)
            gate = lambda br: misc_ref[:, GATE_OFF + br * C_HEADS + h:GATE_OFF + br * C_HEADS + h + 1]
            heads[h] = gate(0) * o_cmp[g][rows] + gate(1) * o_slc[rows] + gate(2) * o_win[rows]

    for pair in range(C_HEADS // 2):
        halves = []
        for h in (2 * pair, 2 * pair + 1):
            x = heads[h]
            if h // C_REP != h % 2:
                x = pltpu.roll(x, HEAD_DIM, 1)
            halves.append(x)
        o_ref[:, pair * LANES:(pair + 1) * LANES] = jnp.where(low, halves[0], halves[1])


def _nsa_prompt(qc, kvcb, kcv, misc, tb):
    B, S, _ = qc.shape
    nbp = kcv.shape[1]
    nsb = nbp * CMP_STRIDE // SLC_BLOCK
    return pl.pallas_call(
        functools.partial(_nsa_kernel, nbp=nbp, nsb=nsb),
        grid=(B, S // Q_BLOCK),
        in_specs=[pl.BlockSpec((None, Q_BLOCK, C_W), lambda b, i: (b, i, 0)),
                  pl.BlockSpec((None, S, 6 * KV_W), lambda b, i: (b, 0, 0)),
                  pl.BlockSpec((None, nbp, 2 * KV_W), lambda b, i: (b, 0, 0)),
                  pl.BlockSpec((None, Q_BLOCK, 128), lambda b, i: (b, i, 0)),
                  _const_spec(tb.shape)],
        out_specs=pl.BlockSpec((None, Q_BLOCK, C_W), lambda b, i: (b, i, 0)),
        out_shape=jax.ShapeDtypeStruct((B, S, C_W), F32),
        compiler_params=_cparams(("parallel", "arbitrary")),
        name="nsa_prompt",
    )(qc, kvcb, kcv, misc, tb)


def _mix_ffn_kernel(a_ref, b_ref, c_ref, x_ref, ga_ref, gb_ref, gc_ref, wo_ref, gpm_ref, gpf_ref,
                    win_ref, wout_ref, gpo_ref, o_ref, *, nsplit):
    an = _rms(a_ref[...], ga_ref[...]).astype(BF16)
    bn = _rms(b_ref[...], gb_ref[...]).astype(BF16)
    cn = _rms(c_ref[...], gc_ref[...]).astype(BF16)
    mix = (_dot(an, wo_ref[0:A_W, :]) + _dot(bn, wo_ref[A_W:A_W + B_W, :])
           + _dot(cn, wo_ref[A_W + B_W:A_W + B_W + C_W, :]))
    x1 = x_ref[...] + _rms(mix, gpm_ref[...])
    h = _rms(x1, gpf_ref[...]).astype(BF16)
    wid = D_FF // nsplit
    y = jnp.zeros(x1.shape, F32)
    for c in range(nsplit):
        lo = c * wid
        gate = _dot(h, win_ref[:, lo:lo + wid])
        up = _dot(h, win_ref[:, D_FF + lo:D_FF + lo + wid])
        act = gate * _sigmoid(gate) * up
        y = y + _dot(act.astype(BF16), wout_ref[lo:lo + wid, :])
    o_ref[...] = x1 + _rms(y, gpo_ref[...])


def _mix_ffn(a, b, c, x, lw, tm):
    R = x.shape[0]
    row = lambda n: pl.BlockSpec((tm, n), lambda i: (i, 0))
    once = lambda shape: pl.BlockSpec(shape, lambda i: (0, 0), pipeline_mode=pl.Buffered(1))
    return pl.pallas_call(
        functools.partial(_mix_ffn_kernel, nsplit=2),
        grid=(R // tm,),
        in_specs=[row(A_W), row(B_W), row(C_W), row(D_MODEL),
                  once((1, A_W)), once((1, B_W)), once((1, C_W)), once((D_MODEL, D_MODEL)),
                  once((1, D_MODEL)), once((1, D_MODEL)), once((D_MODEL, 2 * D_FF)), once((D_FF, D_MODEL)),
                  once((1, D_MODEL))],
        out_specs=row(D_MODEL),
        out_shape=jax.ShapeDtypeStruct((R, D_MODEL), F32),
        compiler_params=_cparams(("parallel",)),
        name="mix_ffn",
    )(a, b, c, x, lw["ga"], lw["gb"], lw["gc"], lw["w_o"], lw["gpm"], lw["gpf"], lw["w_ffn_in"],
      lw["w_ffn_out"], lw["gpo"])


def _head_block(x, h):
    return x[h * HEAD_DIM:(h + 1) * HEAD_DIM]


def _pad_heads(x):
    return jnp.concatenate([x, jnp.zeros((HEAD_PAD - x.shape[0], x.shape[1]), F32)], axis=0)


def _row_softmax_step(s, m_prev):
    m_new = jnp.maximum(m_prev, jnp.max(s, axis=1, keepdims=True))
    return m_new, jnp.exp(m_prev - m_new), jnp.exp(s - m_new)


def _fox_dec_kernel(pt_ref, q_ref, *refs, npp):
    pages = refs[:npp]
    lfs = refs[npp:2 * npp]
    new_ref, newlf_ref, o_ref, m_sc, l_sc, c_sc, acc_sc = refs[2 * npp:]
    j = pl.program_id(1)
    page = pages[0].shape[1]
    q = q_ref[...]
    upper = jnp.where(_iota((page, page), 0) <= _iota((page, page), 1), 1.0, 0.0)

    @pl.when(j == 0)
    def _():
        m_sc[...] = jnp.full(m_sc.shape, M_INIT, F32)
        l_sc[...] = jnp.zeros(l_sc.shape, F32)
        c_sc[...] = jnp.zeros(c_sc.shape, F32)
        acc_sc[...] = jnp.zeros(acc_sc.shape, F32)

    def scores(kt):
        prod = kt * q
        return _pad_heads(jnp.concatenate(
            [jnp.sum(_head_block(prod, h), axis=0, keepdims=True) for h in range(A_HEADS)], axis=0))

    def accumulate(alpha, p_tiles, v_tiles):
        for h in range(A_HEADS):
            a = _head_block(acc_sc, h) * alpha[h:h + 1, :]
            for p, v in zip(p_tiles, v_tiles):
                a = a + p[h:h + 1, :] * v[A_W + h * HEAD_DIM:A_W + (h + 1) * HEAD_DIM, :]
            acc_sc[h * HEAD_DIM:(h + 1) * HEAD_DIM, :] = a

    c_local = _dot_exact(jnp.concatenate([lf[...] for lf in lfs], axis=0), upper)
    totals = [jnp.broadcast_to(c_local[i * HEAD_PAD:(i + 1) * HEAD_PAD, page - 1:page], (HEAD_PAD, page))
              for i in range(npp)]
    before = jnp.broadcast_to(c_sc[...], (HEAD_PAD, page))
    s_parts = []
    for i, pg in enumerate(pages):
        s_parts.append(scores(pg[0:A_W, :]) - (c_local[i * HEAD_PAD:(i + 1) * HEAD_PAD] + before))
        before = before + totals[i]
    c_sc[...] = before[:, 0:1]
    m_new, alpha, p = _row_softmax_step(jnp.concatenate(s_parts, axis=1), m_sc[...])
    l_sc[...] = alpha * l_sc[...] + jnp.sum(p, axis=1, keepdims=True)
    m_sc[...] = m_new
    accumulate(alpha, [p[:, i * page:(i + 1) * page] for i in range(npp)], pages)

    @pl.when(j == pl.num_programs(1) - 1)
    def _():
        first = _iota((1, page), 1) == 0
        s_new = jnp.where(first, scores(new_ref[0:A_W, :]) - (c_sc[...] + newlf_ref[...]), NEG)
        m_new, alpha, p_new = _row_softmax_step(s_new, m_sc[...])
        l_inv = 1.0 / (alpha * l_sc[...] + jnp.sum(p_new, axis=1, keepdims=True))
        accumulate(alpha, [p_new], [new_ref])
        for h in range(A_HEADS):
            rows = slice(h * HEAD_DIM, (h + 1) * HEAD_DIM)
            o_ref[rows, :] = jnp.sum(acc_sc[rows, :], axis=1, keepdims=True) * l_inv[h:h + 1, :]


def _fox_decode(layer, page_table, q_rep, pool_t, logf_t, new_t, new_lf, npp):
    nb, npages = page_table.shape
    page = pool_t.shape[3]
    page_of = lambda b, j, i, pt: pt[b, j * npp + i]
    page_spec = lambda i: pl.BlockSpec((None, None, 2 * A_W, page), lambda b, j, pt: (layer, page_of(b, j, i, pt), 0, 0))
    logf_spec = lambda i: pl.BlockSpec((None, None, HEAD_PAD, page), lambda b, j, pt: (layer, page_of(b, j, i, pt), 0, 0))
    per_b = lambda shape: pl.BlockSpec((None,) + shape, lambda b, j, pt: (b,) + (0,) * len(shape))
    grid_spec = pltpu.PrefetchScalarGridSpec(
        num_scalar_prefetch=1,
        grid=(nb, npages // npp),
        in_specs=[per_b((A_W, page))] + [page_spec(i) for i in range(npp)] + [logf_spec(i) for i in range(npp)]
        + [per_b((2 * A_W, page)), per_b((HEAD_PAD, 1))],
        out_specs=per_b((A_W, 1)),
        scratch_shapes=[pltpu.VMEM((HEAD_PAD, 1), F32), pltpu.VMEM((HEAD_PAD, 1), F32), pltpu.VMEM((HEAD_PAD, 1), F32),
                        pltpu.VMEM((A_W, page), F32)],
    )
    return pl.pallas_call(
        functools.partial(_fox_dec_kernel, npp=npp),
        grid_spec=grid_spec,
        out_shape=jax.ShapeDtypeStruct((nb, A_W, 1), F32),
        compiler_params=_cparams(("parallel", "arbitrary")),
        name="fox_decode",
    )(page_table, q_rep, *([pool_t] * npp), *([logf_t] * npp), new_t, new_lf)


def _cmp_dec_kernel(pt_ref, *refs, npp, nc, nsb_all):
    pages = refs[:npp]
    w_ref, pe_ref, b1_ref, w2_ref, b2_ref, q_ref, ocmp_ref, sel_ref, x_sc = refs[npp:]
    j = pl.program_id(1)
    per_page = pages[0].shape[1]
    for i, pg in enumerate(pages):
        row0 = pl.multiple_of((j * npp + i) * per_page, per_page)
        for c in range(2 * C_KV):
            x_sc[c, pl.ds(row0, per_page), :] = pg[c]

    @pl.when(j == pl.num_programs(1) - 1)
    def _():
        kc = [[_compress_core(x_sc[kv * C_KV + g].astype(BF16), w_ref[kv], pe_ref[kv, 0], pe_ref[kv, 1],
                              b1_ref[kv], w2_ref[kv], b2_ref[kv]).astype(BF16)
               for g in range(C_KV)] for kv in range(2)]
        q = q_ref[...]
        grp0 = _iota((HEAD_PAD, 1), 0) < C_REP
        pick = lambda x0, x1: jnp.where(grp0, x0, x1)
        s = pick(_dot_nt(q, kc[0][0]), _dot_nt(q, kc[0][1]))
        valid = _iota((1, nc), 1) < nc - 1
        s = jnp.where(valid, s, NEG)
        e = jnp.exp(s - jnp.max(s, axis=-1, keepdims=True))
        prob = jnp.where(valid, e / jnp.sum(e, axis=-1, keepdims=True), 0.0)
        pb = prob.astype(BF16)
        ocmp_ref[...] = pick(_dot(pb, kc[1][0]), _dot(pb, kc[1][1]))

        per = SLC_BLOCK // CMP_STRIDE
        nsbp = sel_ref.shape[1]
        hrow = _iota((HEAD_PAD, 1), 0)
        psum = jnp.concatenate(
            [jnp.sum(jnp.where((hrow >= C_REP * g) & (hrow < C_REP * (g + 1)), prob, 0.0), axis=0, keepdims=True)
             for g in range(C_KV)] + [jnp.zeros((HEAD_PAD - C_KV, nc), F32)], axis=0)
        pool = jnp.where(_shr(_iota((nc, nsbp), 0), per) == _iota((nc, nsbp), 1), 1.0, 0.0)
        imp = _dot_exact(psum, pool)
        jl = _iota((1, nsbp), 1)
        cur = nsb_all - 1
        forced = (jl == 0) | (jl == cur) | (jl == cur - 1)
        score = jnp.where(forced, FORCE, jnp.where(jl <= cur, imp, -2.0))
        topn = min(SLC_TOPN, nsb_all)
        rank = jnp.zeros(score.shape, F32)
        for jp in range(nsb_all):
            col = score[:, jp:jp + 1]
            beats = (col > score) | ((col == score) & (jp < jl))
            rank = rank + jnp.where(beats, 1.0, 0.0)
        sel_ref[...] = jnp.where(rank < topn, 1.0, 0.0)


def _cmp_decode(layer, page_table, pool, cw, qh, npp):
    nb, npages = page_table.shape
    per_page, width = pool.shape[3], pool.shape[4]
    nc = npages * per_page
    nsb_all = nc * CMP_STRIDE // SLC_BLOCK + 1
    nsbp = -(-nsb_all // LANES) * LANES

    def page_spec(i):
        return pl.BlockSpec((None, None, 2 * C_KV, per_page, width),
                            lambda b, j, pt: (layer, pt[b, j * npp + i], 0, 0, 0))

    const = lambda a: pl.BlockSpec(a.shape, lambda b, j, pt: (0,) * a.ndim)
    grid_spec = pltpu.PrefetchScalarGridSpec(
        num_scalar_prefetch=1,
        grid=(nb, npages // npp),
        in_specs=[page_spec(i) for i in range(npp)]
        + [const(cw["w"]), const(cw["pe"]), const(cw["b1"]), const(cw["w2"]), const(cw["b2"]),
           pl.BlockSpec((None, HEAD_PAD, HEAD_DIM), lambda b, j, pt: (b, 0, 0))],
        out_specs=[pl.BlockSpec((None, HEAD_PAD, HEAD_DIM), lambda b, j, pt: (b, 0, 0)),
                   pl.BlockSpec((None, HEAD_PAD, nsbp), lambda b, j, pt: (b, 0, 0))],
        scratch_shapes=[pltpu.VMEM((2 * C_KV, nc, width), F32)],
    )
    return pl.pallas_call(
        functools.partial(_cmp_dec_kernel, npp=npp, nc=nc, nsb_all=nsb_all),
        grid_spec=grid_spec,
        out_shape=[jax.ShapeDtypeStruct((nb, HEAD_PAD, HEAD_DIM), F32),
                   jax.ShapeDtypeStruct((nb, HEAD_PAD, nsbp), F32)],
        compiler_params=_cparams(("parallel", "arbitrary")),
        name="cmp_decode",
    )(page_table, *([pool] * npp), cw["w"], cw["pe"], cw["b1"], cw["w2"], cw["b2"], qh)


def _slc_dec_kernel(plist_ref, lpage_ref, cnt_ref, flag_ref, q_ref, *refs, npp, nslot, nflag, last_page):
    pages = refs[:npp]
    (nbias_ref, fbias_ref, win_ref, wbias_ref, b0_ref, new_ref, gate_ref, ocmp_ref, o_ref,
     m_sc, l_sc, acc_sc) = refs[npp:]
    b = pl.program_id(0)
    j = pl.program_id(1)
    page = pages[0].shape[1]
    q = q_ref[...]
    grp0 = _iota((HEAD_PAD, 1), 0) < C_REP
    lane = _iota((1, page), 1)

    def scores(kt):
        return _pad_heads(jnp.concatenate(
            [jnp.sum(_head_block(kt, h // C_REP) * _head_block(q, h), axis=0, keepdims=True)
             for h in range(C_HEADS)], axis=0))

    def weighted(acc, alpha, p_tiles, v_tiles):
        out = []
        for h in range(C_HEADS):
            a = _head_block(acc, h) * alpha[h:h + 1, :]
            for p, v in zip(p_tiles, v_tiles):
                a = a + p[h:h + 1, :] * _head_block(v, h // C_REP)
            out.append(a)
        return jnp.concatenate(out, axis=0)

    def finish(acc, l):
        l_inv = 1.0 / l
        return jnp.concatenate(
            [jnp.sum(_head_block(acc, h), axis=1, keepdims=True) * l_inv[h:h + 1, :] for h in range(C_HEADS)], axis=0)

    @pl.when(j == 0)
    def _():
        m_sc[...] = jnp.full(m_sc.shape, M_INIT, F32)
        l_sc[...] = jnp.zeros(l_sc.shape, F32)
        acc_sc[...] = jnp.zeros(acc_sc.shape, F32)

    nblk = page // SLC_BLOCK
    s_parts = []
    for i, pg in enumerate(pages):
        slot = j * npp + i
        lp = lpage_ref[b * nslot + slot]
        live = slot < cnt_ref[b]
        mask = jnp.zeros((HEAD_PAD, page), jnp.int32)
        for blk in range(nblk):
            f = [flag_ref[(b * C_KV + g) * nflag + lp * nblk + blk] for g in range(C_KV)]
            mask = jnp.where(_shr(lane, SLC_BLOCK) == blk, jnp.where(grp0, f[0], f[1]), mask)
        mask = (mask > 0) & live
        bias = jnp.where(lp == last_page, nbias_ref[...], fbias_ref[...])
        s_parts.append(jnp.where(mask, scores(pg[0:KV_W, :]) + bias, NEG))
    m_new, alpha, p = _row_softmax_step(jnp.concatenate(s_parts, axis=1), m_sc[...])
    l_sc[...] = alpha * l_sc[...] + jnp.sum(p, axis=1, keepdims=True)
    m_sc[...] = m_new
    acc_sc[...] = weighted(acc_sc[...], alpha, [p[:, i * page:(i + 1) * page] for i in range(npp)],
                           [pg[KV_W:2 * KV_W, :] for pg in pages])

    @pl.when(j == pl.num_programs(1) - 1)
    def _():
        first = lane == 0
        s_new = jnp.where(first, scores(new_ref[0:KV_W, :]) + b0_ref[...], NEG)
        m_new, alpha, p_new = _row_softmax_step(s_new, m_sc[...])
        l = alpha * l_sc[...] + jnp.sum(p_new, axis=1, keepdims=True)
        o_slc = finish(weighted(acc_sc[...], alpha, [p_new], [new_ref[KV_W:2 * KV_W, :]]), l)
        nwin = win_ref.shape[1] // page
        tiles = lambda ref, lo: [ref[lo:lo + KV_W, c * page:(c + 1) * page] for c in range(nwin)]
        s_win = [scores(kt) + wbias_ref[:, c * page:(c + 1) * page] for c, kt in enumerate(tiles(win_ref, 0))]
        s_win.append(jnp.where(first, scores(new_ref[2 * KV_W:3 * KV_W, :]) + b0_ref[...], NEG))
        s = jnp.concatenate(s_win, axis=1)
        p = jnp.exp(s - jnp.max(s, axis=1, keepdims=True))
        acc = weighted(jnp.zeros((C_W, page), F32), jnp.zeros((HEAD_PAD, 1), F32),
                       [p[:, c * page:(c + 1) * page] for c in range(nwin + 1)],
                       tiles(win_ref, KV_W) + [new_ref[3 * KV_W:4 * KV_W, :]])
        o_win = finish(acc, jnp.sum(p, axis=1, keepdims=True))
        o_ref[...] = gate_ref[:, 0:1] * ocmp_ref[...] + gate_ref[:, 1:2] * o_slc + gate_ref[:, 2:3] * o_win


def _slc_decode(layer, plist, lpage, cnt, flags, nflag, q_rep, pool_t, nbias, fbias, win_t, wbias, b0, new_t, gates,
                ocmp, npp, nslot, last_page):
    nb = q_rep.shape[0]
    page = pool_t.shape[3]
    wb = win_t.shape[3]

    def page_spec(i):
        return pl.BlockSpec((None, None, 2 * KV_W, page),
                            lambda b, j, pls, lps, cn, fl: (layer, pls[b * nslot + j * npp + i], 0, 0))

    fixed = lambda shape: pl.BlockSpec(shape, lambda b, j, pls, lps, cn, fl: (0,) * len(shape))
    per_b = lambda shape: pl.BlockSpec((None,) + shape, lambda b, j, pls, lps, cn, fl: (b,) + (0,) * len(shape))
    grid_spec = pltpu.PrefetchScalarGridSpec(
        num_scalar_prefetch=4,
        grid=(nb, nslot // npp),
        in_specs=[per_b((C_W, page))] + [page_spec(i) for i in range(npp)]
        + [fixed((HEAD_PAD, page)), fixed((HEAD_PAD, 1)),
           pl.BlockSpec((None, None, 2 * KV_W, wb), lambda b, j, pls, lps, cn, fl: (layer, b, 0, 0)),
           fixed((HEAD_PAD, wb)), fixed((HEAD_PAD, 1)), per_b((4 * KV_W, page)), per_b((C_W, 3)), per_b((C_W, 1))],
        out_specs=per_b((C_W, 1)),
        scratch_shapes=[pltpu.VMEM((HEAD_PAD, 1), F32), pltpu.VMEM((HEAD_PAD, 1), F32), pltpu.VMEM((C_W, page), F32)],
    )
    return pl.pallas_call(
        functools.partial(_slc_dec_kernel, npp=npp, nslot=nslot, nflag=nflag, last_page=last_page),
        grid_spec=grid_spec,
        out_shape=jax.ShapeDtypeStruct((nb, C_W, 1), F32),
        compiler_params=_cparams(("parallel", "arbitrary")),
        name="slc_win_decode",
    )(plist, lpage, cnt, flags, q_rep, *([pool_t] * npp), nbias, fbias, win_t, wbias, b0, new_t, gates, ocmp)


def _t5_bucket(dist):
    n = jnp.maximum(dist, 0)
    max_exact = NUM_BUCKETS // 2
    nf = jnp.maximum(n, 1).astype(F32)
    large = max_exact + (jnp.log(nf / max_exact) / math.log(MAX_DISTANCE / max_exact)
                         * (NUM_BUCKETS - max_exact)).astype(jnp.int32)
    return jnp.where(n < max_exact, n, jnp.minimum(large, NUM_BUCKETS - 1))


def _prep_w_in(w_in):
    splits = [int(s) for s in np.cumsum(PROJ_WIDTHS)[:-1]]
    a_q, a_k, a_v, a_f, b_u, b_v, c_q, c_kc, c_vc, c_ks, c_vs, c_kw, c_vw, c_g = jnp.split(w_in, splits, axis=-1)
    pad = jnp.zeros(w_in.shape[:-1] + (LANES - A_HEADS - 3 * C_HEADS,), w_in.dtype)
    return jnp.concatenate([a_q, a_k, a_v, b_u, b_v, c_q, c_kc, c_vc, c_ks, c_vs, c_kw, c_vw, a_f, c_g, pad],
                           axis=-1).astype(BF16)


def _prep_w_state(w_in):
    splits = [int(s) for s in np.cumsum(PROJ_WIDTHS)[:-1]]
    _, a_k, a_v, _, _, _, _, c_kc, c_vc, c_ks, c_vs, c_kw, c_vw, _ = jnp.split(w_in, splits, axis=-1)
    return jnp.concatenate([a_k, a_v, c_kc, c_vc, c_ks, c_vs, c_kw, c_vw], axis=-1).transpose(0, 2, 1).astype(BF16)


def _prep_compress(cmp_pe, cmp_w1, cmp_b1, cmp_w2, cmp_b2):
    depth = cmp_w1.shape[0]
    half = CMP_LEN // 2
    eye = jnp.eye(C_KV, dtype=F32)
    w1 = cmp_w1.reshape(depth, 2, 2, half, HEAD_DIM, CMP_HIDDEN)
    wab = jnp.einsum("zkhldj,gG->zklgdhGj", w1, eye).reshape(depth, 2, half * KV_W, 2 * C_KV * CMP_HIDDEN)
    pe = cmp_pe.reshape(depth, 2, 2, half, 1, HEAD_DIM)
    pe_bd = jnp.broadcast_to(pe, (depth, 2, 2, half, C_KV, HEAD_DIM)).reshape(depth, 2, 2, 1, half * KV_W)
    pe_bd = jnp.broadcast_to(pe_bd, (depth, 2, 2, SUBLANES, half * KV_W))
    w2bd = jnp.einsum("zkjd,gG->zkgjGd", cmp_w2, eye).reshape(depth, 2, C_KV * CMP_HIDDEN, KV_W)
    w_g = w1.transpose(0, 1, 3, 4, 2, 5).reshape(depth, 2, half * HEAD_DIM, 2 * CMP_HIDDEN)
    pe_g = jnp.broadcast_to(cmp_pe.reshape(depth, 2, 2, 1, half * HEAD_DIM), (depth, 2, 2, SUBLANES, half * HEAD_DIM))
    prompt = {
        "wab": wab.astype(BF16), "pe": pe_bd.astype(BF16),
        "b1t": jnp.tile(cmp_b1, (1, 1, C_KV))[:, :, None, :],
        "w2bd": w2bd.astype(BF16),
        "b2t": jnp.tile(cmp_b2, (1, 1, C_KV))[:, :, None, :],
    }
    sample = {"w": w_g.astype(BF16), "pe": pe_g.astype(BF16), "b1": cmp_b1[:, :, None, :],
              "w2": cmp_w2.astype(BF16), "b2": cmp_b2[:, :, None, :]}
    return prompt, sample


def _prompt_bias_tiles(table):
    tl = jnp.arange(Q_BLOCK)[:, None]
    sl = jnp.arange(Q_BLOCK)[None, :]
    d = tl - sl
    far = table[_t5_bucket(jnp.asarray(8 * MAX_DISTANCE))]
    tiles = []
    for delta in (0, Q_BLOCK):
        b = table[_t5_bucket(d + delta)] - far
        tiles.append(b.transpose(2, 0, 1).reshape(C_KV, C_REP * Q_BLOCK, Q_BLOCK))
    return jnp.stack(tiles, axis=1).astype(F32)


def _sample_bias(table, page, wb):
    rows = lambda x: jnp.pad(x.T, ((0, HEAD_PAD - C_HEADS), (0, 0))).astype(F32)
    nbias = rows(table[_t5_bucket(page - jnp.arange(page))])
    fbias = rows(table[_t5_bucket(jnp.asarray([8 * MAX_DISTANCE]))])
    wbias = rows(table[_t5_bucket(wb - jnp.arange(wb))])
    b0 = rows(table[0:1])
    return nbias, fbias, wbias, b0


def _lane_repeat(x, n):
    return jnp.broadcast_to(x.astype(F32)[:, :, None], x.shape + (n,))


def _lane_zero(x, n):
    return jnp.pad(x.astype(F32)[:, :, None], ((0, 0), (0, 0), (0, n - 1)))


def _layer_prompt(layer, depth, x, lw, cw, tb, states, tm_proj, tm_mix, tq_fox):
    B, S, _ = x.shape
    R = B * S
    x2 = x.reshape(R, D_MODEL)
    p = _proj(x2, lw["g_pre"], lw["w_in"], lw["bf"], lw["lng"], lw["lnb"], tm_proj, state_rows=False)
    states = _state_proj(layer, x2, lw["g_pre"], lw["w_state"], states, depth, B, S, tm_proj)
    lf = p["misc"][:, :A_HEADS].reshape(B, S, A_HEADS)
    lf_t = jnp.pad(lf, ((0, 0), (0, 0), (0, HEAD_PAD - A_HEADS))).transpose(0, 2, 1)
    crow = _cumsum_lanes(lf_t)[:, :A_HEADS].reshape(B, A_HEADS // 2, 2, S)
    a_out = _fox_prompt(p["qa"].reshape(B, S, A_W), p["kvab"].reshape(B, S, 2 * A_W), crow, tq_fox)
    b_out = _gmlp(p["bu"], p["vn"], lw["ws"], lw["bst"], min(8, R // CHUNK))
    kcv = _compress_prompt(p["cmp"].reshape(B, S, 2 * KV_W), cw)
    c_out = _nsa_prompt(p["qc"].reshape(B, S, C_W), p["kvcb"].reshape(B, S, 6 * KV_W), kcv,
                        p["misc"].reshape(B, S, 128), tb)
    y = _mix_ffn(a_out.reshape(R, A_W), b_out, c_out.reshape(R, C_W), x2, lw, tm_mix)
    return y.reshape(B, S, D_MODEL), lf, states


def _layer_sample(layer, x, lw, cw, sb, pools, page_table):
    nb = x.shape[0]
    fox_t, logf_t, cmp_pool, slc_t, win_t = pools
    npages = page_table.shape[1]
    page = fox_t.shape[3]
    past = npages * page
    npp = min(PAGES_PER_STEP, npages)
    x2 = x.reshape(nb, D_MODEL)
    p = _proj(x2, lw["g_pre"], lw["w_in"], lw["bf"], lw["lng"], lw["lnb"], nb, state_rows=True)
    qa, kva, bu, vn, qc, misc = p["qa"], p["kva"], p["bu"], p["vn"], p["qc"], p["misc"]
    cmp_r, slc_r, win_r = p["cmp"], p["slc"], p["win"]
    lf = misc[:, :A_HEADS]

    new_lf = jnp.pad(lf, ((0, 0), (0, HEAD_PAD - A_HEADS)))[:, :, None]
    a_out = _fox_decode(layer, page_table, _lane_repeat(qa, page), fox_t, logf_t, _lane_zero(kva, page), new_lf,
                        npp).reshape(nb, A_W)

    pad_chunk = lambda t: jnp.zeros((nb, CHUNK, B_W), F32).at[:, 0].set(t).reshape(nb * CHUNK, B_W)
    b_out = _gmlp(pad_chunk(bu), pad_chunk(vn), lw["ws"], lw["bst"], min(8, nb)).reshape(nb, CHUNK, B_W)[:, 0]

    qh = jnp.pad(qc.reshape(nb, C_HEADS, HEAD_DIM), ((0, 0), (0, HEAD_PAD - C_HEADS), (0, 0)))
    o_cmp, sel = _cmp_decode(layer, page_table, cmp_pool, cw, qh, npp)
    nsb_all = past // SLC_BLOCK + 1
    per_page = page // SLC_BLOCK
    nflag = (npages + 1) * per_page
    flags = jnp.pad(sel[:, :C_KV, :nsb_all] > 0.5, ((0, 0), (0, 0), (0, nflag - nsb_all)))
    need = flags[:, :, :npages * per_page].reshape(nb, C_KV, npages, per_page).any(axis=(1, 3))
    nslot = -(-min(npages, C_KV * SLC_TOPN) // npp) * npp
    pidx = jnp.arange(npages, dtype=jnp.int32)[None, :]
    order = jnp.argsort(jnp.where(need, pidx, pidx + npages), axis=1)[:, :nslot]
    cnt = jnp.sum(need, axis=1).astype(jnp.int32)
    last = jnp.take_along_axis(order, jnp.maximum(cnt - 1, 0)[:, None], axis=1)
    lpage = jnp.where(jnp.arange(nslot)[None, :] < cnt[:, None], order, last).astype(jnp.int32)
    plist = jnp.take_along_axis(page_table, lpage, axis=1)
    g3 = misc[:, GATE_OFF:GATE_OFF + 3 * C_HEADS].reshape(nb, 3, C_HEADS).transpose(0, 2, 1)
    gates = jnp.repeat(g3, HEAD_DIM, axis=1)
    nbias, fbias, wbias, b0 = sb
    new_t = _lane_zero(jnp.concatenate([slc_r, win_r], axis=1), page)
    c_out = _slc_decode(layer, plist.reshape(-1), lpage.reshape(-1), cnt, flags.reshape(-1).astype(jnp.int32), nflag,
                        _lane_repeat(qc, page), slc_t, nbias, fbias, win_t, wbias, b0, new_t, gates,
                        o_cmp[:, :C_HEADS].reshape(nb, C_W, 1), npp, nslot, npages - 1).reshape(nb, C_W)

    y = _mix_ffn(a_out, b_out, c_out, x2, lw, nb)
    kv5 = lambda r: r.reshape(nb, 1, 2, C_KV, HEAD_DIM)
    states = (kva.reshape(nb, 1, 2, A_HEADS, HEAD_DIM), lf[:, None, :], kv5(cmp_r), kv5(slc_r), kv5(win_r),
              vn.reshape(nb, 1, B_GROUPS, HEAD_DIM))
    return y.reshape(nb, 1, D_MODEL), states


def kernel(x_prompt, x_sample, cache_fox_kv, cache_fox_logf, cache_cmp_kv, cache_slc_kv, state_win_kv, page_table, rel_bias_table, norm_pre_mix, w_in, b_forget, gmlp_ln_g, gmlp_ln_b, gmlp_ws, gmlp_bs, cmp_pe, cmp_w1, cmp_b1, cmp_w2, cmp_b2, norm_group_a, norm_group_b, norm_group_c, w_o, norm_post_mix, norm_pre_ffn, w_ffn_in, w_ffn_out, norm_post_ffn):
    depth = w_in.shape[0]
    assert x_sample.shape[1] == 1
    B, S, _ = x_prompt.shape
    nb = x_sample.shape[0]
    n_pool, page = cache_fox_kv.shape[1], cache_fox_kv.shape[2]
    wb = state_win_kv.shape[2]

    w_in_p = _prep_w_in(w_in)
    bf = jnp.pad(b_forget, ((0, 0), (0, LANES - A_HEADS)))[:, None, :]
    cw_prompt, cw_sample = _prep_compress(cmp_pe, cmp_w1, cmp_b1, cmp_w2, cmp_b2)
    tb = _prompt_bias_tiles(rel_bias_table)
    sb = _sample_bias(rel_bias_table, page, wb)
    w_o_b = w_o.astype(BF16)
    w_fi_b = w_ffn_in.astype(BF16)
    w_fo_b = w_ffn_out.astype(BF16)
    bst = gmlp_bs.transpose(0, 2, 1)
    feat_pos = lambda a: jnp.moveaxis(a, 2, -1).reshape(a.shape[0], a.shape[1], -1, a.shape[2])
    chunks = page // CMP_STRIDE
    cmp_x = cache_cmp_kv.reshape(depth, n_pool, chunks, CMP_STRIDE, 2, C_KV, HEAD_DIM)
    cmp_x = cmp_x.transpose(0, 1, 4, 5, 2, 3, 6).reshape(depth, n_pool, 2 * C_KV, chunks, CMP_STRIDE * HEAD_DIM)
    logf_t = jnp.pad(cache_fox_logf.transpose(0, 1, 3, 2), ((0, 0), (0, 0), (0, HEAD_PAD - A_HEADS), (0, 0)))
    pools = (feat_pos(cache_fox_kv), logf_t, cmp_x, feat_pos(cache_slc_kv), feat_pos(state_win_kv))

    tm_proj = min(512, B * S)
    tm_mix = min(256, B * S)
    tq_fox = min(512, S)

    w_state = _prep_w_state(w_in)
    xp, xs = x_prompt, x_sample
    prompt_states = None
    logf_p = []
    ss = [[] for _ in range(6)]
    for l in range(depth):
        r1 = lambda a: a[l][None, :]
        lw = {
            "g_pre": r1(norm_pre_mix), "w_in": w_in_p[l], "w_state": w_state[l], "bf": bf[l], "lng": r1(gmlp_ln_g),
            "lnb": r1(gmlp_ln_b), "ws": gmlp_ws[l], "bst": bst[l], "ga": r1(norm_group_a), "gb": r1(norm_group_b),
            "gc": r1(norm_group_c), "w_o": w_o_b[l], "gpm": r1(norm_post_mix), "gpf": r1(norm_pre_ffn),
            "w_ffn_in": w_fi_b[l], "w_ffn_out": w_fo_b[l], "gpo": r1(norm_post_ffn),
        }
        xp, lf, prompt_states = _layer_prompt(l, depth, xp, lw, {k: v[l] for k, v in cw_prompt.items()}, tb,
                                              prompt_states, tm_proj, tm_mix, tq_fox)
        logf_p.append(lf)
        xs, st_s = _layer_sample(l, xs, lw, {k: v[l] for k, v in cw_sample.items()}, sb, pools, page_table)
        for lst, st in zip(ss, st_s):
            lst.append(st)
    pos_major = lambda a, heads: jnp.moveaxis(a.reshape(depth, B, 2, heads, HEAD_DIM, a.shape[-1]), -1, 2)
    fox_t, cmp_t, slc_t, win_t = prompt_states
    wbp = min(WINDOW, S)
    sp = [pos_major(fox_t, A_HEADS), jnp.stack(logf_p), pos_major(cmp_t, C_KV), pos_major(slc_t, C_KV),
          pos_major(win_t[..., S - wbp:], C_KV)]
    ss = [jnp.stack(st) for st in ss]
    ss[4] = jnp.concatenate([state_win_kv[:, :, 1:], ss[4]], axis=2)
    return tuple([xp, xs] + sp + ss)
```

```python
import functools
import math

import numpy as np
import jax
import jax.numpy as jnp
from jax import lax
from jax.experimental import pallas as pl
from jax.experimental.pallas import tpu as pltpu

F32 = jnp.float32
BF16 = jnp.bfloat16
HIGHEST = lax.Precision.HIGHEST

D_MODEL = 1024
HEAD_DIM = 64
A_HEADS = 6
B_GROUPS = 4
C_HEADS = 6
C_KV = 2
C_REP = C_HEADS // C_KV
A_W = A_HEADS * HEAD_DIM
B_W = B_GROUPS * HEAD_DIM
C_W = C_HEADS * HEAD_DIM
KV_W = C_KV * HEAD_DIM
Q_BLOCK = 128
CHUNK = 128
CMP_LEN = 32
CMP_STRIDE = 16
CMP_HIDDEN = 256
SLC_BLOCK = 64
SLC_TOPN = 16
WINDOW = 512
NUM_BUCKETS = 32
MAX_DISTANCE = 128
D_FF = 2816
EPS = 1e-6
NEG = -1e30
M_INIT = 0.5 * NEG
FORCE = 1e4
LOG2E = math.log2(math.e)
SCALE = HEAD_DIM ** -0.5 * LOG2E
PROJ_WIDTHS = (A_W, A_W, A_W, A_HEADS, B_W, B_W, C_W, KV_W, KV_W, KV_W, KV_W, KV_W, KV_W, 3 * C_HEADS)

LANES = 128
SUBLANES = 8
HEAD_PAD = 8
VMEM_LIMIT = 56 * 1024 * 1024
PAGES_PER_STEP = 16
CMP_PAGES_PER_STEP = 32

P_QA = (0, 384)
P_KVA = (384, 1152)
P_BU = (1152, 1408)
P_BV = (1408, 1664)
P_QC = (1664, 2048)
P_CMP = (2048, 2304)
P_SLC = (2304, 2560)
P_WIN = (2560, 2816)
P_MISC = (2816, 2944)
N_PROJ = 2944
GATE_OFF = A_HEADS


def _dot(a, b):
    return jnp.dot(a, b, preferred_element_type=F32)


def _dot_exact(a, b):
    return jnp.dot(a, b, preferred_element_type=F32, precision=HIGHEST)


def _dot_nt(a, b, precision=None):
    return lax.dot_general(a, b, (((1,), (1,)), ((), ())), preferred_element_type=F32, precision=precision)


def _gelu(x):
    return 0.5 * x * (1.0 + jnp.tanh(math.sqrt(2.0 / math.pi) * (x + 0.044715 * (x * x * x))))


def _sigmoid(x):
    return 1.0 / (1.0 + jnp.exp(-x))


def _rms(x, g):
    return x * lax.rsqrt(jnp.mean(x * x, axis=-1, keepdims=True) + EPS) * g


def _iota(shape, dim):
    return lax.broadcasted_iota(jnp.int32, shape, dim)


def _shr(x, n):
    return jnp.right_shift(x, int(math.log2(n)))


def _cparams(sem):
    return pltpu.CompilerParams(dimension_semantics=sem, vmem_limit_bytes=VMEM_LIMIT)


def _const_spec(shape):
    nd = len(shape)
    return pl.BlockSpec(shape, lambda *_: (0,) * nd)


def _proj_kernel(x_ref, g_ref, w_ref, bf_ref, lng_ref, lnb_ref,
                 qa_ref, kvab_ref, bu_ref, vn_ref, qc_ref, cmp_ref, kvcb_ref, misc_ref,
                 kva_ref=None, slc_ref=None, win_ref=None):
    h = _rms(x_ref[...], g_ref[...]).astype(BF16)

    def mm(seg):
        return _dot(h, w_ref[:, seg[0]:seg[1]])

    qa_ref[...] = (mm(P_QA) * SCALE).astype(BF16)
    kva = mm(P_KVA)
    if kva_ref is not None:
        kva_ref[...] = kva
    kvab_ref[...] = kva.astype(BF16)
    bu_ref[...] = _gelu(mm(P_BU))
    v = _gelu(mm(P_BV))
    mu = jnp.mean(v, axis=-1, keepdims=True)
    var = jnp.mean(jnp.square(v - mu), axis=-1, keepdims=True)
    vn_ref[...] = (v - mu) * lax.rsqrt(var + EPS) * lng_ref[...] + lnb_ref[...]
    qc_ref[...] = (mm(P_QC) * SCALE).astype(BF16)
    c = mm(P_CMP)
    cmp_ref[...] = c
    kvcb_ref[:, 0:256] = c.astype(BF16)
    c = mm(P_SLC)
    if slc_ref is not None:
        slc_ref[...] = c
    kvcb_ref[:, 256:512] = c.astype(BF16)
    c = mm(P_WIN)
    if win_ref is not None:
        win_ref[...] = c
    kvcb_ref[:, 512:768] = c.astype(BF16)
    m = mm(P_MISC) + bf_ref[...]
    lane = _iota(m.shape, 1)
    logsig = jnp.minimum(m, 0.0) - jnp.log1p(jnp.exp(-jnp.abs(m)))
    misc_ref[...] = jnp.where(lane < A_HEADS, logsig, _sigmoid(m))


def _proj(x2d, g, w, bf, lng, lnb, tm, state_rows):
    R = x2d.shape[0]
    row = lambda n: pl.BlockSpec((tm, n), lambda i: (i, 0))
    outs = [("qa", 384, BF16), ("kvab", 768, BF16), ("bu", 256, F32), ("vn", 256, F32), ("qc", 384, BF16),
            ("cmp", 256, F32), ("kvcb", 768, BF16), ("misc", 128, F32)]
    if state_rows:
        outs += [("kva", 768, F32), ("slc", 256, F32), ("win", 256, F32)]
    res = pl.pallas_call(
        _proj_kernel,
        grid=(R // tm,),
        in_specs=[row(D_MODEL), _const_spec((1, D_MODEL)), _const_spec((D_MODEL, N_PROJ)),
                  _const_spec((1, 128)), _const_spec((1, B_W)), _const_spec((1, B_W))],
        out_specs=[row(n) for _, n, _ in outs],
        out_shape=[jax.ShapeDtypeStruct((R, n), dt) for _, n, dt in outs],
        compiler_params=_cparams(("parallel",)),
        name="proj",
    )(x2d, g, w, bf, lng, lnb)
    return {name: r for (name, _, _), r in zip(outs, res)}


STATE_FEATS = (2 * A_W, 2 * KV_W, 2 * KV_W, 2 * KV_W)


def _state_kernel(x_ref, g_ref, wt_ref, *refs):
    outs = refs[-len(STATE_FEATS):]
    h = _rms(x_ref[...], g_ref[...]).astype(BF16)
    lo = 0
    for o_ref, n in zip(outs, STATE_FEATS):
        o_ref[...] = _dot_nt(wt_ref[lo:lo + n, :], h)
        lo += n


def _state_proj(layer, x2d, g, wt, prev, depth, B, S, tm):
    nblk = S // tm
    shapes = [jax.ShapeDtypeStruct((depth, B, n, S), F32) for n in STATE_FEATS]
    out_specs = [pl.BlockSpec((None, None, n, tm), lambda i: (layer, i // nblk, 0, i % nblk)) for n in STATE_FEATS]
    in_specs = [pl.BlockSpec((tm, D_MODEL), lambda i: (i, 0)), _const_spec((1, D_MODEL)), _const_spec(wt.shape)]
    args = [x2d, g, wt]
    aliases = {}
    if prev is not None:
        in_specs += [pl.BlockSpec(memory_space=pl.ANY)] * len(prev)
        aliases = {len(args) + k: k for k in range(len(prev))}
        args += list(prev)
    return pl.pallas_call(
        _state_kernel,
        grid=(B * nblk,),
        in_specs=in_specs,
        out_specs=out_specs,
        out_shape=shapes,
        input_output_aliases=aliases,
        compiler_params=_cparams(("arbitrary",)),
        name="state_proj",
    )(*args)


def _fox_kernel(q_ref, k_ref, v_ref, c_ref, o_ref, *, tq):
    qi = pl.program_id(2)
    tk = tq
    q = q_ref[...]
    lane = _iota((1, LANES), 1)
    low = lane < HEAD_DIM
    zero = jnp.zeros_like(q)
    qm = (jnp.where(low, q, zero), jnp.where(low, zero, q))
    one_lane = (HEAD_DIM, 0)

    def v_aug(v, hh):
        mine = low if hh == 0 else jnp.logical_not(low)
        ones = jnp.where(lane == one_lane[hh], 1.0, 0.0).astype(BF16)
        return jnp.where(mine, v, jnp.broadcast_to(ones, v.shape))

    def tile(ki, carry, causal):
        base = pl.multiple_of(ki * tk, tk)
        k = k_ref[pl.ds(base, tk), :]
        v = v_ref[pl.ds(base, tk), :]
        out = []
        for hh in range(2):
            m, acc = carry[hh]
            s = _dot_nt(qm[hh], k) - c_ref[hh:hh + 1, pl.ds(base, tk)]
            if causal is not None:
                s = jnp.where(causal, s, NEG)
            m_new = jnp.maximum(m, jnp.max(s, axis=-1, keepdims=True))
            alpha = jnp.exp2(m - m_new)
            p = jnp.exp2(s - m_new)
            acc = alpha * acc + _dot(p.astype(BF16), v_aug(v, hh))
            out.append((m_new, acc))
        return tuple(out)

    init = tuple((jnp.full((tq, 1), M_INIT, F32), jnp.zeros((tq, LANES), F32)) for _ in range(2))
    carry = lax.fori_loop(0, qi, lambda ki, c: tile(ki, c, None), init)
    causal = _iota((1, tk), 1) <= _iota((tq, 1), 0)
    (_, acc0), (_, acc1) = tile(qi, carry, causal)
    l0 = acc0[:, one_lane[0]:one_lane[0] + 1]
    l1 = acc1[:, one_lane[1]:one_lane[1] + 1]
    o_ref[...] = jnp.where(low, acc0 * (1.0 / l0), acc1 * (1.0 / l1))


def _fox_prompt(qa, kvab, crow, tq):
    B, S, _ = qa.shape
    half = A_HEADS // 2
    return pl.pallas_call(
        functools.partial(_fox_kernel, tq=tq),
        grid=(B, half, S // tq),
        in_specs=[
            pl.BlockSpec((None, tq, LANES), lambda b, p, qi: (b, qi, p)),
            pl.BlockSpec((None, S, LANES), lambda b, p, qi: (b, 0, p)),
            pl.BlockSpec((None, S, LANES), lambda b, p, qi: (b, 0, half + p)),
            pl.BlockSpec((None, None, 2, S), lambda b, p, qi: (b, p, 0, 0)),
        ],
        out_specs=pl.BlockSpec((None, tq, LANES), lambda b, p, qi: (b, qi, p)),
        out_shape=jax.ShapeDtypeStruct((B, S, A_W), F32),
        compiler_params=_cparams(("parallel", "parallel", "arbitrary")),
        name="fox_prompt",
    )(qa, kvab, kvab, crow)


def _cumsum_kernel(x_ref, o_ref):
    upper = jnp.where(_iota((LANES, LANES), 0) <= _iota((LANES, LANES), 1), 1.0, 0.0)
    carry = jnp.zeros((x_ref.shape[0], 1), F32)
    for i in range(x_ref.shape[1] // LANES):
        lanes = slice(i * LANES, (i + 1) * LANES)
        c = _dot_exact(x_ref[:, lanes], upper) + carry
        o_ref[:, lanes] = c * LOG2E
        carry = c[:, LANES - 1:LANES]


def _cumsum_lanes(x):
    n, h, length = x.shape
    spec = pl.BlockSpec((None, h, length), lambda i: (i, 0, 0))
    return pl.pallas_call(
        _cumsum_kernel, grid=(n,), in_specs=[spec], out_specs=spec,
        out_shape=jax.ShapeDtypeStruct(x.shape, F32),
        compiler_params=_cparams(("parallel",)),
        name="cumsum_logf",
    )(x)


def _gmlp_kernel(u_ref, vn_ref, ws_ref, bst_ref, o_ref, *, nchunk):
    tril = _iota((CHUNK, CHUNK), 1) <= _iota((CHUNK, CHUNK), 0)
    lane = _iota((1, B_W), 1)
    ws = [jnp.where(tril, ws_ref[h], 0.0).astype(BF16) for h in range(B_GROUPS)]
    for c in range(nchunk):
        rows = slice(c * CHUNK, (c + 1) * CHUNK)
        vn = vn_ref[rows, :].astype(BF16)
        z = jnp.zeros((CHUNK, B_W), F32)
        for h in range(B_GROUPS):
            zh = _dot(ws[h], vn) + bst_ref[:, h:h + 1]
            z = jnp.where(_shr(lane, HEAD_DIM) == h, zh, z)
        o_ref[rows, :] = u_ref[rows, :] * z


def _gmlp(u, vn, ws, bst, nchunk):
    R = u.shape[0]
    tm = nchunk * CHUNK
    row = pl.BlockSpec((tm, B_W), lambda i: (i, 0))
    return pl.pallas_call(
        functools.partial(_gmlp_kernel, nchunk=nchunk),
        grid=(R // tm,),
        in_specs=[row, row, _const_spec((B_GROUPS, CHUNK, CHUNK)), _const_spec((CHUNK, B_GROUPS))],
        out_specs=row,
        out_shape=jax.ShapeDtypeStruct((R, B_W), F32),
        compiler_params=_cparams(("parallel",)),
        name="gmlp",
    )(u, vn, ws, bst)


def _compress_core(x, w, pe_a, pe_b, b1, w2, b2):
    nc = x.shape[0]
    half = w.shape[1] // 2
    h = _dot(x, w)
    const = _dot(pe_a, w[:, :half]) + _dot(pe_b, w[:, half:])
    second = pltpu.roll(h[:, half:], nc - 1, 0)
    hid = _gelu(h[:, :half] + second + const[0:1, :] + b1)
    return _dot(hid.astype(BF16), w2) + b2


def _cmp_prompt_kernel(krows_ref, vrows_ref, wab_ref, pe_ref, b1_ref, w2_ref, b2_ref, o_ref, *, nc):
    for kv, rows_ref in enumerate((krows_ref, vrows_ref)):
        x = jnp.concatenate(
            [rows_ref[pl.ds(l, nc, stride=CMP_STRIDE), :] for l in range(CMP_STRIDE)], axis=-1).astype(BF16)
        o_ref[:, kv * KV_W:(kv + 1) * KV_W] = _compress_core(
            x, wab_ref[kv], pe_ref[kv, 0], pe_ref[kv, 1], b1_ref[kv], w2_ref[kv], b2_ref[kv]).astype(BF16)


def _compress_prompt(cmp_rows, cw):
    B, S, _ = cmp_rows.shape
    nc = S // CMP_STRIDE
    return pl.pallas_call(
        functools.partial(_cmp_prompt_kernel, nc=nc),
        grid=(B,),
        in_specs=[pl.BlockSpec((None, S, KV_W), lambda b: (b, 0, 0)), pl.BlockSpec((None, S, KV_W), lambda b: (b, 0, 1)),
                  _const_spec(cw["wab"].shape), _const_spec(cw["pe"].shape), _const_spec(cw["b1t"].shape),
                  _const_spec(cw["w2bd"].shape), _const_spec(cw["b2t"].shape)],
        out_specs=pl.BlockSpec((None, nc, 2 * KV_W), lambda b: (b, 0, 0)),
        out_shape=jax.ShapeDtypeStruct((B, nc, 2 * KV_W), BF16),
        compiler_params=_cparams(("parallel",)),
        name="compress_prompt",
    )(cmp_rows, cmp_rows, cw["wab"], cw["pe"], cw["b1t"], cw["w2bd"], cw["b2t"])


def _rank_select_t(score_t, topn):
    n = score_t.shape[0]
    groups = [score_t[lo:lo + SUBLANES] for lo in range(0, n, SUBLANES)]
    sub = _iota((SUBLANES, 1), 0)
    ranks = [jnp.zeros(g.shape, F32) for g in groups]
    for jp in range(n):
        row = score_t[jp:jp + 1, :]
        for gi, sg in enumerate(groups):
            lo = gi * SUBLANES
            if lo + SUBLANES - 1 <= jp:
                beats = row > sg
            elif lo > jp:
                beats = row >= sg
            else:
                beats = (row > sg) | ((row == sg) & (sub + lo > jp))
            ranks[gi] = ranks[gi] + jnp.where(beats, 1.0, 0.0)
    return jnp.concatenate([jnp.where(r < topn, 1.0, 0.0) for r in ranks], axis=0)


def _attend_tiles(qg, tiles, carry):
    m, acc = carry
    s_parts = []
    for k, _, bias, mask in tiles:
        s = _dot_nt(qg, k)
        if bias is not None:
            s = s + bias
        s_parts.append(jnp.where(mask, s, NEG))
    m_new = m
    for s in s_parts:
        m_new = jnp.maximum(m_new, jnp.max(s, axis=-1, keepdims=True))
    acc = jnp.exp2(m - m_new) * acc
    for s, (_, v, _, _) in zip(s_parts, tiles):
        acc = acc + _dot(jnp.exp2(s - m_new).astype(BF16), v)
    return m_new, acc


def _nsa_kernel(q_ref, kv_ref, cmp_ref, misc_ref, tb_ref, o_ref, *, nbp, nsb):
    qi = pl.program_id(1)
    tq = Q_BLOCK
    nrow = C_REP * tq
    t0 = qi * tq
    trow = t0 + _iota((tq, 1), 0)
    trow3 = jnp.concatenate([trow] * C_REP, axis=0)
    tl3 = trow3 - t0
    sl = _iota((1, tq), 1)
    causal3 = sl <= tl3
    topn = min(SLC_TOPN, nsb)
    lane = _iota((1, LANES), 1)
    low = lane < HEAD_DIM
    per = SLC_BLOCK // CMP_STRIDE
    pool_t = jnp.where(_shr(_iota((nsb, nbp), 1), per) == _iota((nsb, nbp), 0), 1.0, 0.0)
    groups = range(C_KV)
    mine = [low, jnp.logical_not(low)]
    one_lane = [HEAD_DIM, 0]
    ones = [jnp.where(lane == one_lane[g], 1.0, 0.0).astype(BF16) for g in groups]
    ks_lo, vs_lo, kw_lo, vw_lo = 2 * KV_W, 3 * KV_W, 4 * KV_W, 5 * KV_W

    def kv_tile(base, width, k_lo, v_lo):
        k = kv_ref[pl.ds(base, width), k_lo:k_lo + KV_W]
        v = kv_ref[pl.ds(base, width), v_lo:v_lo + KV_W]
        return k, [jnp.where(mine[g], v, jnp.broadcast_to(ones[g], v.shape)) for g in groups]

    qg, o_cmp, sel = [], [], []
    for g in groups:
        parts = []
        for r in range(C_REP):
            h = C_REP * g + r
            src = q_ref[:, (h // 2) * LANES:(h // 2 + 1) * LANES].astype(F32)
            if h % 2 != g:
                src = pltpu.roll(src, HEAD_DIM, 1)
            parts.append(jnp.where(mine[g], src, 0.0))
        q = jnp.concatenate(parts, axis=0).astype(BF16)
        qg.append(q)

        s = _dot_nt(q, cmp_ref[:, 0:KV_W])
        valid = (_iota((1, nbp), 1) * CMP_STRIDE + (CMP_LEN - 1)) <= trow3
        s = jnp.where(valid, s, NEG)
        e = jnp.exp2(s - jnp.max(s, axis=-1, keepdims=True))
        prob = jnp.where(valid, e * (1.0 / jnp.sum(e, axis=-1, keepdims=True)), 0.0)
        o_cmp.append(_dot(prob.astype(BF16), cmp_ref[:, KV_W:2 * KV_W]))

        psum = prob[0:tq] + prob[tq:2 * tq] + prob[2 * tq:3 * tq]
        imp_t = _dot_nt(pool_t, psum, HIGHEST)
        j = _iota((nsb, 1), 0)
        tcol = t0 + _iota((1, tq), 1)
        cur = _shr(tcol, SLC_BLOCK)
        forced = (j == 0) | (j == cur) | (j == cur - 1)
        score_t = jnp.where(forced, FORCE, jnp.where(j * SLC_BLOCK <= tcol, imp_t, -1.0))
        sel.append(_rank_select_t(score_t, topn).T.astype(BF16))

    def sel_masks(base, width):
        blk = base // SLC_BLOCK + _shr(_iota((nsb, width), 1), SLC_BLOCK)
        expand = jnp.where(_iota((nsb, width), 0) == blk, 1.0, 0.0).astype(BF16)
        return [jnp.concatenate([_dot(sel[g], expand)] * C_REP, axis=0) > 0.5 for g in groups]

    init = (jnp.full((nrow, 1), M_INIT, F32), jnp.zeros((nrow, LANES), F32))

    def far_step(base, width, carry):
        k, v = kv_tile(base, width, ks_lo, vs_lo)
        masks = sel_masks(base, width)
        return tuple(_attend_tiles(qg[g], [(k, v[g], None, masks[g])], carry[g]) for g in groups)

    n_far = jnp.maximum(qi - 1, 0)
    n_big = n_far // 4
    carry = lax.fori_loop(0, n_big, lambda i, c: far_step(pl.multiple_of(i * (4 * tq), 4 * tq), 4 * tq, c),
                          (init, init))
    carry = lax.fori_loop(0, n_far - 4 * n_big,
                          lambda i, c: far_step(pl.multiple_of((n_big * 4 + i) * tq, tq), tq, c), carry)

    pbase = pl.multiple_of(jnp.maximum(qi - 1, 0) * tq, tq)
    dbase = pl.multiple_of(t0, tq)
    has_prev = qi >= 1
    far_w = WINDOW - tq
    fbase = pl.multiple_of(jnp.maximum(t0 - WINDOW, 0), tq)
    kpos = fbase + _iota((1, far_w), 1)
    far_mask = (kpos < t0 - tq) & (trow3 - kpos <= WINDOW)
    mask_p = sel_masks(pbase, tq)
    mask_d = sel_masks(dbase, tq)
    ksp, vsp = kv_tile(pbase, tq, ks_lo, vs_lo)
    ksd, vsd = kv_tile(dbase, tq, ks_lo, vs_lo)
    kwf, vwf = kv_tile(fbase, far_w, kw_lo, vw_lo)
    kwp, vwp = kv_tile(pbase, tq, kw_lo, vw_lo)
    kwd, vwd = kv_tile(dbase, tq, kw_lo, vw_lo)
    heads = [None] * C_HEADS
    for g in groups:
        bias_diag = tb_ref[g, 0]
        bias_prev = tb_ref[g, 1]
        _, acc = _attend_tiles(qg[g], [(ksp, vsp[g], bias_prev, mask_p[g] & has_prev),
                                       (ksd, vsd[g], bias_diag, mask_d[g] & causal3)], carry[g])
        o_slc = acc * (1.0 / acc[:, one_lane[g]:one_lane[g] + 1])
        cw = _attend_tiles(qg[g], [(kwf, vwf[g], None, far_mask)], init)
        _, acc = _attend_tiles(qg[g], [(kwp, vwp[g], bias_prev, (tl3 >= 0) & has_prev),
                                       (kwd, vwd[g], bias_diag, causal3)], cw)
        o_win = acc * (1.0 / acc[:, one_lane[g]:one_lane[g] + 1])
        for r in range(C_REP):
            h = C_REP * g + r
            rows = slice(r * tq, (r + 1) * tq)
            gate = lambda br: misc_ref[:, GATE_OFF + br * C_HEADS + h:GATE_OFF + br * C_HEADS + h + 1]
            heads[h] = gate(0) * o_cmp[g][rows] + gate(1) * o_slc[rows] + gate(2) * o_win[rows]

    for pair in range(C_HEADS // 2):
        halves = []
        for h in (2 * pair, 2 * pair + 1):
            x = heads[h]
            if h // C_REP != h % 2:
                x = pltpu.roll(x, HEAD_DIM, 1)
            halves.append(x)
        o_ref[:, pair * LANES:(pair + 1) * LANES] = jnp.where(low, halves[0], halves[1])


def _nsa_prompt(qc, kvcb, kcv, misc, tb):
    B, S, _ = qc.shape
    nbp = kcv.shape[1]
    nsb = nbp * CMP_STRIDE // SLC_BLOCK
    return pl.pallas_call(
        functools.partial(_nsa_kernel, nbp=nbp, nsb=nsb),
        grid=(B, S // Q_BLOCK),
        in_specs=[pl.BlockSpec((None, Q_BLOCK, C_W), lambda b, i: (b, i, 0)),
                  pl.BlockSpec((None, S, 6 * KV_W), lambda b, i: (b, 0, 0)),
                  pl.BlockSpec((None, nbp, 2 * KV_W), lambda b, i: (b, 0, 0)),
                  pl.BlockSpec((None, Q_BLOCK, 128), lambda b, i: (b, i, 0)),
                  _const_spec(tb.shape)],
        out_specs=pl.BlockSpec((None, Q_BLOCK, C_W), lambda b, i: (b, i, 0)),
        out_shape=jax.ShapeDtypeStruct((B, S, C_W), F32),
        compiler_params=_cparams(("parallel", "arbitrary")),
        name="nsa_prompt",
    )(qc, kvcb, kcv, misc, tb)


def _mix_ffn_kernel(a_ref, b_ref, c_ref, x_ref, ga_ref, gb_ref, gc_ref, wo_ref, gpm_ref, gpf_ref,
                    win_ref, wout_ref, gpo_ref, o_ref, *, nsplit):
    an = _rms(a_ref[...], ga_ref[...]).astype(BF16)
    bn = _rms(b_ref[...], gb_ref[...]).astype(BF16)
    cn = _rms(c_ref[...], gc_ref[...]).astype(BF16)
    mix = (_dot(an, wo_ref[0:A_W, :]) + _dot(bn, wo_ref[A_W:A_W + B_W, :])
           + _dot(cn, wo_ref[A_W + B_W:A_W + B_W + C_W, :]))
    x1 = x_ref[...] + _rms(mix, gpm_ref[...])
    h = _rms(x1, gpf_ref[...]).astype(BF16)
    wid = D_FF // nsplit
    y = jnp.zeros(x1.shape, F32)
    for c in range(nsplit):
        lo = c * wid
        gate = _dot(h, win_ref[:, lo:lo + wid])
        up = _dot(h, win_ref[:, D_FF + lo:D_FF + lo + wid])
        act = gate * _sigmoid(gate) * up
        y = y + _dot(act.astype(BF16), wout_ref[lo:lo + wid, :])
    o_ref[...] = x1 + _rms(y, gpo_ref[...])


def _mix_ffn(a, b, c, x, lw, tm):
    R = x.shape[0]
    row = lambda n: pl.BlockSpec((tm, n), lambda i: (i, 0))
    once = lambda shape: pl.BlockSpec(shape, lambda i: (0, 0), pipeline_mode=pl.Buffered(1))
    return pl.pallas_call(
        functools.partial(_mix_ffn_kernel, nsplit=2),
        grid=(R // tm,),
        in_specs=[row(A_W), row(B_W), row(C_W), row(D_MODEL),
                  once((1, A_W)), once((1, B_W)), once((1, C_W)), once((D_MODEL, D_MODEL)),
                  once((1, D_MODEL)), once((1, D_MODEL)), once((D_MODEL, 2 * D_FF)), once((D_FF, D_MODEL)),
                  once((1, D_MODEL))],
        out_specs=row(D_MODEL),
        out_shape=jax.ShapeDtypeStruct((R, D_MODEL), F32),
        compiler_params=_cparams(("parallel",)),
        name="mix_ffn",
    )(a, b, c, x, lw["ga"], lw["gb"], lw["gc"], lw["w_o"], lw["gpm"], lw["gpf"], lw["w_ffn_in"],
      lw["w_ffn_out"], lw["gpo"])


def _head_block(x, h):
    return x[h * HEAD_DIM:(h + 1) * HEAD_DIM]


def _pad_heads(x):
    return jnp.concatenate([x, jnp.zeros((HEAD_PAD - x.shape[0], x.shape[1]), F32)], axis=0)


def _row_softmax_step(s, m_prev):
    m_new = jnp.maximum(m_prev, jnp.max(s, axis=1, keepdims=True))
    return m_new, jnp.exp2(m_prev - m_new), jnp.exp2(s - m_new)


def _fox_dec_kernel(pt_ref, q_ref, *refs, npp):
    pages = refs[:npp]
    lfs = refs[npp:2 * npp]
    new_ref, newlf_ref, o_ref, m_sc, l_sc, c_sc, acc_sc = refs[2 * npp:]
    j = pl.program_id(1)
    page = pages[0].shape[1]
    q = q_ref[...]
    upper = jnp.where(_iota((page, page), 0) <= _iota((page, page), 1), 1.0, 0.0)

    @pl.when(j == 0)
    def _():
        m_sc[...] = jnp.full(m_sc.shape, M_INIT, F32)
        l_sc[...] = jnp.zeros(l_sc.shape, F32)
        c_sc[...] = jnp.zeros(c_sc.shape, F32)
        acc_sc[...] = jnp.zeros(acc_sc.shape, F32)

    def scores(kt):
        prod = kt * q
        return _pad_heads(jnp.concatenate(
            [jnp.sum(_head_block(prod, h), axis=0, keepdims=True) for h in range(A_HEADS)], axis=0))

    def accumulate(alpha, p_tiles, v_tiles):
        for h in range(A_HEADS):
            a = _head_block(acc_sc, h) * alpha[h:h + 1, :]
            for p, v in zip(p_tiles, v_tiles):
                a = a + p[h:h + 1, :] * v[A_W + h * HEAD_DIM:A_W + (h + 1) * HEAD_DIM, :]
            acc_sc[h * HEAD_DIM:(h + 1) * HEAD_DIM, :] = a

    c_local = _dot_exact(jnp.concatenate([lf[...] for lf in lfs], axis=0), upper)
    totals = [jnp.broadcast_to(c_local[i * HEAD_PAD:(i + 1) * HEAD_PAD, page - 1:page], (HEAD_PAD, page))
              for i in range(npp)]
    before = jnp.broadcast_to(c_sc[...], (HEAD_PAD, page))
    s_parts = []
    for i, pg in enumerate(pages):
        s_parts.append(scores(pg[0:A_W, :]) - LOG2E * (c_local[i * HEAD_PAD:(i + 1) * HEAD_PAD] + before))
        before = before + totals[i]
    c_sc[...] = before[:, 0:1]
    m_new, alpha, p = _row_softmax_step(jnp.concatenate(s_parts, axis=1), m_sc[...])
    l_sc[...] = alpha * l_sc[...] + jnp.sum(p, axis=1, keepdims=True)
    m_sc[...] = m_new
    accumulate(alpha, [p[:, i * page:(i + 1) * page] for i in range(npp)], pages)

    @pl.when(j == pl.num_programs(1) - 1)
    def _():
        first = _iota((1, page), 1) == 0
        s_new = jnp.where(first, scores(new_ref[0:A_W, :]) - LOG2E * (c_sc[...] + newlf_ref[...]), NEG)
        m_new, alpha, p_new = _row_softmax_step(s_new, m_sc[...])
        l_inv = 1.0 / (alpha * l_sc[...] + jnp.sum(p_new, axis=1, keepdims=True))
        accumulate(alpha, [p_new], [new_ref])
        for h in range(A_HEADS):
            rows = slice(h * HEAD_DIM, (h + 1) * HEAD_DIM)
            o_ref[rows, :] = jnp.sum(acc_sc[rows, :], axis=1, keepdims=True) * l_inv[h:h + 1, :]


def _fox_decode(layer, page_table, q_rep, pool_t, logf_t, new_t, new_lf, npp):
    nb, npages = page_table.shape
    page = pool_t.shape[3]
    page_of = lambda b, j, i, pt: pt[b, j * npp + i]
    page_spec = lambda i: pl.BlockSpec((None, None, 2 * A_W, page), lambda b, j, pt: (layer, page_of(b, j, i, pt), 0, 0))
    logf_spec = lambda i: pl.BlockSpec((None, None, HEAD_PAD, page), lambda b, j, pt: (layer, page_of(b, j, i, pt), 0, 0))
    per_b = lambda shape: pl.BlockSpec((None,) + shape, lambda b, j, pt: (b,) + (0,) * len(shape))
    grid_spec = pltpu.PrefetchScalarGridSpec(
        num_scalar_prefetch=1,
        grid=(nb, npages // npp),
        in_specs=[per_b((A_W, page))] + [page_spec(i) for i in range(npp)] + [logf_spec(i) for i in range(npp)]
        + [per_b((2 * A_W, page)), per_b((HEAD_PAD, 1))],
        out_specs=per_b((A_W, 1)),
        scratch_shapes=[pltpu.VMEM((HEAD_PAD, 1), F32), pltpu.VMEM((HEAD_PAD, 1), F32), pltpu.VMEM((HEAD_PAD, 1), F32),
                        pltpu.VMEM((A_W, page), F32)],
    )
    return pl.pallas_call(
        functools.partial(_fox_dec_kernel, npp=npp),
        grid_spec=grid_spec,
        out_shape=jax.ShapeDtypeStruct((nb, A_W, 1), F32),
        compiler_params=_cparams(("parallel", "arbitrary")),
        name="fox_decode",
    )(page_table, q_rep, *([pool_t] * npp), *([logf_t] * npp), new_t, new_lf)


def _cmp_dec_kernel(pt_ref, *refs, npp, nc, nsb_all):
    pages = refs[:npp]
    w_ref, pe_ref, b1_ref, w2_ref, b2_ref, q_ref, ocmp_ref, sel_ref, x_sc = refs[npp:]
    j = pl.program_id(1)
    per_page = pages[0].shape[1]
    for i, pg in enumerate(pages):
        row0 = pl.multiple_of((j * npp + i) * per_page, per_page)
        for c in range(2 * C_KV):
            x_sc[c, pl.ds(row0, per_page), :] = pg[c]

    @pl.when(j == pl.num_programs(1) - 1)
    def _():
        kc = [[_compress_core(x_sc[kv * C_KV + g].astype(BF16), w_ref[kv], pe_ref[kv, 0], pe_ref[kv, 1],
                              b1_ref[kv], w2_ref[kv], b2_ref[kv]).astype(BF16)
               for g in range(C_KV)] for kv in range(2)]
        q = q_ref[...]
        grp0 = _iota((HEAD_PAD, 1), 0) < C_REP
        pick = lambda x0, x1: jnp.where(grp0, x0, x1)
        s = pick(_dot_nt(q, kc[0][0]), _dot_nt(q, kc[0][1]))
        valid = _iota((1, nc), 1) < nc - 1
        s = jnp.where(valid, s, NEG)
        e = jnp.exp2(s - jnp.max(s, axis=-1, keepdims=True))
        prob = jnp.where(valid, e / jnp.sum(e, axis=-1, keepdims=True), 0.0)
        pb = prob.astype(BF16)
        ocmp_ref[...] = pick(_dot(pb, kc[1][0]), _dot(pb, kc[1][1]))

        per = SLC_BLOCK // CMP_STRIDE
        nsbp = sel_ref.shape[1]
        hrow = _iota((HEAD_PAD, 1), 0)
        psum = jnp.concatenate(
            [jnp.sum(jnp.where((hrow >= C_REP * g) & (hrow < C_REP * (g + 1)), prob, 0.0), axis=0, keepdims=True)
             for g in range(C_KV)] + [jnp.zeros((HEAD_PAD - C_KV, nc), F32)], axis=0)
        pool = jnp.where(_shr(_iota((nc, nsbp), 0), per) == _iota((nc, nsbp), 1), 1.0, 0.0)
        imp = _dot_exact(psum, pool)
        jl = _iota((1, nsbp), 1)
        cur = nsb_all - 1
        forced = (jl == 0) | (jl == cur) | (jl == cur - 1)
        score = jnp.where(forced, FORCE, jnp.where(jl <= cur, imp, -2.0))
        topn = min(SLC_TOPN, nsb_all)
        rank = jnp.zeros(score.shape, F32)
        for jp in range(nsb_all):
            col = score[:, jp:jp + 1]
            beats = (col > score) | ((col == score) & (jp < jl))
            rank = rank + jnp.where(beats, 1.0, 0.0)
        sel_ref[...] = jnp.where(rank < topn, 1.0, 0.0)


def _cmp_decode(layer, page_table, pool, cw, qh, npp):
    nb, npages = page_table.shape
    per_page, width = pool.shape[3], pool.shape[4]
    nc = npages * per_page
    nsb_all = nc * CMP_STRIDE // SLC_BLOCK + 1
    nsbp = -(-nsb_all // LANES) * LANES

    def page_spec(i):
        return pl.BlockSpec((None, None, 2 * C_KV, per_page, width),
                            lambda b, j, pt: (layer, pt[b, j * npp + i], 0, 0, 0))

    const = lambda a: pl.BlockSpec(a.shape, lambda b, j, pt: (0,) * a.ndim)
    grid_spec = pltpu.PrefetchScalarGridSpec(
        num_scalar_prefetch=1,
        grid=(nb, npages // npp),
        in_specs=[page_spec(i) for i in range(npp)]
        + [const(cw["w"]), const(cw["pe"]), const(cw["b1"]), const(cw["w2"]), const(cw["b2"]),
           pl.BlockSpec((None, HEAD_PAD, HEAD_DIM), lambda b, j, pt: (b, 0, 0))],
        out_specs=[pl.BlockSpec((None, HEAD_PAD, HEAD_DIM), lambda b, j, pt: (b, 0, 0)),
                   pl.BlockSpec((None, HEAD_PAD, nsbp), lambda b, j, pt: (b, 0, 0))],
        scratch_shapes=[pltpu.VMEM((2 * C_KV, nc, width), F32)],
    )
    return pl.pallas_call(
        functools.partial(_cmp_dec_kernel, npp=npp, nc=nc, nsb_all=nsb_all),
        grid_spec=grid_spec,
        out_shape=[jax.ShapeDtypeStruct((nb, HEAD_PAD, HEAD_DIM), F32),
                   jax.ShapeDtypeStruct((nb, HEAD_PAD, nsbp), F32)],
        compiler_params=_cparams(("parallel", "arbitrary")),
        name="cmp_decode",
    )(page_table, *([pool] * npp), cw["w"], cw["pe"], cw["b1"], cw["w2"], cw["b2"], qh)


def _slc_dec_kernel(plist_ref, lpage_ref, cnt_ref, flag_ref, q_ref, *refs, npp, nslot, nflag, last_page):
    pages = refs[:npp]
    (nbias_ref, fbias_ref, win_ref, wbias_ref, b0_ref, new_ref, gate_ref, ocmp_ref, o_ref,
     m_sc, l_sc, acc_sc) = refs[npp:]
    b = pl.program_id(0)
    j = pl.program_id(1)
    page = pages[0].shape[1]
    q = q_ref[...]
    grp0 = _iota((HEAD_PAD, 1), 0) < C_REP
    lane = _iota((1, page), 1)

    def scores(kt):
        return _pad_heads(jnp.concatenate(
            [jnp.sum(_head_block(kt, h // C_REP) * _head_block(q, h), axis=0, keepdims=True)
             for h in range(C_HEADS)], axis=0))

    def weighted(acc, alpha, p_tiles, v_tiles):
        out = []
        for h in range(C_HEADS):
            a = _head_block(acc, h) * alpha[h:h + 1, :]
            for p, v in zip(p_tiles, v_tiles):
                a = a + p[h:h + 1, :] * _head_block(v, h // C_REP)
            out.append(a)
        return jnp.concatenate(out, axis=0)

    def finish(acc, l):
        l_inv = 1.0 / l
        return jnp.concatenate(
            [jnp.sum(_head_block(acc, h), axis=1, keepdims=True) * l_inv[h:h + 1, :] for h in range(C_HEADS)], axis=0)

    @pl.when(j == 0)
    def _():
        m_sc[...] = jnp.full(m_sc.shape, M_INIT, F32)
        l_sc[...] = jnp.zeros(l_sc.shape, F32)
        acc_sc[...] = jnp.zeros(acc_sc.shape, F32)

    nblk = page // SLC_BLOCK
    s_parts = []
    for i, pg in enumerate(pages):
        slot = j * npp + i
        lp = lpage_ref[b * nslot + slot]
        live = slot < cnt_ref[b]
        mask = jnp.zeros((HEAD_PAD, page), jnp.int32)
        for blk in range(nblk):
            f = [flag_ref[(b * C_KV + g) * nflag + lp * nblk + blk] for g in range(C_KV)]
            mask = jnp.where(_shr(lane, SLC_BLOCK) == blk, jnp.where(grp0, f[0], f[1]), mask)
        mask = (mask > 0) & live
        bias = jnp.where(lp == last_page, nbias_ref[...], fbias_ref[...])
        s_parts.append(jnp.where(mask, scores(pg[0:KV_W, :]) + bias, NEG))
    m_new, alpha, p = _row_softmax_step(jnp.concatenate(s_parts, axis=1), m_sc[...])
    l_sc[...] = alpha * l_sc[...] + jnp.sum(p, axis=1, keepdims=True)
    m_sc[...] = m_new
    acc_sc[...] = weighted(acc_sc[...], alpha, [p[:, i * page:(i + 1) * page] for i in range(npp)],
                           [pg[KV_W:2 * KV_W, :] for pg in pages])

    @pl.when(j == pl.num_programs(1) - 1)
    def _():
        first = lane == 0
        s_new = jnp.where(first, scores(new_ref[0:KV_W, :]) + b0_ref[...], NEG)
        m_new, alpha, p_new = _row_softmax_step(s_new, m_sc[...])
        l = alpha * l_sc[...] + jnp.sum(p_new, axis=1, keepdims=True)
        o_slc = finish(weighted(acc_sc[...], alpha, [p_new], [new_ref[KV_W:2 * KV_W, :]]), l)
        nwin = win_ref.shape[1] // page
        tiles = lambda ref, lo: [ref[lo:lo + KV_W, c * page:(c + 1) * page] for c in range(nwin)]
        s_win = [scores(kt) + wbias_ref[:, c * page:(c + 1) * page] for c, kt in enumerate(tiles(win_ref, 0))]
        s_win.append(jnp.where(first, scores(new_ref[2 * KV_W:3 * KV_W, :]) + b0_ref[...], NEG))
        s = jnp.concatenate(s_win, axis=1)
        p = jnp.exp2(s - jnp.max(s, axis=1, keepdims=True))
        acc = weighted(jnp.zeros((C_W, page), F32), jnp.zeros((HEAD_PAD, 1), F32),
                       [p[:, c * page:(c + 1) * page] for c in range(nwin + 1)],
                       tiles(win_ref, KV_W) + [new_ref[3 * KV_W:4 * KV_W, :]])
        o_win = finish(acc, jnp.sum(p, axis=1, keepdims=True))
        o_ref[...] = gate_ref[:, 0:1] * ocmp_ref[...] + gate_ref[:, 1:2] * o_slc + gate_ref[:, 2:3] * o_win


def _slc_decode(layer, plist, lpage, cnt, flags, nflag, q_rep, pool_t, nbias, fbias, win_t, wbias, b0, new_t, gates,
                ocmp, npp, nslot, last_page):
    nb = q_rep.shape[0]
    page = pool_t.shape[3]
    wb = win_t.shape[3]

    def page_spec(i):
        return pl.BlockSpec((None, None, 2 * KV_W, page),
                            lambda b, j, pls, lps, cn, fl: (layer, pls[b * nslot + j * npp + i], 0, 0))

    fixed = lambda shape: pl.BlockSpec(shape, lambda b, j, pls, lps, cn, fl: (0,) * len(shape))
    per_b = lambda shape: pl.BlockSpec((None,) + shape, lambda b, j, pls, lps, cn, fl: (b,) + (0,) * len(shape))
    grid_spec = pltpu.PrefetchScalarGridSpec(
        num_scalar_prefetch=4,
        grid=(nb, nslot // npp),
        in_specs=[per_b((C_W, page))] + [page_spec(i) for i in range(npp)]
        + [fixed((HEAD_PAD, page)), fixed((HEAD_PAD, 1)),
           pl.BlockSpec((None, None, 2 * KV_W, wb), lambda b, j, pls, lps, cn, fl: (layer, b, 0, 0)),
           fixed((HEAD_PAD, wb)), fixed((HEAD_PAD, 1)), per_b((4 * KV_W, page)), per_b((C_W, 3)), per_b((C_W, 1))],
        out_specs=per_b((C_W, 1)),
        scratch_shapes=[pltpu.VMEM((HEAD_PAD, 1), F32), pltpu.VMEM((HEAD_PAD, 1), F32), pltpu.VMEM((C_W, page), F32)],
    )
    return pl.pallas_call(
        functools.partial(_slc_dec_kernel, npp=npp, nslot=nslot, nflag=nflag, last_page=last_page),
        grid_spec=grid_spec,
        out_shape=jax.ShapeDtypeStruct((nb, C_W, 1), F32),
        compiler_params=_cparams(("parallel", "arbitrary")),
        name="slc_win_decode",
    )(plist, lpage, cnt, flags, q_rep, *([pool_t] * npp), nbias, fbias, win_t, wbias, b0, new_t, gates, ocmp)


def _t5_bucket(dist):
    n = jnp.maximum(dist, 0)
    max_exact = NUM_BUCKETS // 2
    nf = jnp.maximum(n, 1).astype(F32)
    large = max_exact + (jnp.log(nf / max_exact) / math.log(MAX_DISTANCE / max_exact)
                         * (NUM_BUCKETS - max_exact)).astype(jnp.int32)
    return jnp.where(n < max_exact, n, jnp.minimum(large, NUM_BUCKETS - 1))


def _prep_w_in(w_in):
    splits = [int(s) for s in np.cumsum(PROJ_WIDTHS)[:-1]]
    a_q, a_k, a_v, a_f, b_u, b_v, c_q, c_kc, c_vc, c_ks, c_vs, c_kw, c_vw, c_g = jnp.split(w_in, splits, axis=-1)
    pad = jnp.zeros(w_in.shape[:-1] + (LANES - A_HEADS - 3 * C_HEADS,), w_in.dtype)
    return jnp.concatenate([a_q, a_k, a_v, b_u, b_v, c_q, c_kc, c_vc, c_ks, c_vs, c_kw, c_vw, a_f, c_g, pad],
                           axis=-1).astype(BF16)


def _prep_w_state(w_in):
    splits = [int(s) for s in np.cumsum(PROJ_WIDTHS)[:-1]]
    _, a_k, a_v, _, _, _, _, c_kc, c_vc, c_ks, c_vs, c_kw, c_vw, _ = jnp.split(w_in, splits, axis=-1)
    return jnp.concatenate([a_k, a_v, c_kc, c_vc, c_ks, c_vs, c_kw, c_vw], axis=-1).transpose(0, 2, 1).astype(BF16)


def _prep_compress(cmp_pe, cmp_w1, cmp_b1, cmp_w2, cmp_b2):
    depth = cmp_w1.shape[0]
    half = CMP_LEN // 2
    eye = jnp.eye(C_KV, dtype=F32)
    w1 = cmp_w1.reshape(depth, 2, 2, half, HEAD_DIM, CMP_HIDDEN)
    wab = jnp.einsum("zkhldj,gG->zklgdhGj", w1, eye).reshape(depth, 2, half * KV_W, 2 * C_KV * CMP_HIDDEN)
    pe = cmp_pe.reshape(depth, 2, 2, half, 1, HEAD_DIM)
    pe_bd = jnp.broadcast_to(pe, (depth, 2, 2, half, C_KV, HEAD_DIM)).reshape(depth, 2, 2, 1, half * KV_W)
    pe_bd = jnp.broadcast_to(pe_bd, (depth, 2, 2, SUBLANES, half * KV_W))
    w2bd = jnp.einsum("zkjd,gG->zkgjGd", cmp_w2, eye).reshape(depth, 2, C_KV * CMP_HIDDEN, KV_W)
    w_g = w1.transpose(0, 1, 3, 4, 2, 5).reshape(depth, 2, half * HEAD_DIM, 2 * CMP_HIDDEN)
    pe_g = jnp.broadcast_to(cmp_pe.reshape(depth, 2, 2, 1, half * HEAD_DIM), (depth, 2, 2, SUBLANES, half * HEAD_DIM))
    prompt = {
        "wab": wab.astype(BF16), "pe": pe_bd.astype(BF16),
        "b1t": jnp.tile(cmp_b1, (1, 1, C_KV))[:, :, None, :],
        "w2bd": w2bd.astype(BF16),
        "b2t": jnp.tile(cmp_b2, (1, 1, C_KV))[:, :, None, :],
    }
    sample = {"w": w_g.astype(BF16), "pe": pe_g.astype(BF16), "b1": cmp_b1[:, :, None, :],
              "w2": cmp_w2.astype(BF16), "b2": cmp_b2[:, :, None, :]}
    return prompt, sample


def _prompt_bias_tiles(table):
    tl = jnp.arange(Q_BLOCK)[:, None]
    sl = jnp.arange(Q_BLOCK)[None, :]
    d = tl - sl
    far = table[_t5_bucket(jnp.asarray(8 * MAX_DISTANCE))]
    tiles = []
    for delta in (0, Q_BLOCK):
        b = table[_t5_bucket(d + delta)] - far
        tiles.append(b.transpose(2, 0, 1).reshape(C_KV, C_REP * Q_BLOCK, Q_BLOCK))
    return (jnp.stack(tiles, axis=1) * LOG2E).astype(F32)


def _sample_bias(table, page, wb):
    rows = lambda x: jnp.pad(x.T * LOG2E, ((0, HEAD_PAD - C_HEADS), (0, 0))).astype(F32)
    nbias = rows(table[_t5_bucket(page - jnp.arange(page))])
    fbias = rows(table[_t5_bucket(jnp.asarray([8 * MAX_DISTANCE]))])
    wbias = rows(table[_t5_bucket(wb - jnp.arange(wb))])
    b0 = rows(table[0:1])
    return nbias, fbias, wbias, b0


def _lane_repeat(x, n):
    return jnp.broadcast_to(x.astype(F32)[:, :, None], x.shape + (n,))


def _lane_zero(x, n):
    return jnp.pad(x.astype(F32)[:, :, None], ((0, 0), (0, 0), (0, n - 1)))


def _layer_prompt(layer, depth, x, lw, cw, tb, states, tm_proj, tm_mix, tq_fox):
    B, S, _ = x.shape
    R = B * S
    x2 = x.reshape(R, D_MODEL)
    p = _proj(x2, lw["g_pre"], lw["w_in"], lw["bf"], lw["lng"], lw["lnb"], tm_proj, state_rows=False)
    states = _state_proj(layer, x2, lw["g_pre"], lw["w_state"], states, depth, B, S, tm_proj)
    lf = p["misc"][:, :A_HEADS].reshape(B, S, A_HEADS)
    lf_t = jnp.pad(lf, ((0, 0), (0, 0), (0, HEAD_PAD - A_HEADS))).transpose(0, 2, 1)
    crow = _cumsum_lanes(lf_t)[:, :A_HEADS].reshape(B, A_HEADS // 2, 2, S)
    a_out = _fox_prompt(p["qa"].reshape(B, S, A_W), p["kvab"].reshape(B, S, 2 * A_W), crow, tq_fox)
    b_out = _gmlp(p["bu"], p["vn"], lw["ws"], lw["bst"], min(8, R // CHUNK))
    kcv = _compress_prompt(p["cmp"].reshape(B, S, 2 * KV_W), cw)
    c_out = _nsa_prompt(p["qc"].reshape(B, S, C_W), p["kvcb"].reshape(B, S, 6 * KV_W), kcv,
                        p["misc"].reshape(B, S, 128), tb)
    y = _mix_ffn(a_out.reshape(R, A_W), b_out, c_out.reshape(R, C_W), x2, lw, tm_mix)
    return y.reshape(B, S, D_MODEL), lf, states


def _layer_sample(layer, x, lw, cw, sb, pools, page_table):
    nb = x.shape[0]
    fox_t, logf_t, cmp_pool, slc_t, win_t = pools
    npages = page_table.shape[1]
    page = fox_t.shape[3]
    past = npages * page
    npp = min(PAGES_PER_STEP, npages)
    x2 = x.reshape(nb, D_MODEL)
    p = _proj(x2, lw["g_pre"], lw["w_in"], lw["bf"], lw["lng"], lw["lnb"], nb, state_rows=True)
    qa, kva, bu, vn, qc, misc = p["qa"], p["kva"], p["bu"], p["vn"], p["qc"], p["misc"]
    cmp_r, slc_r, win_r = p["cmp"], p["slc"], p["win"]
    lf = misc[:, :A_HEADS]

    new_lf = jnp.pad(lf, ((0, 0), (0, HEAD_PAD - A_HEADS)))[:, :, None]
    a_out = _fox_decode(layer, page_table, _lane_repeat(qa, page), fox_t, logf_t, _lane_zero(kva, page), new_lf,
                        npp).reshape(nb, A_W)

    pad_chunk = lambda t: jnp.zeros((nb, CHUNK, B_W), F32).at[:, 0].set(t).reshape(nb * CHUNK, B_W)
    b_out = _gmlp(pad_chunk(bu), pad_chunk(vn), lw["ws"], lw["bst"], min(8, nb)).reshape(nb, CHUNK, B_W)[:, 0]

    qh = jnp.pad(qc.reshape(nb, C_HEADS, HEAD_DIM), ((0, 0), (0, HEAD_PAD - C_HEADS), (0, 0)))
    o_cmp, sel = _cmp_decode(layer, page_table, cmp_pool, cw, qh, min(CMP_PAGES_PER_STEP, npages))
    nsb_all = past // SLC_BLOCK + 1
    per_page = page // SLC_BLOCK
    nflag = (npages + 1) * per_page
    flags = jnp.pad(sel[:, :C_KV, :nsb_all] > 0.5, ((0, 0), (0, 0), (0, nflag - nsb_all)))
    need = flags[:, :, :npages * per_page].reshape(nb, C_KV, npages, per_page).any(axis=(1, 3))
    nslot = -(-min(npages, C_KV * SLC_TOPN) // npp) * npp
    pidx = jnp.arange(npages, dtype=jnp.int32)[None, :]
    order = jnp.argsort(jnp.where(need, pidx, pidx + npages), axis=1)[:, :nslot]
    cnt = jnp.sum(need, axis=1).astype(jnp.int32)
    last = jnp.take_along_axis(order, jnp.maximum(cnt - 1, 0)[:, None], axis=1)
    lpage = jnp.where(jnp.arange(nslot)[None, :] < cnt[:, None], order, last).astype(jnp.int32)
    plist = jnp.take_along_axis(page_table, lpage, axis=1)
    g3 = misc[:, GATE_OFF:GATE_OFF + 3 * C_HEADS].reshape(nb, 3, C_HEADS).transpose(0, 2, 1)
    gates = jnp.repeat(g3, HEAD_DIM, axis=1)
    nbias, fbias, wbias, b0 = sb
    new_t = _lane_zero(jnp.concatenate([slc_r, win_r], axis=1), page)
    c_out = _slc_decode(layer, plist.reshape(-1), lpage.reshape(-1), cnt, flags.reshape(-1).astype(jnp.int32), nflag,
                        _lane_repeat(qc, page), slc_t, nbias, fbias, win_t, wbias, b0, new_t, gates,
                        o_cmp[:, :C_HEADS].reshape(nb, C_W, 1), npp, nslot, npages - 1).reshape(nb, C_W)

    y = _mix_ffn(a_out, b_out, c_out, x2, lw, nb)
    kv5 = lambda r: r.reshape(nb, 1, 2, C_KV, HEAD_DIM)
    states = (kva.reshape(nb, 1, 2, A_HEADS, HEAD_DIM), lf[:, None, :], kv5(cmp_r), kv5(slc_r), kv5(win_r),
              vn.reshape(nb, 1, B_GROUPS, HEAD_DIM))
    return y.reshape(nb, 1, D_MODEL), states


def kernel(x_prompt, x_sample, cache_fox_kv, cache_fox_logf, cache_cmp_kv, cache_slc_kv, state_win_kv, page_table, rel_bias_table, norm_pre_mix, w_in, b_forget, gmlp_ln_g, gmlp_ln_b, gmlp_ws, gmlp_bs, cmp_pe, cmp_w1, cmp_b1, cmp_w2, cmp_b2, norm_group_a, norm_group_b, norm_group_c, w_o, norm_post_mix, norm_pre_ffn, w_ffn_in, w_ffn_out, norm_post_ffn):
    depth = w_in.shape[0]
    assert x_sample.shape[1] == 1
    B, S, _ = x_prompt.shape
    nb = x_sample.shape[0]
    n_pool, page = cache_fox_kv.shape[1], cache_fox_kv.shape[2]
    wb = state_win_kv.shape[2]

    w_in_p = _prep_w_in(w_in)
    bf = jnp.pad(b_forget, ((0, 0), (0, LANES - A_HEADS)))[:, None, :]
    cw_prompt, cw_sample = _prep_compress(cmp_pe, cmp_w1, cmp_b1, cmp_w2, cmp_b2)
    tb = _prompt_bias_tiles(rel_bias_table)
    sb = _sample_bias(rel_bias_table, page, wb)
    w_o_b = w_o.astype(BF16)
    w_fi_b = w_ffn_in.astype(BF16)
    w_fo_b = w_ffn_out.astype(BF16)
    bst = gmlp_bs.transpose(0, 2, 1)
    feat_pos = lambda a: jnp.moveaxis(a, 2, -1).reshape(a.shape[0], a.shape[1], -1, a.shape[2])
    chunks = page // CMP_STRIDE
    cmp_x = cache_cmp_kv.reshape(depth, n_pool, chunks, CMP_STRIDE, 2, C_KV, HEAD_DIM)
    cmp_x = cmp_x.transpose(0, 1, 4, 5, 2, 3, 6).reshape(depth, n_pool, 2 * C_KV, chunks, CMP_STRIDE * HEAD_DIM)
    logf_t = jnp.pad(cache_fox_logf.transpose(0, 1, 3, 2), ((0, 0), (0, 0), (0, HEAD_PAD - A_HEADS), (0, 0)))
    pools = (feat_pos(cache_fox_kv), logf_t, cmp_x, feat_pos(cache_slc_kv), feat_pos(state_win_kv))

    tm_proj = min(512, B * S)
    tm_mix = min(256, B * S)
    tq_fox = min(512, S)

    w_state = _prep_w_state(w_in)
    xp, xs = x_prompt, x_sample
    prompt_states = None
    logf_p = []
    ss = [[] for _ in range(6)]
    for l in range(depth):
        r1 = lambda a: a[l][None, :]
        lw = {
            "g_pre": r1(norm_pre_mix), "w_in": w_in_p[l], "w_state": w_state[l], "bf": bf[l], "lng": r1(gmlp_ln_g),
            "lnb": r1(gmlp_ln_b), "ws": gmlp_ws[l], "bst": bst[l], "ga": r1(norm_group_a), "gb": r1(norm_group_b),
            "gc": r1(norm_group_c), "w_o": w_o_b[l], "gpm": r1(norm_post_mix), "gpf": r1(norm_pre_ffn),
            "w_ffn_in": w_fi_b[l], "w_ffn_out": w_fo_b[l], "gpo": r1(norm_post_ffn),
        }
        xp, lf, prompt_states = _layer_prompt(l, depth, xp, lw, {k: v[l] for k, v in cw_prompt.items()}, tb,
                                              prompt_states, tm_proj, tm_mix, tq_fox)
        logf_p.append(lf)
        xs, st_s = _layer_sample(l, xs, lw, {k: v[l] for k, v in cw_sample.items()}, sb, pools, page_table)
        for lst, st in zip(ss, st_s):
            lst.append(st)
    pos_major = lambda a, heads: jnp.moveaxis(a.reshape(depth, B, 2, heads, HEAD_DIM, a.shape[-1]), -1, 2)
    fox_t, cmp_t, slc_t, win_t = prompt_states
    wbp = min(WINDOW, S)
    sp = [pos_major(fox_t, A_HEADS), jnp.stack(logf_p), pos_major(cmp_t, C_KV), pos_major(slc_t, C_KV),
          pos_major(win_t[..., S - wbp:], C_KV)]
    ss = [jnp.stack(st) for st in ss]
    ss[4] = jnp.concatenate([state_win_kv[:, :, 1:], ss[4]], axis=2)
    return tuple([xp, xs] + sp + ss)
```

```python
import functools
import math

import numpy as np
import jax
import jax.numpy as jnp
from jax import lax
from jax.experimental import pallas as pl
from jax.experimental.pallas import tpu as pltpu

F32 = jnp.float32
BF16 = jnp.bfloat16
HIGHEST = lax.Precision.HIGHEST

D_MODEL = 1024
HEAD_DIM = 64
A_HEADS = 6
B_GROUPS = 4
C_HEADS = 6
C_KV = 2
C_REP = C_HEADS // C_KV
A_W = A_HEADS * HEAD_DIM
B_W = B_GROUPS * HEAD_DIM
C_W = C_HEADS * HEAD_DIM
KV_W = C_KV * HEAD_DIM
Q_BLOCK = 128
CHUNK = 128
CMP_LEN = 32
CMP_STRIDE = 16
CMP_HIDDEN = 256
SLC_BLOCK = 64
SLC_TOPN = 16
WINDOW = 512
NUM_BUCKETS = 32
MAX_DISTANCE = 128
D_FF = 2816
EPS = 1e-6
NEG = -1e30
M_INIT = 0.5 * NEG
FORCE = 1e4
LOG2E = math.log2(math.e)
SCALE = HEAD_DIM ** -0.5 * LOG2E
PROJ_WIDTHS = (A_W, A_W, A_W, A_HEADS, B_W, B_W, C_W, KV_W, KV_W, KV_W, KV_W, KV_W, KV_W, 3 * C_HEADS)

LANES = 128
SUBLANES = 8
HEAD_PAD = 8
VMEM_LIMIT = 56 * 1024 * 1024
PAGES_PER_STEP = 16
CMP_PAGES_PER_STEP = 32

P_QA = (0, 384)
P_KVA = (384, 1152)
P_BU = (1152, 1408)
P_BV = (1408, 1664)
P_QC = (1664, 2048)
P_CMP = (2048, 2304)
P_SLC = (2304, 2560)
P_WIN = (2560, 2816)
P_MISC = (2816, 2944)
N_PROJ = 2944
GATE_OFF = A_HEADS


def _dot(a, b):
    return jnp.dot(a, b, preferred_element_type=F32)


def _dot_exact(a, b):
    return jnp.dot(a, b, preferred_element_type=F32, precision=HIGHEST)


def _dot_nt(a, b, precision=None):
    return lax.dot_general(a, b, (((1,), (1,)), ((), ())), preferred_element_type=F32, precision=precision)


def _gelu(x):
    return 0.5 * x * (1.0 + jnp.tanh(math.sqrt(2.0 / math.pi) * (x + 0.044715 * (x * x * x))))


def _sigmoid(x):
    return 1.0 / (1.0 + jnp.exp(-x))


def _rms(x, g):
    return x * lax.rsqrt(jnp.mean(x * x, axis=-1, keepdims=True) + EPS) * g


def _iota(shape, dim):
    return lax.broadcasted_iota(jnp.int32, shape, dim)


def _shr(x, n):
    return jnp.right_shift(x, int(math.log2(n)))


def _cparams(sem):
    return pltpu.CompilerParams(dimension_semantics=sem, vmem_limit_bytes=VMEM_LIMIT)


def _const_spec(shape):
    nd = len(shape)
    return pl.BlockSpec(shape, lambda *_: (0,) * nd)


def _proj_kernel(x_ref, g_ref, w_ref, bf_ref, lng_ref, lnb_ref,
                 qa_ref, kvab_ref, bu_ref, vn_ref, qc_ref, cmp_ref, kvcb_ref, misc_ref,
                 kva_ref=None, slc_ref=None, win_ref=None):
    h = _rms(x_ref[...], g_ref[...]).astype(BF16)

    def mm(seg):
        return _dot(h, w_ref[:, seg[0]:seg[1]])

    qa_ref[...] = (mm(P_QA) * SCALE).astype(BF16)
    kva = mm(P_KVA)
    if kva_ref is not None:
        kva_ref[...] = kva
    kvab_ref[...] = kva.astype(BF16)
    bu_ref[...] = _gelu(mm(P_BU))
    v = _gelu(mm(P_BV))
    mu = jnp.mean(v, axis=-1, keepdims=True)
    var = jnp.mean(jnp.square(v - mu), axis=-1, keepdims=True)
    vn_ref[...] = (v - mu) * lax.rsqrt(var + EPS) * lng_ref[...] + lnb_ref[...]
    qc_ref[...] = (mm(P_QC) * SCALE).astype(BF16)
    c = mm(P_CMP)
    cmp_ref[...] = c
    kvcb_ref[:, 0:256] = c.astype(BF16)
    c = mm(P_SLC)
    if slc_ref is not None:
        slc_ref[...] = c
    kvcb_ref[:, 256:512] = c.astype(BF16)
    c = mm(P_WIN)
    if win_ref is not None:
        win_ref[...] = c
    kvcb_ref[:, 512:768] = c.astype(BF16)
    m = mm(P_MISC) + bf_ref[...]
    lane = _iota(m.shape, 1)
    logsig = jnp.minimum(m, 0.0) - jnp.log1p(jnp.exp(-jnp.abs(m)))
    misc_ref[...] = jnp.where(lane < A_HEADS, logsig, _sigmoid(m))


def _proj(x2d, g, w, bf, lng, lnb, tm, state_rows):
    R = x2d.shape[0]
    row = lambda n: pl.BlockSpec((tm, n), lambda i: (i, 0))
    outs = [("qa", 384, BF16), ("kvab", 768, BF16), ("bu", 256, F32), ("vn", 256, F32), ("qc", 384, BF16),
            ("cmp", 256, F32), ("kvcb", 768, BF16), ("misc", 128, F32)]
    if state_rows:
        outs += [("kva", 768, F32), ("slc", 256, F32), ("win", 256, F32)]
    res = pl.pallas_call(
        _proj_kernel,
        grid=(R // tm,),
        in_specs=[row(D_MODEL), _const_spec((1, D_MODEL)), _const_spec((D_MODEL, N_PROJ)),
                  _const_spec((1, 128)), _const_spec((1, B_W)), _const_spec((1, B_W))],
        out_specs=[row(n) for _, n, _ in outs],
        out_shape=[jax.ShapeDtypeStruct((R, n), dt) for _, n, dt in outs],
        compiler_params=_cparams(("parallel",)),
        name="proj",
    )(x2d, g, w, bf, lng, lnb)
    return {name: r for (name, _, _), r in zip(outs, res)}


STATE_FEATS = (2 * A_W, 2 * KV_W, 2 * KV_W, 2 * KV_W)


def _state_kernel(x_ref, g_ref, wt_ref, *refs):
    outs = refs[-len(STATE_FEATS):]
    h = _rms(x_ref[...], g_ref[...]).astype(BF16)
    lo = 0
    for o_ref, n in zip(outs, STATE_FEATS):
        o_ref[...] = _dot_nt(wt_ref[lo:lo + n, :], h)
        lo += n


def _state_proj(layer, x2d, g, wt, prev, depth, B, S, tm):
    nblk = S // tm
    shapes = [jax.ShapeDtypeStruct((depth, B, n, S), F32) for n in STATE_FEATS]
    out_specs = [pl.BlockSpec((None, None, n, tm), lambda i: (layer, i // nblk, 0, i % nblk)) for n in STATE_FEATS]
    in_specs = [pl.BlockSpec((tm, D_MODEL), lambda i: (i, 0)), _const_spec((1, D_MODEL)), _const_spec(wt.shape)]
    args = [x2d, g, wt]
    aliases = {}
    if prev is not None:
        in_specs += [pl.BlockSpec(memory_space=pl.ANY)] * len(prev)
        aliases = {len(args) + k: k for k in range(len(prev))}
        args += list(prev)
    return pl.pallas_call(
        _state_kernel,
        grid=(B * nblk,),
        in_specs=in_specs,
        out_specs=out_specs,
        out_shape=shapes,
        input_output_aliases=aliases,
        compiler_params=_cparams(("arbitrary",)),
        name="state_proj",
    )(*args)


def _fox_kernel(q_ref, k_ref, v_ref, c_ref, o_ref, *, tq):
    qi = pl.program_id(2)
    tk = tq
    q = q_ref[...]
    lane = _iota((1, LANES), 1)
    low = lane < HEAD_DIM
    zero = jnp.zeros_like(q)
    qm = (jnp.where(low, q, zero), jnp.where(low, zero, q))
    one_lane = (HEAD_DIM, 0)

    def v_aug(v, hh):
        mine = low if hh == 0 else jnp.logical_not(low)
        ones = jnp.where(lane == one_lane[hh], 1.0, 0.0).astype(BF16)
        return jnp.where(mine, v, jnp.broadcast_to(ones, v.shape))

    def tile(ki, carry, causal):
        base = pl.multiple_of(ki * tk, tk)
        k = k_ref[pl.ds(base, tk), :]
        v = v_ref[pl.ds(base, tk), :]
        out = []
        for hh in range(2):
            m, acc = carry[hh]
            s = _dot_nt(qm[hh], k) - c_ref[hh:hh + 1, pl.ds(base, tk)]
            if causal is not None:
                s = jnp.where(causal, s, NEG)
            m_new = jnp.maximum(m, jnp.max(s, axis=-1, keepdims=True))
            alpha = jnp.exp2(m - m_new)
            p = jnp.exp2(s - m_new)
            acc = alpha * acc + _dot(p.astype(BF16), v_aug(v, hh))
            out.append((m_new, acc))
        return tuple(out)

    init = tuple((jnp.full((tq, 1), M_INIT, F32), jnp.zeros((tq, LANES), F32)) for _ in range(2))
    carry = lax.fori_loop(0, qi, lambda ki, c: tile(ki, c, None), init)
    causal = _iota((1, tk), 1) <= _iota((tq, 1), 0)
    (_, acc0), (_, acc1) = tile(qi, carry, causal)
    l0 = acc0[:, one_lane[0]:one_lane[0] + 1]
    l1 = acc1[:, one_lane[1]:one_lane[1] + 1]
    o_ref[...] = jnp.where(low, acc0 * (1.0 / l0), acc1 * (1.0 / l1))


def _fox_prompt(qa, kvab, crow, tq):
    B, S, _ = qa.shape
    half = A_HEADS // 2
    return pl.pallas_call(
        functools.partial(_fox_kernel, tq=tq),
        grid=(B, half, S // tq),
        in_specs=[
            pl.BlockSpec((None, tq, LANES), lambda b, p, qi: (b, qi, p)),
            pl.BlockSpec((None, S, LANES), lambda b, p, qi: (b, 0, p)),
            pl.BlockSpec((None, S, LANES), lambda b, p, qi: (b, 0, half + p)),
            pl.BlockSpec((None, None, 2, S), lambda b, p, qi: (b, p, 0, 0)),
        ],
        out_specs=pl.BlockSpec((None, tq, LANES), lambda b, p, qi: (b, qi, p)),
        out_shape=jax.ShapeDtypeStruct((B, S, A_W), F32),
        compiler_params=_cparams(("parallel", "parallel", "arbitrary")),
        name="fox_prompt",
    )(qa, kvab, kvab, crow)


def _cumsum_kernel(x_ref, o_ref):
    upper = jnp.where(_iota((LANES, LANES), 0) <= _iota((LANES, LANES), 1), 1.0, 0.0)
    carry = jnp.zeros((x_ref.shape[0], 1), F32)
    for i in range(x_ref.shape[1] // LANES):
        lanes = slice(i * LANES, (i + 1) * LANES)
        c = _dot_exact(x_ref[:, lanes], upper) + carry
        o_ref[:, lanes] = c * LOG2E
        carry = c[:, LANES - 1:LANES]


def _cumsum_lanes(x):
    n, h, length = x.shape
    spec = pl.BlockSpec((None, h, length), lambda i: (i, 0, 0))
    return pl.pallas_call(
        _cumsum_kernel, grid=(n,), in_specs=[spec], out_specs=spec,
        out_shape=jax.ShapeDtypeStruct(x.shape, F32),
        compiler_params=_cparams(("parallel",)),
        name="cumsum_logf",
    )(x)


def _gmlp_kernel(u_ref, vn_ref, ws_ref, bst_ref, o_ref, *, nchunk):
    tril = _iota((CHUNK, CHUNK), 1) <= _iota((CHUNK, CHUNK), 0)
    lane = _iota((1, B_W), 1)
    ws = [jnp.where(tril, ws_ref[h], 0.0).astype(BF16) for h in range(B_GROUPS)]
    for c in range(nchunk):
        rows = slice(c * CHUNK, (c + 1) * CHUNK)
        vn = vn_ref[rows, :].astype(BF16)
        z = jnp.zeros((CHUNK, B_W), F32)
        for h in range(B_GROUPS):
            zh = _dot(ws[h], vn) + bst_ref[:, h:h + 1]
            z = jnp.where(_shr(lane, HEAD_DIM) == h, zh, z)
        o_ref[rows, :] = u_ref[rows, :] * z


def _gmlp(u, vn, ws, bst, nchunk):
    R = u.shape[0]
    tm = nchunk * CHUNK
    row = pl.BlockSpec((tm, B_W), lambda i: (i, 0))
    return pl.pallas_call(
        functools.partial(_gmlp_kernel, nchunk=nchunk),
        grid=(R // tm,),
        in_specs=[row, row, _const_spec((B_GROUPS, CHUNK, CHUNK)), _const_spec((CHUNK, B_GROUPS))],
        out_specs=row,
        out_shape=jax.ShapeDtypeStruct((R, B_W), F32),
        compiler_params=_cparams(("parallel",)),
        name="gmlp",
    )(u, vn, ws, bst)


def _compress_core(x, w, pe_a, pe_b, b1, w2, b2):
    nc = x.shape[0]
    half = w.shape[1] // 2
    h = _dot(x, w)
    const = _dot(pe_a, w[:, :half]) + _dot(pe_b, w[:, half:])
    second = pltpu.roll(h[:, half:], nc - 1, 0)
    hid = _gelu(h[:, :half] + second + const[0:1, :] + b1)
    return _dot(hid.astype(BF16), w2) + b2


def _cmp_prompt_kernel(krows_ref, vrows_ref, wab_ref, pe_ref, b1_ref, w2_ref, b2_ref, o_ref, *, nc):
    for kv, rows_ref in enumerate((krows_ref, vrows_ref)):
        x = jnp.concatenate(
            [rows_ref[pl.ds(l, nc, stride=CMP_STRIDE), :] for l in range(CMP_STRIDE)], axis=-1).astype(BF16)
        o_ref[:, kv * KV_W:(kv + 1) * KV_W] = _compress_core(
            x, wab_ref[kv], pe_ref[kv, 0], pe_ref[kv, 1], b1_ref[kv], w2_ref[kv], b2_ref[kv]).astype(BF16)


def _compress_prompt(cmp_rows, cw):
    B, S, _ = cmp_rows.shape
    nc = S // CMP_STRIDE
    return pl.pallas_call(
        functools.partial(_cmp_prompt_kernel, nc=nc),
        grid=(B,),
        in_specs=[pl.BlockSpec((None, S, KV_W), lambda b: (b, 0, 0)), pl.BlockSpec((None, S, KV_W), lambda b: (b, 0, 1)),
                  _const_spec(cw["wab"].shape), _const_spec(cw["pe"].shape), _const_spec(cw["b1t"].shape),
                  _const_spec(cw["w2bd"].shape), _const_spec(cw["b2t"].shape)],
        out_specs=pl.BlockSpec((None, nc, 2 * KV_W), lambda b: (b, 0, 0)),
        out_shape=jax.ShapeDtypeStruct((B, nc, 2 * KV_W), BF16),
        compiler_params=_cparams(("parallel",)),
        name="compress_prompt",
    )(cmp_rows, cmp_rows, cw["wab"], cw["pe"], cw["b1t"], cw["w2bd"], cw["b2t"])


def _rank_select_t(score_t, topn):
    n = score_t.shape[0]
    groups = [score_t[lo:lo + SUBLANES] for lo in range(0, n, SUBLANES)]
    sub = _iota((SUBLANES, 1), 0)
    ranks = [jnp.zeros(g.shape, F32) for g in groups]
    for jp in range(n):
        row = score_t[jp:jp + 1, :]
        for gi, sg in enumerate(groups):
            lo = gi * SUBLANES
            if lo + SUBLANES - 1 <= jp:
                beats = row > sg
            elif lo > jp:
                beats = row >= sg
            else:
                beats = (row > sg) | ((row == sg) & (sub + lo > jp))
            ranks[gi] = ranks[gi] + jnp.where(beats, 1.0, 0.0)
    return jnp.concatenate([jnp.where(r < topn, 1.0, 0.0) for r in ranks], axis=0)


def _attend_tiles(qg, tiles, carry):
    m, acc = carry
    s_parts = []
    for k, _, bias, mask in tiles:
        s = _dot_nt(qg, k)
        if bias is not None:
            s = s + bias
        s_parts.append(jnp.where(mask, s, NEG))
    m_new = m
    for s in s_parts:
        m_new = jnp.maximum(m_new, jnp.max(s, axis=-1, keepdims=True))
    acc = jnp.exp2(m - m_new) * acc
    for s, (_, v, _, _) in zip(s_parts, tiles):
        acc = acc + _dot(jnp.exp2(s - m_new).astype(BF16), v)
    return m_new, acc


def _nsa_kernel(q_ref, kv_ref, cmp_ref, misc_ref, tb_ref, o_ref, *, nbp, nsb):
    qi = pl.program_id(1)
    tq = Q_BLOCK
    nrow = C_REP * tq
    t0 = qi * tq
    trow = t0 + _iota((tq, 1), 0)
    trow3 = jnp.concatenate([trow] * C_REP, axis=0)
    tl3 = trow3 - t0
    sl = _iota((1, tq), 1)
    causal3 = sl <= tl3
    topn = min(SLC_TOPN, nsb)
    lane = _iota((1, LANES), 1)
    low = lane < HEAD_DIM
    per = SLC_BLOCK // CMP_STRIDE
    pool_t = jnp.where(_shr(_iota((nsb, nbp), 1), per) == _iota((nsb, nbp), 0), 1.0, 0.0)
    groups = range(C_KV)
    mine = [low, jnp.logical_not(low)]
    one_lane = [HEAD_DIM, 0]
    ones = [jnp.where(lane == one_lane[g], 1.0, 0.0).astype(BF16) for g in groups]
    ks_lo, vs_lo, kw_lo, vw_lo = 2 * KV_W, 3 * KV_W, 4 * KV_W, 5 * KV_W

    def kv_tile(base, width, k_lo, v_lo):
        k = kv_ref[pl.ds(base, width), k_lo:k_lo + KV_W]
        v = kv_ref[pl.ds(base, width), v_lo:v_lo + KV_W]
        return k, [jnp.where(mine[g], v, jnp.broadcast_to(ones[g], v.shape)) for g in groups]

    qg, o_cmp, sel = [], [], []
    for g in groups:
        parts = []
        for r in range(C_REP):
            h = C_REP * g + r
            src = q_ref[:, (h // 2) * LANES:(h // 2 + 1) * LANES].astype(F32)
            if h % 2 != g:
                src = pltpu.roll(src, HEAD_DIM, 1)
            parts.append(jnp.where(mine[g], src, 0.0))
        q = jnp.concatenate(parts, axis=0).astype(BF16)
        qg.append(q)

        s = _dot_nt(q, cmp_ref[:, 0:KV_W])
        valid = (_iota((1, nbp), 1) * CMP_STRIDE + (CMP_LEN - 1)) <= trow3
        s = jnp.where(valid, s, NEG)
        e = jnp.exp2(s - jnp.max(s, axis=-1, keepdims=True))
        prob = jnp.where(valid, e * (1.0 / jnp.sum(e, axis=-1, keepdims=True)), 0.0)
        o_cmp.append(_dot(prob.astype(BF16), cmp_ref[:, KV_W:2 * KV_W]))

        psum = prob[0:tq] + prob[tq:2 * tq] + prob[2 * tq:3 * tq]
        imp_t = _dot_nt(pool_t, psum, HIGHEST)
        j = _iota((nsb, 1), 0)
        tcol = t0 + _iota((1, tq), 1)
        cur = _shr(tcol, SLC_BLOCK)
        forced = (j == 0) | (j == cur) | (j == cur - 1)
        score_t = jnp.where(forced, FORCE, jnp.where(j * SLC_BLOCK <= tcol, imp_t, -1.0))
        sel.append(_rank_select_t(score_t, topn).T.astype(BF16))

    def sel_masks(base, width):
        blk = base // SLC_BLOCK + _shr(_iota((nsb, width), 1), SLC_BLOCK)
        expand = jnp.where(_iota((nsb, width), 0) == blk, 1.0, 0.0).astype(BF16)
        return [jnp.concatenate([_dot(sel[g], expand)] * C_REP, axis=0) > 0.5 for g in groups]

    init = (jnp.full((nrow, 1), M_INIT, F32), jnp.zeros((nrow, LANES), F32))

    def far_step(base, width, carry):
        k, v = kv_tile(base, width, ks_lo, vs_lo)
        masks = sel_masks(base, width)
        return tuple(_attend_tiles(qg[g], [(k, v[g], None, masks[g])], carry[g]) for g in groups)

    n_far = jnp.maximum(qi - 1, 0)
    carry, done = (init, init), 0
    for span in (8, 4, 1):
        count = (n_far - done) // span
        carry = lax.fori_loop(
            0, count,
            lambda i, c, span=span, done=done: far_step(pl.multiple_of((done + i * span) * tq, tq), span * tq, c),
            carry)
        done = done + count * span

    pbase = pl.multiple_of(jnp.maximum(qi - 1, 0) * tq, tq)
    dbase = pl.multiple_of(t0, tq)
    has_prev = qi >= 1
    far_w = WINDOW - tq
    fbase = pl.multiple_of(jnp.maximum(t0 - WINDOW, 0), tq)
    kpos = fbase + _iota((1, far_w), 1)
    far_mask = (kpos < t0 - tq) & (trow3 - kpos <= WINDOW)
    mask_p = sel_masks(pbase, tq)
    mask_d = sel_masks(dbase, tq)
    ksp, vsp = kv_tile(pbase, tq, ks_lo, vs_lo)
    ksd, vsd = kv_tile(dbase, tq, ks_lo, vs_lo)
    kwf, vwf = kv_tile(fbase, far_w, kw_lo, vw_lo)
    kwp, vwp = kv_tile(pbase, tq, kw_lo, vw_lo)
    kwd, vwd = kv_tile(dbase, tq, kw_lo, vw_lo)
    heads = [None] * C_HEADS
    for g in groups:
        bias_diag = tb_ref[g, 0]
        bias_prev = tb_ref[g, 1]
        _, acc = _attend_tiles(qg[g], [(ksp, vsp[g], bias_prev, mask_p[g] & has_prev),
                                       (ksd, vsd[g], bias_diag, mask_d[g] & causal3)], carry[g])
        o_slc = acc * (1.0 / acc[:, one_lane[g]:one_lane[g] + 1])
        cw = _attend_tiles(qg[g], [(kwf, vwf[g], None, far_mask)], init)
        _, acc = _attend_tiles(qg[g], [(kwp, vwp[g], bias_prev, (tl3 >= 0) & has_prev),
                                       (kwd, vwd[g], bias_diag, causal3)], cw)
        o_win = acc * (1.0 / acc[:, one_lane[g]:one_lane[g] + 1])
        for r in range(C_REP):
            h = C_REP * g + r
            rows = slice(r * tq, (r + 1) * tq)
            gate = lambda br: misc_ref[:, GATE_OFF + br * C_HEADS + h:GATE_OFF + br * C_HEADS + h + 1]
            heads[h] = gate(0) * o_cmp[g][rows] + gate(1) * o_slc[rows] + gate(2) * o_win[rows]

    for pair in range(C_HEADS // 2):
        halves = []
        for h in (2 * pair, 2 * pair + 1):
            x = heads[h]
            if h // C_REP != h % 2:
                x = pltpu.roll(x, HEAD_DIM, 1)
            halves.append(x)
        o_ref[:, pair * LANES:(pair + 1) * LANES] = jnp.where(low, halves[0], halves[1])


def _nsa_prompt(qc, kvcb, kcv, misc, tb):
    B, S, _ = qc.shape
    nbp = kcv.shape[1]
    nsb = nbp * CMP_STRIDE // SLC_BLOCK
    return pl.pallas_call(
        functools.partial(_nsa_kernel, nbp=nbp, nsb=nsb),
        grid=(B, S // Q_BLOCK),
        in_specs=[pl.BlockSpec((None, Q_BLOCK, C_W), lambda b, i: (b, i, 0)),
                  pl.BlockSpec((None, S, 6 * KV_W), lambda b, i: (b, 0, 0)),
                  pl.BlockSpec((None, nbp, 2 * KV_W), lambda b, i: (b, 0, 0)),
                  pl.BlockSpec((None, Q_BLOCK, 128), lambda b, i: (b, i, 0)),
                  _const_spec(tb.shape)],
        out_specs=pl.BlockSpec((None, Q_BLOCK, C_W), lambda b, i: (b, i, 0)),
        out_shape=jax.ShapeDtypeStruct((B, S, C_W), F32),
        compiler_params=_cparams(("parallel", "arbitrary")),
        name="nsa_prompt",
    )(qc, kvcb, kcv, misc, tb)


def _mix_ffn_kernel(a_ref, b_ref, c_ref, x_ref, ga_ref, gb_ref, gc_ref, wo_ref, gpm_ref, gpf_ref,
                    win_ref, wout_ref, gpo_ref, o_ref, *, nsplit):
    an = _rms(a_ref[...], ga_ref[...]).astype(BF16)
    bn = _rms(b_ref[...], gb_ref[...]).astype(BF16)
    cn = _rms(c_ref[...], gc_ref[...]).astype(BF16)
    mix = (_dot(an, wo_ref[0:A_W, :]) + _dot(bn, wo_ref[A_W:A_W + B_W, :])
           + _dot(cn, wo_ref[A_W + B_W:A_W + B_W + C_W, :]))
    x1 = x_ref[...] + _rms(mix, gpm_ref[...])
    h = _rms(x1, gpf_ref[...]).astype(BF16)
    wid = D_FF // nsplit
    y = jnp.zeros(x1.shape, F32)
    for c in range(nsplit):
        lo = c * wid
        gate = _dot(h, win_ref[:, lo:lo + wid])
        up = _dot(h, win_ref[:, D_FF + lo:D_FF + lo + wid])
        act = gate * _sigmoid(gate) * up
        y = y + _dot(act.astype(BF16), wout_ref[lo:lo + wid, :])
    o_ref[...] = x1 + _rms(y, gpo_ref[...])


def _mix_ffn(a, b, c, x, lw, tm):
    R = x.shape[0]
    row = lambda n: pl.BlockSpec((tm, n), lambda i: (i, 0))
    once = lambda shape: pl.BlockSpec(shape, lambda i: (0, 0), pipeline_mode=pl.Buffered(1))
    return pl.pallas_call(
        functools.partial(_mix_ffn_kernel, nsplit=2),
        grid=(R // tm,),
        in_specs=[row(A_W), row(B_W), row(C_W), row(D_MODEL),
                  once((1, A_W)), once((1, B_W)), once((1, C_W)), once((D_MODEL, D_MODEL)),
                  once((1, D_MODEL)), once((1, D_MODEL)), once((D_MODEL, 2 * D_FF)), once((D_FF, D_MODEL)),
                  once((1, D_MODEL))],
        out_specs=row(D_MODEL),
        out_shape=jax.ShapeDtypeStruct((R, D_MODEL), F32),
        compiler_params=_cparams(("parallel",)),
        name="mix_ffn",
    )(a, b, c, x, lw["ga"], lw["gb"], lw["gc"], lw["w_o"], lw["gpm"], lw["gpf"], lw["w_ffn_in"],
      lw["w_ffn_out"], lw["gpo"])


def _head_block(x, h):
    return x[h * HEAD_DIM:(h + 1) * HEAD_DIM]


def _pad_heads(x):
    return jnp.concatenate([x, jnp.zeros((HEAD_PAD - x.shape[0], x.shape[1]), F32)], axis=0)


def _row_softmax_step(s, m_prev):
    m_new = jnp.maximum(m_prev, jnp.max(s, axis=1, keepdims=True))
    return m_new, jnp.exp2(m_prev - m_new), jnp.exp2(s - m_new)


def _fox_dec_kernel(pt_ref, q_ref, *refs, npp):
    pages = refs[:npp]
    lfs = refs[npp:2 * npp]
    new_ref, newlf_ref, o_ref, m_sc, l_sc, c_sc, acc_sc = refs[2 * npp:]
    j = pl.program_id(1)
    page = pages[0].shape[1]
    q = q_ref[...]
    upper = jnp.where(_iota((page, page), 0) <= _iota((page, page), 1), 1.0, 0.0)

    @pl.when(j == 0)
    def _():
        m_sc[...] = jnp.full(m_sc.shape, M_INIT, F32)
        l_sc[...] = jnp.zeros(l_sc.shape, F32)
        c_sc[...] = jnp.zeros(c_sc.shape, F32)
        acc_sc[...] = jnp.zeros(acc_sc.shape, F32)

    def scores(kt):
        prod = kt * q
        return _pad_heads(jnp.concatenate(
            [jnp.sum(_head_block(prod, h), axis=0, keepdims=True) for h in range(A_HEADS)], axis=0))

    def accumulate(alpha, p_tiles, v_tiles):
        for h in range(A_HEADS):
            a = _head_block(acc_sc, h) * alpha[h:h + 1, :]
            for p, v in zip(p_tiles, v_tiles):
                a = a + p[h:h + 1, :] * v[A_W + h * HEAD_DIM:A_W + (h + 1) * HEAD_DIM, :]
            acc_sc[h * HEAD_DIM:(h + 1) * HEAD_DIM, :] = a

    c_local = _dot_exact(jnp.concatenate([lf[...] for lf in lfs], axis=0), upper)
    totals = [jnp.broadcast_to(c_local[i * HEAD_PAD:(i + 1) * HEAD_PAD, page - 1:page], (HEAD_PAD, page))
              for i in range(npp)]
    before = jnp.broadcast_to(c_sc[...], (HEAD_PAD, page))
    s_parts = []
    for i, pg in enumerate(pages):
        s_parts.append(scores(pg[0:A_W, :]) - LOG2E * (c_local[i * HEAD_PAD:(i + 1) * HEAD_PAD] + before))
        before = before + totals[i]
    c_sc[...] = before[:, 0:1]
    m_new, alpha, p = _row_softmax_step(jnp.concatenate(s_parts, axis=1), m_sc[...])
    l_sc[...] = alpha * l_sc[...] + jnp.sum(p, axis=1, keepdims=True)
    m_sc[...] = m_new
    accumulate(alpha, [p[:, i * page:(i + 1) * page] for i in range(npp)], pages)

    @pl.when(j == pl.num_programs(1) - 1)
    def _():
        first = _iota((1, page), 1) == 0
        s_new = jnp.where(first, scores(new_ref[0:A_W, :]) - LOG2E * (c_sc[...] + newlf_ref[...]), NEG)
        m_new, alpha, p_new = _row_softmax_step(s_new, m_sc[...])
        l_inv = 1.0 / (alpha * l_sc[...] + jnp.sum(p_new, axis=1, keepdims=True))
        accumulate(alpha, [p_new], [new_ref])
        for h in range(A_HEADS):
            rows = slice(h * HEAD_DIM, (h + 1) * HEAD_DIM)
            o_ref[rows, :] = jnp.sum(acc_sc[rows, :], axis=1, keepdims=True) * l_inv[h:h + 1, :]


def _fox_decode(layer, page_table, q_rep, pool_t, logf_t, new_t, new_lf, npp):
    nb, npages = page_table.shape
    page = pool_t.shape[3]
    page_of = lambda b, j, i, pt: pt[b, j * npp + i]
    page_spec = lambda i: pl.BlockSpec((None, None, 2 * A_W, page), lambda b, j, pt: (layer, page_of(b, j, i, pt), 0, 0))
    logf_spec = lambda i: pl.BlockSpec((None, None, HEAD_PAD, page), lambda b, j, pt: (layer, page_of(b, j, i, pt), 0, 0))
    per_b = lambda shape: pl.BlockSpec((None,) + shape, lambda b, j, pt: (b,) + (0,) * len(shape))
    grid_spec = pltpu.PrefetchScalarGridSpec(
        num_scalar_prefetch=1,
        grid=(nb, npages // npp),
        in_specs=[per_b((A_W, page))] + [page_spec(i) for i in range(npp)] + [logf_spec(i) for i in range(npp)]
        + [per_b((2 * A_W, page)), per_b((HEAD_PAD, 1))],
        out_specs=per_b((A_W, 1)),
        scratch_shapes=[pltpu.VMEM((HEAD_PAD, 1), F32), pltpu.VMEM((HEAD_PAD, 1), F32), pltpu.VMEM((HEAD_PAD, 1), F32),
                        pltpu.VMEM((A_W, page), F32)],
    )
    return pl.pallas_call(
        functools.partial(_fox_dec_kernel, npp=npp),
        grid_spec=grid_spec,
        out_shape=jax.ShapeDtypeStruct((nb, A_W, 1), F32),
        compiler_params=_cparams(("parallel", "arbitrary")),
        name="fox_decode",
    )(page_table, q_rep, *([pool_t] * npp), *([logf_t] * npp), new_t, new_lf)


def _cmp_dec_kernel(pt_ref, *refs, npp, nc, nsb_all):
    pages = refs[:npp]
    w_ref, pe_ref, b1_ref, w2_ref, b2_ref, q_ref, ocmp_ref, sel_ref, x_sc = refs[npp:]
    j = pl.program_id(1)
    per_page = pages[0].shape[1]
    for i, pg in enumerate(pages):
        row0 = pl.multiple_of((j * npp + i) * per_page, per_page)
        for c in range(2 * C_KV):
            x_sc[c, pl.ds(row0, per_page), :] = pg[c]

    @pl.when(j == pl.num_programs(1) - 1)
    def _():
        kc = [[_compress_core(x_sc[kv * C_KV + g].astype(BF16), w_ref[kv], pe_ref[kv, 0], pe_ref[kv, 1],
                              b1_ref[kv], w2_ref[kv], b2_ref[kv]).astype(BF16)
               for g in range(C_KV)] for kv in range(2)]
        q = q_ref[...]
        grp0 = _iota((HEAD_PAD, 1), 0) < C_REP
        pick = lambda x0, x1: jnp.where(grp0, x0, x1)
        s = pick(_dot_nt(q, kc[0][0]), _dot_nt(q, kc[0][1]))
        valid = _iota((1, nc), 1) < nc - 1
        s = jnp.where(valid, s, NEG)
        e = jnp.exp2(s - jnp.max(s, axis=-1, keepdims=True))
        prob = jnp.where(valid, e / jnp.sum(e, axis=-1, keepdims=True), 0.0)
        pb = prob.astype(BF16)
        ocmp_ref[...] = pick(_dot(pb, kc[1][0]), _dot(pb, kc[1][1]))

        per = SLC_BLOCK // CMP_STRIDE
        nsbp = sel_ref.shape[1]
        hrow = _iota((HEAD_PAD, 1), 0)
        psum = jnp.concatenate(
            [jnp.sum(jnp.where((hrow >= C_REP * g) & (hrow < C_REP * (g + 1)), prob, 0.0), axis=0, keepdims=True)
             for g in range(C_KV)] + [jnp.zeros((HEAD_PAD - C_KV, nc), F32)], axis=0)
        pool = jnp.where(_shr(_iota((nc, nsbp), 0), per) == _iota((nc, nsbp), 1), 1.0, 0.0)
        imp = _dot_exact(psum, pool)
        jl = _iota((1, nsbp), 1)
        cur = nsb_all - 1
        forced = (jl == 0) | (jl == cur) | (jl == cur - 1)
        score = jnp.where(forced, FORCE, jnp.where(jl <= cur, imp, -2.0))
        topn = min(SLC_TOPN, nsb_all)
        rank = jnp.zeros(score.shape, F32)
        for jp in range(nsb_all):
            col = score[:, jp:jp + 1]
            beats = (col > score) | ((col == score) & (jp < jl))
            rank = rank + jnp.where(beats, 1.0, 0.0)
        sel_ref[...] = jnp.where(rank < topn, 1.0, 0.0)


def _cmp_decode(layer, page_table, pool, cw, qh, npp):
    nb, npages = page_table.shape
    per_page, width = pool.shape[3], pool.shape[4]
    nc = npages * per_page
    nsb_all = nc * CMP_STRIDE // SLC_BLOCK + 1
    nsbp = -(-nsb_all // LANES) * LANES

    def page_spec(i):
        return pl.BlockSpec((None, None, 2 * C_KV, per_page, width),
                            lambda b, j, pt: (layer, pt[b, j * npp + i], 0, 0, 0))

    const = lambda a: pl.BlockSpec(a.shape, lambda b, j, pt: (0,) * a.ndim)
    grid_spec = pltpu.PrefetchScalarGridSpec(
        num_scalar_prefetch=1,
        grid=(nb, npages // npp),
        in_specs=[page_spec(i) for i in range(npp)]
        + [const(cw["w"]), const(cw["pe"]), const(cw["b1"]), const(cw["w2"]), const(cw["b2"]),
           pl.BlockSpec((None, HEAD_PAD, HEAD_DIM), lambda b, j, pt: (b, 0, 0))],
        out_specs=[pl.BlockSpec((None, HEAD_PAD, HEAD_DIM), lambda b, j, pt: (b, 0, 0)),
                   pl.BlockSpec((None, HEAD_PAD, nsbp), lambda b, j, pt: (b, 0, 0))],
        scratch_shapes=[pltpu.VMEM((2 * C_KV, nc, width), F32)],
    )
    return pl.pallas_call(
        functools.partial(_cmp_dec_kernel, npp=npp, nc=nc, nsb_all=nsb_all),
        grid_spec=grid_spec,
        out_shape=[jax.ShapeDtypeStruct((nb, HEAD_PAD, HEAD_DIM), F32),
                   jax.ShapeDtypeStruct((nb, HEAD_PAD, nsbp), F32)],
        compiler_params=_cparams(("parallel", "arbitrary")),
        name="cmp_decode",
    )(page_table, *([pool] * npp), cw["w"], cw["pe"], cw["b1"], cw["w2"], cw["b2"], qh)


def _slc_dec_kernel(plist_ref, lpage_ref, cnt_ref, flag_ref, q_ref, *refs, npp, nslot, nflag, last_page):
    pages = refs[:npp]
    (nbias_ref, fbias_ref, win_ref, wbias_ref, b0_ref, new_ref, gate_ref, ocmp_ref, o_ref,
     m_sc, l_sc, acc_sc) = refs[npp:]
    b = pl.program_id(0)
    j = pl.program_id(1)
    page = pages[0].shape[1]
    q = q_ref[...]
    grp0 = _iota((HEAD_PAD, 1), 0) < C_REP
    lane = _iota((1, page), 1)

    def scores(kt):
        return _pad_heads(jnp.concatenate(
            [jnp.sum(_head_block(kt, h // C_REP) * _head_block(q, h), axis=0, keepdims=True)
             for h in range(C_HEADS)], axis=0))

    def weighted(acc, alpha, p_tiles, v_tiles):
        out = []
        for h in range(C_HEADS):
            a = _head_block(acc, h) * alpha[h:h + 1, :]
            for p, v in zip(p_tiles, v_tiles):
                a = a + p[h:h + 1, :] * _head_block(v, h // C_REP)
            out.append(a)
        return jnp.concatenate(out, axis=0)

    def finish(acc, l):
        l_inv = 1.0 / l
        return jnp.concatenate(
            [jnp.sum(_head_block(acc, h), axis=1, keepdims=True) * l_inv[h:h + 1, :] for h in range(C_HEADS)], axis=0)

    @pl.when(j == 0)
    def _():
        m_sc[...] = jnp.full(m_sc.shape, M_INIT, F32)
        l_sc[...] = jnp.zeros(l_sc.shape, F32)
        acc_sc[...] = jnp.zeros(acc_sc.shape, F32)

    nblk = page // SLC_BLOCK
    s_parts = []
    for i, pg in enumerate(pages):
        slot = j * npp + i
        lp = lpage_ref[b * nslot + slot]
        live = slot < cnt_ref[b]
        mask = jnp.zeros((HEAD_PAD, page), jnp.int32)
        for blk in range(nblk):
            f = [flag_ref[(b * C_KV + g) * nflag + lp * nblk + blk] for g in range(C_KV)]
            mask = jnp.where(_shr(lane, SLC_BLOCK) == blk, jnp.where(grp0, f[0], f[1]), mask)
        mask = (mask > 0) & live
        bias = jnp.where(lp == last_page, nbias_ref[...], fbias_ref[...])
        s_parts.append(jnp.where(mask, scores(pg[0:KV_W, :]) + bias, NEG))
    m_new, alpha, p = _row_softmax_step(jnp.concatenate(s_parts, axis=1), m_sc[...])
    l_sc[...] = alpha * l_sc[...] + jnp.sum(p, axis=1, keepdims=True)
    m_sc[...] = m_new
    acc_sc[...] = weighted(acc_sc[...], alpha, [p[:, i * page:(i + 1) * page] for i in range(npp)],
                           [pg[KV_W:2 * KV_W, :] for pg in pages])

    @pl.when(j == pl.num_programs(1) - 1)
    def _():
        first = lane == 0
        s_new = jnp.where(first, scores(new_ref[0:KV_W, :]) + b0_ref[...], NEG)
        m_new, alpha, p_new = _row_softmax_step(s_new, m_sc[...])
        l = alpha * l_sc[...] + jnp.sum(p_new, axis=1, keepdims=True)
        o_slc = finish(weighted(acc_sc[...], alpha, [p_new], [new_ref[KV_W:2 * KV_W, :]]), l)
        nwin = win_ref.shape[1] // page
        tiles = lambda ref, lo: [ref[lo:lo + KV_W, c * page:(c + 1) * page] for c in range(nwin)]
        s_win = [scores(kt) + wbias_ref[:, c * page:(c + 1) * page] for c, kt in enumerate(tiles(win_ref, 0))]
        s_win.append(jnp.where(first, scores(new_ref[2 * KV_W:3 * KV_W, :]) + b0_ref[...], NEG))
        s = jnp.concatenate(s_win, axis=1)
        p = jnp.exp2(s - jnp.max(s, axis=1, keepdims=True))
        acc = weighted(jnp.zeros((C_W, page), F32), jnp.zeros((HEAD_PAD, 1), F32),
                       [p[:, c * page:(c + 1) * page] for c in range(nwin + 1)],
                       tiles(win_ref, KV_W) + [new_ref[3 * KV_W:4 * KV_W, :]])
        o_win = finish(acc, jnp.sum(p, axis=1, keepdims=True))
        o_ref[...] = gate_ref[:, 0:1] * ocmp_ref[...] + gate_ref[:, 1:2] * o_slc + gate_ref[:, 2:3] * o_win


def _slc_decode(layer, plist, lpage, cnt, flags, nflag, q_rep, pool_t, nbias, fbias, win_t, wbias, b0, new_t, gates,
                ocmp, npp, nslot, last_page):
    nb = q_rep.shape[0]
    page = pool_t.shape[3]
    wb = win_t.shape[3]

    def page_spec(i):
        return pl.BlockSpec((None, None, 2 * KV_W, page),
                            lambda b, j, pls, lps, cn, fl: (layer, pls[b * nslot + j * npp + i], 0, 0))

    fixed = lambda shape: pl.BlockSpec(shape, lambda b, j, pls, lps, cn, fl: (0,) * len(shape))
    per_b = lambda shape: pl.BlockSpec((None,) + shape, lambda b, j, pls, lps, cn, fl: (b,) + (0,) * len(shape))
    grid_spec = pltpu.PrefetchScalarGridSpec(
        num_scalar_prefetch=4,
        grid=(nb, nslot // npp),
        in_specs=[per_b((C_W, page))] + [page_spec(i) for i in range(npp)]
        + [fixed((HEAD_PAD, page)), fixed((HEAD_PAD, 1)),
           pl.BlockSpec((None, None, 2 * KV_W, wb), lambda b, j, pls, lps, cn, fl: (layer, b, 0, 0)),
           fixed((HEAD_PAD, wb)), fixed((HEAD_PAD, 1)), per_b((4 * KV_W, page)), per_b((C_W, 3)), per_b((C_W, 1))],
        out_specs=per_b((C_W, 1)),
        scratch_shapes=[pltpu.VMEM((HEAD_PAD, 1), F32), pltpu.VMEM((HEAD_PAD, 1), F32), pltpu.VMEM((C_W, page), F32)],
    )
    return pl.pallas_call(
        functools.partial(_slc_dec_kernel, npp=npp, nslot=nslot, nflag=nflag, last_page=last_page),
        grid_spec=grid_spec,
        out_shape=jax.ShapeDtypeStruct((nb, C_W, 1), F32),
        compiler_params=_cparams(("parallel", "arbitrary")),
        name="slc_win_decode",
    )(plist, lpage, cnt, flags, q_rep, *([pool_t] * npp), nbias, fbias, win_t, wbias, b0, new_t, gates, ocmp)


def _t5_bucket(dist):
    n = jnp.maximum(dist, 0)
    max_exact = NUM_BUCKETS // 2
    nf = jnp.maximum(n, 1).astype(F32)
    large = max_exact + (jnp.log(nf / max_exact) / math.log(MAX_DISTANCE / max_exact)
                         * (NUM_BUCKETS - max_exact)).astype(jnp.int32)
    return jnp.where(n < max_exact, n, jnp.minimum(large, NUM_BUCKETS - 1))


def _prep_w_in(w_in):
    splits = [int(s) for s in np.cumsum(PROJ_WIDTHS)[:-1]]
    a_q, a_k, a_v, a_f, b_u, b_v, c_q, c_kc, c_vc, c_ks, c_vs, c_kw, c_vw, c_g = jnp.split(w_in, splits, axis=-1)
    pad = jnp.zeros(w_in.shape[:-1] + (LANES - A_HEADS - 3 * C_HEADS,), w_in.dtype)
    return jnp.concatenate([a_q, a_k, a_v, b_u, b_v, c_q, c_kc, c_vc, c_ks, c_vs, c_kw, c_vw, a_f, c_g, pad],
                           axis=-1).astype(BF16)


def _prep_w_state(w_in):
    splits = [int(s) for s in np.cumsum(PROJ_WIDTHS)[:-1]]
    _, a_k, a_v, _, _, _, _, c_kc, c_vc, c_ks, c_vs, c_kw, c_vw, _ = jnp.split(w_in, splits, axis=-1)
    return jnp.concatenate([a_k, a_v, c_kc, c_vc, c_ks, c_vs, c_kw, c_vw], axis=-1).transpose(0, 2, 1).astype(BF16)


def _prep_compress(cmp_pe, cmp_w1, cmp_b1, cmp_w2, cmp_b2):
    depth = cmp_w1.shape[0]
    half = CMP_LEN // 2
    eye = jnp.eye(C_KV, dtype=F32)
    w1 = cmp_w1.reshape(depth, 2, 2, half, HEAD_DIM, CMP_HIDDEN)
    wab = jnp.einsum("zkhldj,gG->zklgdhGj", w1, eye).reshape(depth, 2, half * KV_W, 2 * C_KV * CMP_HIDDEN)
    pe = cmp_pe.reshape(depth, 2, 2, half, 1, HEAD_DIM)
    pe_bd = jnp.broadcast_to(pe, (depth, 2, 2, half, C_KV, HEAD_DIM)).reshape(depth, 2, 2, 1, half * KV_W)
    pe_bd = jnp.broadcast_to(pe_bd, (depth, 2, 2, SUBLANES, half * KV_W))
    w2bd = jnp.einsum("zkjd,gG->zkgjGd", cmp_w2, eye).reshape(depth, 2, C_KV * CMP_HIDDEN, KV_W)
    w_g = w1.transpose(0, 1, 3, 4, 2, 5).reshape(depth, 2, half * HEAD_DIM, 2 * CMP_HIDDEN)
    pe_g = jnp.broadcast_to(cmp_pe.reshape(depth, 2, 2, 1, half * HEAD_DIM), (depth, 2, 2, SUBLANES, half * HEAD_DIM))
    prompt = {
        "wab": wab.astype(BF16), "pe": pe_bd.astype(BF16),
        "b1t": jnp.tile(cmp_b1, (1, 1, C_KV))[:, :, None, :],
        "w2bd": w2bd.astype(BF16),
        "b2t": jnp.tile(cmp_b2, (1, 1, C_KV))[:, :, None, :],
    }
    sample = {"w": w_g.astype(BF16), "pe": pe_g.astype(BF16), "b1": cmp_b1[:, :, None, :],
              "w2": cmp_w2.astype(BF16), "b2": cmp_b2[:, :, None, :]}
    return prompt, sample


def _prompt_bias_tiles(table):
    tl = jnp.arange(Q_BLOCK)[:, None]
    sl = jnp.arange(Q_BLOCK)[None, :]
    d = tl - sl
    far = table[_t5_bucket(jnp.asarray(8 * MAX_DISTANCE))]
    tiles = []
    for delta in (0, Q_BLOCK):
        b = table[_t5_bucket(d + delta)] - far
        tiles.append(b.transpose(2, 0, 1).reshape(C_KV, C_REP * Q_BLOCK, Q_BLOCK))
    return (jnp.stack(tiles, axis=1) * LOG2E).astype(F32)


def _sample_bias(table, page, wb):
    rows = lambda x: jnp.pad(x.T * LOG2E, ((0, HEAD_PAD - C_HEADS), (0, 0))).astype(F32)
    nbias = rows(table[_t5_bucket(page - jnp.arange(page))])
    fbias = rows(table[_t5_bucket(jnp.asarray([8 * MAX_DISTANCE]))])
    wbias = rows(table[_t5_bucket(wb - jnp.arange(wb))])
    b0 = rows(table[0:1])
    return nbias, fbias, wbias, b0


def _lane_repeat(x, n):
    return jnp.broadcast_to(x.astype(F32)[:, :, None], x.shape + (n,))


def _lane_zero(x, n):
    return jnp.pad(x.astype(F32)[:, :, None], ((0, 0), (0, 0), (0, n - 1)))


def _layer_prompt(layer, depth, x, lw, cw, tb, states, tm_proj, tm_mix, tq_fox):
    B, S, _ = x.shape
    R = B * S
    x2 = x.reshape(R, D_MODEL)
    p = _proj(x2, lw["g_pre"], lw["w_in"], lw["bf"], lw["lng"], lw["lnb"], tm_proj, state_rows=False)
    states = _state_proj(layer, x2, lw["g_pre"], lw["w_state"], states, depth, B, S, tm_proj)
    lf = p["misc"][:, :A_HEADS].reshape(B, S, A_HEADS)
    lf_t = jnp.pad(lf, ((0, 0), (0, 0), (0, HEAD_PAD - A_HEADS))).transpose(0, 2, 1)
    crow = _cumsum_lanes(lf_t)[:, :A_HEADS].reshape(B, A_HEADS // 2, 2, S)
    a_out = _fox_prompt(p["qa"].reshape(B, S, A_W), p["kvab"].reshape(B, S, 2 * A_W), crow, tq_fox)
    b_out = _gmlp(p["bu"], p["vn"], lw["ws"], lw["bst"], min(8, R // CHUNK))
    kcv = _compress_prompt(p["cmp"].reshape(B, S, 2 * KV_W), cw)
    c_out = _nsa_prompt(p["qc"].reshape(B, S, C_W), p["kvcb"].reshape(B, S, 6 * KV_W), kcv,
                        p["misc"].reshape(B, S, 128), tb)
    y = _mix_ffn(a_out.reshape(R, A_W), b_out, c_out.reshape(R, C_W), x2, lw, tm_mix)
    return y.reshape(B, S, D_MODEL), lf, states


def _layer_sample(layer, x, lw, cw, sb, pools, page_table):
    nb = x.shape[0]
    fox_t, logf_t, cmp_pool, slc_t, win_t = pools
    npages = page_table.shape[1]
    page = fox_t.shape[3]
    past = npages * page
    npp = min(PAGES_PER_STEP, npages)
    x2 = x.reshape(nb, D_MODEL)
    p = _proj(x2, lw["g_pre"], lw["w_in"], lw["bf"], lw["lng"], lw["lnb"], nb, state_rows=True)
    qa, kva, bu, vn, qc, misc = p["qa"], p["kva"], p["bu"], p["vn"], p["qc"], p["misc"]
    cmp_r, slc_r, win_r = p["cmp"], p["slc"], p["win"]
    lf = misc[:, :A_HEADS]

    new_lf = jnp.pad(lf, ((0, 0), (0, HEAD_PAD - A_HEADS)))[:, :, None]
    a_out = _fox_decode(layer, page_table, _lane_repeat(qa, page), fox_t, logf_t, _lane_zero(kva, page), new_lf,
                        npp).reshape(nb, A_W)

    pad_chunk = lambda t: jnp.zeros((nb, CHUNK, B_W), F32).at[:, 0].set(t).reshape(nb * CHUNK, B_W)
    b_out = _gmlp(pad_chunk(bu), pad_chunk(vn), lw["ws"], lw["bst"], min(8, nb)).reshape(nb, CHUNK, B_W)[:, 0]

    qh = jnp.pad(qc.reshape(nb, C_HEADS, HEAD_DIM), ((0, 0), (0, HEAD_PAD - C_HEADS), (0, 0)))
    o_cmp, sel = _cmp_decode(layer, page_table, cmp_pool, cw, qh, min(CMP_PAGES_PER_STEP, npages))
    nsb_all = past // SLC_BLOCK + 1
    per_page = page // SLC_BLOCK
    nflag = (npages + 1) * per_page
    flags = jnp.pad(sel[:, :C_KV, :nsb_all] > 0.5, ((0, 0), (0, 0), (0, nflag - nsb_all)))
    need = flags[:, :, :npages * per_page].reshape(nb, C_KV, npages, per_page).any(axis=(1, 3))
    nslot = -(-min(npages, C_KV * SLC_TOPN) // npp) * npp
    pidx = jnp.arange(npages, dtype=jnp.int32)[None, :]
    order = jnp.argsort(jnp.where(need, pidx, pidx + npages), axis=1)[:, :nslot]
    cnt = jnp.sum(need, axis=1).astype(jnp.int32)
    last = jnp.take_along_axis(order, jnp.maximum(cnt - 1, 0)[:, None], axis=1)
    lpage = jnp.where(jnp.arange(nslot)[None, :] < cnt[:, None], order, last).astype(jnp.int32)
    plist = jnp.take_along_axis(page_table, lpage, axis=1)
    g3 = misc[:, GATE_OFF:GATE_OFF + 3 * C_HEADS].reshape(nb, 3, C_HEADS).transpose(0, 2, 1)
    gates = jnp.repeat(g3, HEAD_DIM, axis=1)
    nbias, fbias, wbias, b0 = sb
    new_t = _lane_zero(jnp.concatenate([slc_r, win_r], axis=1), page)
    c_out = _slc_decode(layer, plist.reshape(-1), lpage.reshape(-1), cnt, flags.reshape(-1).astype(jnp.int32), nflag,
                        _lane_repeat(qc, page), slc_t, nbias, fbias, win_t, wbias, b0, new_t, gates,
                        o_cmp[:, :C_HEADS].reshape(nb, C_W, 1), npp, nslot, npages - 1).reshape(nb, C_W)

    y = _mix_ffn(a_out, b_out, c_out, x2, lw, nb)
    kv5 = lambda r: r.reshape(nb, 1, 2, C_KV, HEAD_DIM)
    states = (kva.reshape(nb, 1, 2, A_HEADS, HEAD_DIM), lf[:, None, :], kv5(cmp_r), kv5(slc_r), kv5(win_r),
              vn.reshape(nb, 1, B_GROUPS, HEAD_DIM))
    return y.reshape(nb, 1, D_MODEL), states


def kernel(x_prompt, x_sample, cache_fox_kv, cache_fox_logf, cache_cmp_kv, cache_slc_kv, state_win_kv, page_table, rel_bias_table, norm_pre_mix, w_in, b_forget, gmlp_ln_g, gmlp_ln_b, gmlp_ws, gmlp_bs, cmp_pe, cmp_w1, cmp_b1, cmp_w2, cmp_b2, norm_group_a, norm_group_b, norm_group_c, w_o, norm_post_mix, norm_pre_ffn, w_ffn_in, w_ffn_out, norm_post_ffn):
    depth = w_in.shape[0]
    assert x_sample.shape[1] == 1
    B, S, _ = x_prompt.shape
    nb = x_sample.shape[0]
    n_pool, page = cache_fox_kv.shape[1], cache_fox_kv.shape[2]
    wb = state_win_kv.shape[2]

    w_in_p = _prep_w_in(w_in)
    bf = jnp.pad(b_forget, ((0, 0), (0, LANES - A_HEADS)))[:, None, :]
    cw_prompt, cw_sample = _prep_compress(cmp_pe, cmp_w1, cmp_b1, cmp_w2, cmp_b2)
    tb = _prompt_bias_tiles(rel_bias_table)
    sb = _sample_bias(rel_bias_table, page, wb)
    w_o_b = w_o.astype(BF16)
    w_fi_b = w_ffn_in.astype(BF16)
    w_fo_b = w_ffn_out.astype(BF16)
    bst = gmlp_bs.transpose(0, 2, 1)
    feat_pos = lambda a: jnp.moveaxis(a, 2, -1).reshape(a.shape[0], a.shape[1], -1, a.shape[2])
    chunks = page // CMP_STRIDE
    cmp_x = cache_cmp_kv.reshape(depth, n_pool, chunks, CMP_STRIDE, 2, C_KV, HEAD_DIM)
    cmp_x = cmp_x.transpose(0, 1, 4, 5, 2, 3, 6).reshape(depth, n_pool, 2 * C_KV, chunks, CMP_STRIDE * HEAD_DIM)
    logf_t = jnp.pad(cache_fox_logf.transpose(0, 1, 3, 2), ((0, 0), (0, 0), (0, HEAD_PAD - A_HEADS), (0, 0)))
    pools = (feat_pos(cache_fox_kv), logf_t, cmp_x, feat_pos(cache_slc_kv), feat_pos(state_win_kv))

    tm_proj = min(512, B * S)
    tm_mix = min(256, B * S)
    tq_fox = min(512, S)

    w_state = _prep_w_state(w_in)
    xp, xs = x_prompt, x_sample
    prompt_states = None
    logf_p = []
    ss = [[] for _ in range(6)]
    for l in range(depth):
        r1 = lambda a: a[l][None, :]
        lw = {
            "g_pre": r1(norm_pre_mix), "w_in": w_in_p[l], "w_state": w_state[l], "bf": bf[l], "lng": r1(gmlp_ln_g),
            "lnb": r1(gmlp_ln_b), "ws": gmlp_ws[l], "bst": bst[l], "ga": r1(norm_group_a), "gb": r1(norm_group_b),
            "gc": r1(norm_group_c), "w_o": w_o_b[l], "gpm": r1(norm_post_mix), "gpf": r1(norm_pre_ffn),
            "w_ffn_in": w_fi_b[l], "w_ffn_out": w_fo_b[l], "gpo": r1(norm_post_ffn),
        }
        xp, lf, prompt_states = _layer_prompt(l, depth, xp, lw, {k: v[l] for k, v in cw_prompt.items()}, tb,
                                              prompt_states, tm_proj, tm_mix, tq_fox)
        logf_p.append(lf)
        xs, st_s = _layer_sample(l, xs, lw, {k: v[l] for k, v in cw_sample.items()}, sb, pools, page_table)
        for lst, st in zip(ss, st_s):
            lst.append(st)
    pos_major = lambda a, heads: jnp.moveaxis(a.reshape(depth, B, 2, heads, HEAD_DIM, a.shape[-1]), -1, 2)
    fox_t, cmp_t, slc_t, win_t = prompt_states
    wbp = min(WINDOW, S)
    sp = [pos_major(fox_t, A_HEADS), jnp.stack(logf_p), pos_major(cmp_t, C_KV), pos_major(slc_t, C_KV),
          pos_major(win_t[..., S - wbp:], C_KV)]
    ss = [jnp.stack(st) for st in ss]
    ss[4] = jnp.concatenate([state_win_kv[:, :, 1:], ss[4]], axis=2)
    return tuple([xp, xs] + sp + ss)
```

```python
import functools
import math

import numpy as np
import jax
import jax.numpy as jnp
from jax import lax
from jax.experimental import pallas as pl
from jax.experimental.pallas import tpu as pltpu

F32 = jnp.float32
BF16 = jnp.bfloat16
HIGHEST = lax.Precision.HIGHEST

D_MODEL = 1024
HEAD_DIM = 64
A_HEADS = 6
B_GROUPS = 4
C_HEADS = 6
C_KV = 2
C_REP = C_HEADS // C_KV
A_W = A_HEADS * HEAD_DIM
B_W = B_GROUPS * HEAD_DIM
C_W = C_HEADS * HEAD_DIM
KV_W = C_KV * HEAD_DIM
Q_BLOCK = 128
CHUNK = 128
CMP_LEN = 32
CMP_STRIDE = 16
CMP_HIDDEN = 256
SLC_BLOCK = 64
SLC_TOPN = 16
WINDOW = 512
NUM_BUCKETS = 32
MAX_DISTANCE = 128
D_FF = 2816
EPS = 1e-6
NEG = -1e30
M_INIT = 0.5 * NEG
FORCE = 1e4
LOG2E = math.log2(math.e)
SCALE = HEAD_DIM ** -0.5 * LOG2E
PROJ_WIDTHS = (A_W, A_W, A_W, A_HEADS, B_W, B_W, C_W, KV_W, KV_W, KV_W, KV_W, KV_W, KV_W, 3 * C_HEADS)

LANES = 128
SUBLANES = 8
HEAD_PAD = 8
VMEM_LIMIT = 56 * 1024 * 1024
PAGES_PER_STEP = 16
CMP_PAGES_PER_STEP = 32

P_QA = (0, 384)
P_KVA = (384, 1152)
P_BU = (1152, 1408)
P_BV = (1408, 1664)
P_QC = (1664, 2048)
P_CMP = (2048, 2304)
P_SLC = (2304, 2560)
P_WIN = (2560, 2816)
P_MISC = (2816, 2944)
N_PROJ = 2944
GATE_OFF = A_HEADS


def _dot(a, b):
    return jnp.dot(a, b, preferred_element_type=F32)


def _dot_exact(a, b):
    return jnp.dot(a, b, preferred_element_type=F32, precision=HIGHEST)


def _dot_nt(a, b, precision=None):
    return lax.dot_general(a, b, (((1,), (1,)), ((), ())), preferred_element_type=F32, precision=precision)


def _gelu(x):
    return 0.5 * x * (1.0 + jnp.tanh(math.sqrt(2.0 / math.pi) * (x + 0.044715 * (x * x * x))))


def _sigmoid(x):
    return 1.0 / (1.0 + jnp.exp(-x))


def _rms(x, g):
    return x * lax.rsqrt(jnp.mean(x * x, axis=-1, keepdims=True) + EPS) * g


def _iota(shape, dim):
    return lax.broadcasted_iota(jnp.int32, shape, dim)


def _shr(x, n):
    return jnp.right_shift(x, int(math.log2(n)))


def _cparams(sem):
    return pltpu.CompilerParams(dimension_semantics=sem, vmem_limit_bytes=VMEM_LIMIT)


def _const_spec(shape):
    nd = len(shape)
    return pl.BlockSpec(shape, lambda *_: (0,) * nd)


def _proj_kernel(x_ref, g_ref, w_ref, bf_ref, lng_ref, lnb_ref,
                 qa_ref, kvab_ref, bu_ref, vn_ref, qc_ref, cmp_ref, kvcb_ref, misc_ref,
                 kva_ref=None, slc_ref=None, win_ref=None):
    h = _rms(x_ref[...], g_ref[...]).astype(BF16)

    def mm(seg):
        return _dot(h, w_ref[:, seg[0]:seg[1]])

    qa_ref[...] = (mm(P_QA) * SCALE).astype(BF16)
    kva = mm(P_KVA)
    if kva_ref is not None:
        kva_ref[...] = kva
    kvab_ref[...] = kva.astype(BF16)
    bu_ref[...] = _gelu(mm(P_BU))
    v = _gelu(mm(P_BV))
    mu = jnp.mean(v, axis=-1, keepdims=True)
    var = jnp.mean(jnp.square(v - mu), axis=-1, keepdims=True)
    vn_ref[...] = (v - mu) * lax.rsqrt(var + EPS) * lng_ref[...] + lnb_ref[...]
    qc_ref[...] = (mm(P_QC) * SCALE).astype(BF16)
    c = mm(P_CMP)
    cmp_ref[...] = c
    kvcb_ref[:, 0:256] = c.astype(BF16)
    c = mm(P_SLC)
    if slc_ref is not None:
        slc_ref[...] = c
    kvcb_ref[:, 256:512] = c.astype(BF16)
    c = mm(P_WIN)
    if win_ref is not None:
        win_ref[...] = c
    kvcb_ref[:, 512:768] = c.astype(BF16)
    m = mm(P_MISC) + bf_ref[...]
    lane = _iota(m.shape, 1)
    logsig = jnp.minimum(m, 0.0) - jnp.log1p(jnp.exp(-jnp.abs(m)))
    misc_ref[...] = jnp.where(lane < A_HEADS, logsig, _sigmoid(m))


def _proj(x2d, g, w, bf, lng, lnb, tm, state_rows):
    R = x2d.shape[0]
    row = lambda n: pl.BlockSpec((tm, n), lambda i: (i, 0))
    outs = [("qa", 384, BF16), ("kvab", 768, BF16), ("bu", 256, F32), ("vn", 256, F32), ("qc", 384, BF16),
            ("cmp", 256, F32), ("kvcb", 768, BF16), ("misc", 128, F32)]
    if state_rows:
        outs += [("kva", 768, F32), ("slc", 256, F32), ("win", 256, F32)]
    res = pl.pallas_call(
        _proj_kernel,
        grid=(R // tm,),
        in_specs=[row(D_MODEL), _const_spec((1, D_MODEL)), _const_spec((D_MODEL, N_PROJ)),
                  _const_spec((1, 128)), _const_spec((1, B_W)), _const_spec((1, B_W))],
        out_specs=[row(n) for _, n, _ in outs],
        out_shape=[jax.ShapeDtypeStruct((R, n), dt) for _, n, dt in outs],
        compiler_params=_cparams(("parallel",)),
        name="proj",
    )(x2d, g, w, bf, lng, lnb)
    return {name: r for (name, _, _), r in zip(outs, res)}


STATE_FEATS = (2 * A_W, 2 * KV_W, 2 * KV_W, 2 * KV_W)


def _state_kernel(x_ref, g_ref, wt_ref, *refs):
    outs = refs[-len(STATE_FEATS):]
    h = _rms(x_ref[...], g_ref[...]).astype(BF16)
    lo = 0
    for o_ref, n in zip(outs, STATE_FEATS):
        o_ref[...] = _dot_nt(wt_ref[lo:lo + n, :], h)
        lo += n


def _state_proj(layer, x2d, g, wt, prev, depth, B, S, tm):
    nblk = S // tm
    shapes = [jax.ShapeDtypeStruct((depth, B, n, S), F32) for n in STATE_FEATS]
    out_specs = [pl.BlockSpec((None, None, n, tm), lambda i: (layer, i // nblk, 0, i % nblk)) for n in STATE_FEATS]
    in_specs = [pl.BlockSpec((tm, D_MODEL), lambda i: (i, 0)), _const_spec((1, D_MODEL)), _const_spec(wt.shape)]
    args = [x2d, g, wt]
    aliases = {}
    if prev is not None:
        in_specs += [pl.BlockSpec(memory_space=pl.ANY)] * len(prev)
        aliases = {len(args) + k: k for k in range(len(prev))}
        args += list(prev)
    return pl.pallas_call(
        _state_kernel,
        grid=(B * nblk,),
        in_specs=in_specs,
        out_specs=out_specs,
        out_shape=shapes,
        input_output_aliases=aliases,
        compiler_params=_cparams(("arbitrary",)),
        name="state_proj",
    )(*args)


def _fox_kernel(q_ref, k_ref, v_ref, c_ref, o_ref, *, tq):
    qi = pl.program_id(2)
    tk = tq
    q = q_ref[...]
    lane = _iota((1, LANES), 1)
    low = lane < HEAD_DIM
    zero = jnp.zeros_like(q)
    qm = (jnp.where(low, q, zero), jnp.where(low, zero, q))
    one_lane = (HEAD_DIM, 0)

    def v_aug(v, hh):
        mine = low if hh == 0 else jnp.logical_not(low)
        ones = jnp.where(lane == one_lane[hh], 1.0, 0.0).astype(BF16)
        return jnp.where(mine, v, jnp.broadcast_to(ones, v.shape))

    def tile(ki, carry, causal):
        base = pl.multiple_of(ki * tk, tk)
        k = k_ref[pl.ds(base, tk), :]
        v = v_ref[pl.ds(base, tk), :]
        out = []
        for hh in range(2):
            m, acc = carry[hh]
            s = _dot_nt(qm[hh], k) - c_ref[hh:hh + 1, pl.ds(base, tk)]
            if causal is not None:
                s = jnp.where(causal, s, NEG)
            m_new = jnp.maximum(m, jnp.max(s, axis=-1, keepdims=True))
            alpha = jnp.exp2(m - m_new)
            p = jnp.exp2(s - m_new)
            acc = alpha * acc + _dot(p.astype(BF16), v_aug(v, hh))
            out.append((m_new, acc))
        return tuple(out)

    init = tuple((jnp.full((tq, 1), M_INIT, F32), jnp.zeros((tq, LANES), F32)) for _ in range(2))
    carry = lax.fori_loop(0, qi, lambda ki, c: tile(ki, c, None), init)
    causal = _iota((1, tk), 1) <= _iota((tq, 1), 0)
    (_, acc0), (_, acc1) = tile(qi, carry, causal)
    l0 = acc0[:, one_lane[0]:one_lane[0] + 1]
    l1 = acc1[:, one_lane[1]:one_lane[1] + 1]
    o_ref[...] = jnp.where(low, acc0 * (1.0 / l0), acc1 * (1.0 / l1))


def _fox_prompt(qa, kvab, crow, tq):
    B, S, _ = qa.shape
    half = A_HEADS // 2
    return pl.pallas_call(
        functools.partial(_fox_kernel, tq=tq),
        grid=(B, half, S // tq),
        in_specs=[
            pl.BlockSpec((None, tq, LANES), lambda b, p, qi: (b, qi, p)),
            pl.BlockSpec((None, S, LANES), lambda b, p, qi: (b, 0, p)),
            pl.BlockSpec((None, S, LANES), lambda b, p, qi: (b, 0, half + p)),
            pl.BlockSpec((None, None, 2, S), lambda b, p, qi: (b, p, 0, 0)),
        ],
        out_specs=pl.BlockSpec((None, tq, LANES), lambda b, p, qi: (b, qi, p)),
        out_shape=jax.ShapeDtypeStruct((B, S, A_W), F32),
        compiler_params=_cparams(("parallel", "parallel", "arbitrary")),
        name="fox_prompt",
    )(qa, kvab, kvab, crow)


def _cumsum_kernel(x_ref, o_ref):
    upper = jnp.where(_iota((LANES, LANES), 0) <= _iota((LANES, LANES), 1), 1.0, 0.0)
    carry = jnp.zeros((x_ref.shape[0], 1), F32)
    for i in range(x_ref.shape[1] // LANES):
        lanes = slice(i * LANES, (i + 1) * LANES)
        c = _dot_exact(x_ref[:, lanes], upper) + carry
        o_ref[:, lanes] = c * LOG2E
        carry = c[:, LANES - 1:LANES]


def _cumsum_lanes(x):
    n, h, length = x.shape
    spec = pl.BlockSpec((None, h, length), lambda i: (i, 0, 0))
    return pl.pallas_call(
        _cumsum_kernel, grid=(n,), in_specs=[spec], out_specs=spec,
        out_shape=jax.ShapeDtypeStruct(x.shape, F32),
        compiler_params=_cparams(("parallel",)),
        name="cumsum_logf",
    )(x)


def _gmlp_kernel(u_ref, vn_ref, ws_ref, bst_ref, o_ref, *, nchunk):
    tril = _iota((CHUNK, CHUNK), 1) <= _iota((CHUNK, CHUNK), 0)
    lane = _iota((1, B_W), 1)
    ws = [jnp.where(tril, ws_ref[h], 0.0).astype(BF16) for h in range(B_GROUPS)]
    for c in range(nchunk):
        rows = slice(c * CHUNK, (c + 1) * CHUNK)
        vn = vn_ref[rows, :].astype(BF16)
        z = jnp.zeros((CHUNK, B_W), F32)
        for h in range(B_GROUPS):
            zh = _dot(ws[h], vn) + bst_ref[:, h:h + 1]
            z = jnp.where(_shr(lane, HEAD_DIM) == h, zh, z)
        o_ref[rows, :] = u_ref[rows, :] * z


def _gmlp(u, vn, ws, bst, nchunk):
    R = u.shape[0]
    tm = nchunk * CHUNK
    row = pl.BlockSpec((tm, B_W), lambda i: (i, 0))
    return pl.pallas_call(
        functools.partial(_gmlp_kernel, nchunk=nchunk),
        grid=(R // tm,),
        in_specs=[row, row, _const_spec((B_GROUPS, CHUNK, CHUNK)), _const_spec((CHUNK, B_GROUPS))],
        out_specs=row,
        out_shape=jax.ShapeDtypeStruct((R, B_W), F32),
        compiler_params=_cparams(("parallel",)),
        name="gmlp",
    )(u, vn, ws, bst)


def _compress_core(x, w, pe_a, pe_b, b1, w2, b2):
    nc = x.shape[0]
    half = w.shape[1] // 2
    h = _dot(x, w)
    const = _dot(pe_a, w[:, :half]) + _dot(pe_b, w[:, half:])
    second = pltpu.roll(h[:, half:], nc - 1, 0)
    hid = _gelu(h[:, :half] + second + const[0:1, :] + b1)
    return _dot(hid.astype(BF16), w2) + b2


def _cmp_prompt_kernel(krows_ref, vrows_ref, wab_ref, pe_ref, b1_ref, w2_ref, b2_ref, o_ref, *, nc):
    for kv, rows_ref in enumerate((krows_ref, vrows_ref)):
        x = jnp.concatenate(
            [rows_ref[pl.ds(l, nc, stride=CMP_STRIDE), :] for l in range(CMP_STRIDE)], axis=-1).astype(BF16)
        o_ref[:, kv * KV_W:(kv + 1) * KV_W] = _compress_core(
            x, wab_ref[kv], pe_ref[kv, 0], pe_ref[kv, 1], b1_ref[kv], w2_ref[kv], b2_ref[kv]).astype(BF16)


def _compress_prompt(cmp_rows, cw):
    B, S, _ = cmp_rows.shape
    nc = S // CMP_STRIDE
    return pl.pallas_call(
        functools.partial(_cmp_prompt_kernel, nc=nc),
        grid=(B,),
        in_specs=[pl.BlockSpec((None, S, KV_W), lambda b: (b, 0, 0)), pl.BlockSpec((None, S, KV_W), lambda b: (b, 0, 1)),
                  _const_spec(cw["wab"].shape), _const_spec(cw["pe"].shape), _const_spec(cw["b1t"].shape),
                  _const_spec(cw["w2bd"].shape), _const_spec(cw["b2t"].shape)],
        out_specs=pl.BlockSpec((None, nc, 2 * KV_W), lambda b: (b, 0, 0)),
        out_shape=jax.ShapeDtypeStruct((B, nc, 2 * KV_W), BF16),
        compiler_params=_cparams(("parallel",)),
        name="compress_prompt",
    )(cmp_rows, cmp_rows, cw["wab"], cw["pe"], cw["b1t"], cw["w2bd"], cw["b2t"])


def _rank_select_t(score_t, topn):
    n = score_t.shape[0]
    groups = [score_t[lo:lo + SUBLANES] for lo in range(0, n, SUBLANES)]
    sub = _iota((SUBLANES, 1), 0)
    ranks = [jnp.zeros(g.shape, F32) for g in groups]
    for jp in range(n):
        row = score_t[jp:jp + 1, :]
        for gi, sg in enumerate(groups):
            lo = gi * SUBLANES
            if lo + SUBLANES - 1 <= jp:
                beats = row > sg
            elif lo > jp:
                beats = row >= sg
            else:
                beats = (row > sg) | ((row == sg) & (sub + lo > jp))
            ranks[gi] = ranks[gi] + jnp.where(beats, 1.0, 0.0)
    return jnp.concatenate([jnp.where(r < topn, 1.0, 0.0) for r in ranks], axis=0)


def _attend_tiles(qg, tiles, carry):
    m, acc = carry
    s_parts = []
    for k, _, bias, mask in tiles:
        s = _dot_nt(qg, k)
        if bias is not None:
            s = s + bias
        s_parts.append(jnp.where(mask, s, NEG))
    m_new = m
    for s in s_parts:
        m_new = jnp.maximum(m_new, jnp.max(s, axis=-1, keepdims=True))
    acc = jnp.exp2(m - m_new) * acc
    for s, (_, v, _, _) in zip(s_parts, tiles):
        acc = acc + _dot(jnp.exp2(s - m_new).astype(BF16), v)
    return m_new, acc


def _nsa_kernel(q_ref, kv_ref, cmp_ref, misc_ref, tb_ref, o_ref, *, nbp, nsb):
    qi = pl.program_id(1)
    tq = Q_BLOCK
    nrow = C_REP * tq
    t0 = qi * tq
    trow = t0 + _iota((tq, 1), 0)
    trow3 = jnp.concatenate([trow] * C_REP, axis=0)
    tl3 = trow3 - t0
    sl = _iota((1, tq), 1)
    causal3 = sl <= tl3
    topn = min(SLC_TOPN, nsb)
    lane = _iota((1, LANES), 1)
    low = lane < HEAD_DIM
    per = SLC_BLOCK // CMP_STRIDE
    pool_t = jnp.where(_shr(_iota((nsb, nbp), 1), per) == _iota((nsb, nbp), 0), 1.0, 0.0)
    groups = range(C_KV)
    mine = [low, jnp.logical_not(low)]
    one_lane = [HEAD_DIM, 0]
    ones = [jnp.where(lane == one_lane[g], 1.0, 0.0).astype(BF16) for g in groups]
    ks_lo, vs_lo, kw_lo, vw_lo = 2 * KV_W, 3 * KV_W, 4 * KV_W, 5 * KV_W

    def kv_tile(base, width, k_lo, v_lo):
        k = kv_ref[pl.ds(base, width), k_lo:k_lo + KV_W]
        v = kv_ref[pl.ds(base, width), v_lo:v_lo + KV_W]
        return k, [jnp.where(mine[g], v, jnp.broadcast_to(ones[g], v.shape)) for g in groups]

    qg, o_cmp, sel = [], [], []
    for g in groups:
        parts = []
        for r in range(C_REP):
            h = C_REP * g + r
            src = q_ref[:, (h // 2) * LANES:(h // 2 + 1) * LANES].astype(F32)
            if h % 2 != g:
                src = pltpu.roll(src, HEAD_DIM, 1)
            parts.append(jnp.where(mine[g], src, 0.0))
        q = jnp.concatenate(parts, axis=0).astype(BF16)
        qg.append(q)

        s = _dot_nt(q, cmp_ref[:, 0:KV_W])
        valid = (_iota((1, nbp), 1) * CMP_STRIDE + (CMP_LEN - 1)) <= trow3
        s = jnp.where(valid, s, NEG)
        e = jnp.exp2(s - jnp.max(s, axis=-1, keepdims=True))
        prob = jnp.where(valid, e * (1.0 / jnp.sum(e, axis=-1, keepdims=True)), 0.0)
        o_cmp.append(_dot(prob.astype(BF16), cmp_ref[:, KV_W:2 * KV_W]))

        psum = prob[0:tq] + prob[tq:2 * tq] + prob[2 * tq:3 * tq]
        imp_t = _dot_nt(pool_t, psum, HIGHEST)
        j = _iota((nsb, 1), 0)
        tcol = t0 + _iota((1, tq), 1)
        cur = _shr(tcol, SLC_BLOCK)
        forced = (j == 0) | (j == cur) | (j == cur - 1)
        score_t = jnp.where(forced, FORCE, jnp.where(j * SLC_BLOCK <= tcol, imp_t, -1.0))
        sel.append(_rank_select_t(score_t, topn).T.astype(BF16))

    def sel_masks(base, width):
        blk = base // SLC_BLOCK + _shr(_iota((nsb, width), 1), SLC_BLOCK)
        expand = jnp.where(_iota((nsb, width), 0) == blk, 1.0, 0.0).astype(BF16)
        return [jnp.concatenate([_dot(sel[g], expand)] * C_REP, axis=0) > 0.5 for g in groups]

    init = (jnp.full((nrow, 1), M_INIT, F32), jnp.zeros((nrow, LANES), F32))

    def far_step(base, width, carry):
        k, v = kv_tile(base, width, ks_lo, vs_lo)
        masks = sel_masks(base, width)
        return tuple(_attend_tiles(qg[g], [(k, v[g], None, masks[g])], carry[g]) for g in groups)

    n_far = jnp.maximum(qi - 1, 0)
    carry, done = (init, init), 0
    for span in (8, 4, 2, 1):
        count = (n_far - done) // span
        carry = lax.fori_loop(
            0, count,
            lambda i, c, span=span, done=done: far_step(pl.multiple_of((done + i * span) * tq, tq), span * tq, c),
            carry)
        done = done + count * span

    pbase = pl.multiple_of(jnp.maximum(qi - 1, 0) * tq, tq)
    dbase = pl.multiple_of(t0, tq)
    has_prev = qi >= 1
    far_w = WINDOW - tq
    fbase = pl.multiple_of(jnp.maximum(t0 - WINDOW, 0), tq)
    kpos = fbase + _iota((1, far_w), 1)
    far_mask = (kpos < t0 - tq) & (trow3 - kpos <= WINDOW)
    mask_p = sel_masks(pbase, tq)
    mask_d = sel_masks(dbase, tq)
    ksp, vsp = kv_tile(pbase, tq, ks_lo, vs_lo)
    ksd, vsd = kv_tile(dbase, tq, ks_lo, vs_lo)
    kwf, vwf = kv_tile(fbase, far_w, kw_lo, vw_lo)
    kwp, vwp = kv_tile(pbase, tq, kw_lo, vw_lo)
    kwd, vwd = kv_tile(dbase, tq, kw_lo, vw_lo)
    heads = [None] * C_HEADS
    for g in groups:
        bias_diag = tb_ref[g, 0]
        bias_prev = tb_ref[g, 1]
        _, acc = _attend_tiles(qg[g], [(ksp, vsp[g], bias_prev, mask_p[g] & has_prev),
                                       (ksd, vsd[g], bias_diag, mask_d[g] & causal3)], carry[g])
        o_slc = acc * (1.0 / acc[:, one_lane[g]:one_lane[g] + 1])
        cw = _attend_tiles(qg[g], [(kwf, vwf[g], None, far_mask)], init)
        _, acc = _attend_tiles(qg[g], [(kwp, vwp[g], bias_prev, (tl3 >= 0) & has_prev),
                                       (kwd, vwd[g], bias_diag, causal3)], cw)
        o_win = acc * (1.0 / acc[:, one_lane[g]:one_lane[g] + 1])
        for r in range(C_REP):
            h = C_REP * g + r
            rows = slice(r * tq, (r + 1) * tq)
            gate = lambda br: misc_ref[:, GATE_OFF + br * C_HEADS + h:GATE_OFF + br * C_HEADS + h + 1]
            heads[h] = gate(0) * o_cmp[g][rows] + gate(1) * o_slc[rows] + gate(2) * o_win[rows]

    for pair in range(C_HEADS // 2):
        halves = []
        for h in (2 * pair, 2 * pair + 1):
            x = heads[h]
            if h // C_REP != h % 2:
                x = pltpu.roll(x, HEAD_DIM, 1)
            halves.append(x)
        o_ref[:, pair * LANES:(pair + 1) * LANES] = jnp.where(low, halves[0], halves[1])


def _nsa_prompt(qc, kvcb, kcv, misc, tb):
    B, S, _ = qc.shape
    nbp = kcv.shape[1]
    nsb = nbp * CMP_STRIDE // SLC_BLOCK
    return pl.pallas_call(
        functools.partial(_nsa_kernel, nbp=nbp, nsb=nsb),
        grid=(B, S // Q_BLOCK),
        in_specs=[pl.BlockSpec((None, Q_BLOCK, C_W), lambda b, i: (b, i, 0)),
                  pl.BlockSpec((None, S, 6 * KV_W), lambda b, i: (b, 0, 0)),
                  pl.BlockSpec((None, nbp, 2 * KV_W), lambda b, i: (b, 0, 0)),
                  pl.BlockSpec((None, Q_BLOCK, 128), lambda b, i: (b, i, 0)),
                  _const_spec(tb.shape)],
        out_specs=pl.BlockSpec((None, Q_BLOCK, C_W), lambda b, i: (b, i, 0)),
        out_shape=jax.ShapeDtypeStruct((B, S, C_W), F32),
        compiler_params=_cparams(("parallel", "arbitrary")),
        name="nsa_prompt",
    )(qc, kvcb, kcv, misc, tb)


def _mix_ffn_kernel(a_ref, b_ref, c_ref, x_ref, ga_ref, gb_ref, gc_ref, wo_ref, gpm_ref, gpf_ref,
                    win_ref, wout_ref, gpo_ref, o_ref, *, nsplit):
    an = _rms(a_ref[...], ga_ref[...]).astype(BF16)
    bn = _rms(b_ref[...], gb_ref[...]).astype(BF16)
    cn = _rms(c_ref[...], gc_ref[...]).astype(BF16)
    mix = (_dot(an, wo_ref[0:A_W, :]) + _dot(bn, wo_ref[A_W:A_W + B_W, :])
           + _dot(cn, wo_ref[A_W + B_W:A_W + B_W + C_W, :]))
    x1 = x_ref[...] + _rms(mix, gpm_ref[...])
    h = _rms(x1, gpf_ref[...]).astype(BF16)
    wid = D_FF // nsplit
    y = jnp.zeros(x1.shape, F32)
    for c in range(nsplit):
        lo = c * wid
        gate = _dot(h, win_ref[:, lo:lo + wid])
        up = _dot(h, win_ref[:, D_FF + lo:D_FF + lo + wid])
        act = gate * _sigmoid(gate) * up
        y = y + _dot(act.astype(BF16), wout_ref[lo:lo + wid, :])
    o_ref[...] = x1 + _rms(y, gpo_ref[...])


def _mix_ffn(a, b, c, x, lw, tm):
    R = x.shape[0]
    row = lambda n: pl.BlockSpec((tm, n), lambda i: (i, 0))
    once = lambda shape: pl.BlockSpec(shape, lambda i: (0, 0), pipeline_mode=pl.Buffered(1))
    return pl.pallas_call(
        functools.partial(_mix_ffn_kernel, nsplit=2),
        grid=(R // tm,),
        in_specs=[row(A_W), row(B_W), row(C_W), row(D_MODEL),
                  once((1, A_W)), once((1, B_W)), once((1, C_W)), once((D_MODEL, D_MODEL)),
                  once((1, D_MODEL)), once((1, D_MODEL)), once((D_MODEL, 2 * D_FF)), once((D_FF, D_MODEL)),
                  once((1, D_MODEL))],
        out_specs=row(D_MODEL),
        out_shape=jax.ShapeDtypeStruct((R, D_MODEL), F32),
        compiler_params=_cparams(("parallel",)),
        name="mix_ffn",
    )(a, b, c, x, lw["ga"], lw["gb"], lw["gc"], lw["w_o"], lw["gpm"], lw["gpf"], lw["w_ffn_in"],
      lw["w_ffn_out"], lw["gpo"])


def _head_block(x, h):
    return x[h * HEAD_DIM:(h + 1) * HEAD_DIM]


def _pad_heads(x):
    return jnp.concatenate([x, jnp.zeros((HEAD_PAD - x.shape[0], x.shape[1]), F32)], axis=0)


def _row_softmax_step(s, m_prev):
    m_new = jnp.maximum(m_prev, jnp.max(s, axis=1, keepdims=True))
    return m_new, jnp.exp2(m_prev - m_new), jnp.exp2(s - m_new)


def _fox_dec_kernel(pt_ref, q_ref, *refs, npp):
    pages = refs[:npp]
    lfs = refs[npp:2 * npp]
    new_ref, newlf_ref, o_ref, m_sc, l_sc, c_sc, acc_sc = refs[2 * npp:]
    j = pl.program_id(1)
    page = pages[0].shape[1]
    q = q_ref[...]
    upper = jnp.where(_iota((page, page), 0) <= _iota((page, page), 1), 1.0, 0.0)

    @pl.when(j == 0)
    def _():
        m_sc[...] = jnp.full(m_sc.shape, M_INIT, F32)
        l_sc[...] = jnp.zeros(l_sc.shape, F32)
        c_sc[...] = jnp.zeros(c_sc.shape, F32)
        acc_sc[...] = jnp.zeros(acc_sc.shape, F32)

    def scores(kt):
        prod = kt * q
        return _pad_heads(jnp.concatenate(
            [jnp.sum(_head_block(prod, h), axis=0, keepdims=True) for h in range(A_HEADS)], axis=0))

    def accumulate(alpha, p_tiles, v_tiles):
        for h in range(A_HEADS):
            a = _head_block(acc_sc, h) * alpha[h:h + 1, :]
            for p, v in zip(p_tiles, v_tiles):
                a = a + p[h:h + 1, :] * v[A_W + h * HEAD_DIM:A_W + (h + 1) * HEAD_DIM, :]
            acc_sc[h * HEAD_DIM:(h + 1) * HEAD_DIM, :] = a

    c_local = _dot_exact(jnp.concatenate([lf[...] for lf in lfs], axis=0), upper)
    totals = [jnp.broadcast_to(c_local[i * HEAD_PAD:(i + 1) * HEAD_PAD, page - 1:page], (HEAD_PAD, page))
              for i in range(npp)]
    before = jnp.broadcast_to(c_sc[...], (HEAD_PAD, page))
    s_parts = []
    for i, pg in enumerate(pages):
        s_parts.append(scores(pg[0:A_W, :]) - LOG2E * (c_local[i * HEAD_PAD:(i + 1) * HEAD_PAD] + before))
        before = before + totals[i]
    c_sc[...] = before[:, 0:1]
    m_new, alpha, p = _row_softmax_step(jnp.concatenate(s_parts, axis=1), m_sc[...])
    l_sc[...] = alpha * l_sc[...] + jnp.sum(p, axis=1, keepdims=True)
    m_sc[...] = m_new
    accumulate(alpha, [p[:, i * page:(i + 1) * page] for i in range(npp)], pages)

    @pl.when(j == pl.num_programs(1) - 1)
    def _():
        first = _iota((1, page), 1) == 0
        s_new = jnp.where(first, scores(new_ref[0:A_W, :]) - LOG2E * (c_sc[...] + newlf_ref[...]), NEG)
        m_new, alpha, p_new = _row_softmax_step(s_new, m_sc[...])
        l_inv = 1.0 / (alpha * l_sc[...] + jnp.sum(p_new, axis=1, keepdims=True))
        accumulate(alpha, [p_new], [new_ref])
        for h in range(A_HEADS):
            rows = slice(h * HEAD_DIM, (h + 1) * HEAD_DIM)
            o_ref[rows, :] = jnp.sum(acc_sc[rows, :], axis=1, keepdims=True) * l_inv[h:h + 1, :]


def _fox_decode(layer, page_table, q_rep, pool_t, logf_t, new_t, new_lf, npp):
    nb, npages = page_table.shape
    page = pool_t.shape[3]
    page_of = lambda b, j, i, pt: pt[b, j * npp + i]
    page_spec = lambda i: pl.BlockSpec((None, None, 2 * A_W, page), lambda b, j, pt: (layer, page_of(b, j, i, pt), 0, 0))
    logf_spec = lambda i: pl.BlockSpec((None, None, HEAD_PAD, page), lambda b, j, pt: (layer, page_of(b, j, i, pt), 0, 0))
    per_b = lambda shape: pl.BlockSpec((None,) + shape, lambda b, j, pt: (b,) + (0,) * len(shape))
    grid_spec = pltpu.PrefetchScalarGridSpec(
        num_scalar_prefetch=1,
        grid=(nb, npages // npp),
        in_specs=[per_b((A_W, page))] + [page_spec(i) for i in range(npp)] + [logf_spec(i) for i in range(npp)]
        + [per_b((2 * A_W, page)), per_b((HEAD_PAD, 1))],
        out_specs=per_b((A_W, 1)),
        scratch_shapes=[pltpu.VMEM((HEAD_PAD, 1), F32), pltpu.VMEM((HEAD_PAD, 1), F32), pltpu.VMEM((HEAD_PAD, 1), F32),
                        pltpu.VMEM((A_W, page), F32)],
    )
    return pl.pallas_call(
        functools.partial(_fox_dec_kernel, npp=npp),
        grid_spec=grid_spec,
        out_shape=jax.ShapeDtypeStruct((nb, A_W, 1), F32),
        compiler_params=_cparams(("parallel", "arbitrary")),
        name="fox_decode",
    )(page_table, q_rep, *([pool_t] * npp), *([logf_t] * npp), new_t, new_lf)


def _cmp_dec_kernel(pt_ref, *refs, npp, nc, nsb_all):
    pages = refs[:npp]
    w_ref, pe_ref, b1_ref, w2_ref, b2_ref, q_ref, ocmp_ref, sel_ref, x_sc = refs[npp:]
    j = pl.program_id(1)
    per_page = pages[0].shape[1]
    for i, pg in enumerate(pages):
        row0 = pl.multiple_of((j * npp + i) * per_page, per_page)
        for c in range(2 * C_KV):
            x_sc[c, pl.ds(row0, per_page), :] = pg[c]

    @pl.when(j == pl.num_programs(1) - 1)
    def _():
        kc = [[_compress_core(x_sc[kv * C_KV + g].astype(BF16), w_ref[kv], pe_ref[kv, 0], pe_ref[kv, 1],
                              b1_ref[kv], w2_ref[kv], b2_ref[kv]).astype(BF16)
               for g in range(C_KV)] for kv in range(2)]
        q = q_ref[...]
        grp0 = _iota((HEAD_PAD, 1), 0) < C_REP
        pick = lambda x0, x1: jnp.where(grp0, x0, x1)
        s = pick(_dot_nt(q, kc[0][0]), _dot_nt(q, kc[0][1]))
        valid = _iota((1, nc), 1) < nc - 1
        s = jnp.where(valid, s, NEG)
        e = jnp.exp2(s - jnp.max(s, axis=-1, keepdims=True))
        prob = jnp.where(valid, e / jnp.sum(e, axis=-1, keepdims=True), 0.0)
        pb = prob.astype(BF16)
        ocmp_ref[...] = pick(_dot(pb, kc[1][0]), _dot(pb, kc[1][1]))

        per = SLC_BLOCK // CMP_STRIDE
        nsbp = sel_ref.shape[1]
        hrow = _iota((HEAD_PAD, 1), 0)
        psum = jnp.concatenate(
            [jnp.sum(jnp.where((hrow >= C_REP * g) & (hrow < C_REP * (g + 1)), prob, 0.0), axis=0, keepdims=True)
             for g in range(C_KV)] + [jnp.zeros((HEAD_PAD - C_KV, nc), F32)], axis=0)
        pool = jnp.where(_shr(_iota((nc, nsbp), 0), per) == _iota((nc, nsbp), 1), 1.0, 0.0)
        imp = _dot_exact(psum, pool)
        jl = _iota((1, nsbp), 1)
        cur = nsb_all - 1
        forced = (jl == 0) | (jl == cur) | (jl == cur - 1)
        score = jnp.where(forced, FORCE, jnp.where(jl <= cur, imp, -2.0))
        topn = min(SLC_TOPN, nsb_all)
        rank = jnp.zeros(score.shape, F32)
        for jp in range(nsb_all):
            col = score[:, jp:jp + 1]
            beats = (col > score) | ((col == score) & (jp < jl))
            rank = rank + jnp.where(beats, 1.0, 0.0)
        sel_ref[...] = jnp.where(rank < topn, 1.0, 0.0)


def _cmp_decode(layer, page_table, pool, cw, qh, npp):
    nb, npages = page_table.shape
    per_page, width = pool.shape[3], pool.shape[4]
    nc = npages * per_page
    nsb_all = nc * CMP_STRIDE // SLC_BLOCK + 1
    nsbp = -(-nsb_all // LANES) * LANES

    def page_spec(i):
        return pl.BlockSpec((None, None, 2 * C_KV, per_page, width),
                            lambda b, j, pt: (layer, pt[b, j * npp + i], 0, 0, 0))

    const = lambda a: pl.BlockSpec(a.shape, lambda b, j, pt: (0,) * a.ndim)
    grid_spec = pltpu.PrefetchScalarGridSpec(
        num_scalar_prefetch=1,
        grid=(nb, npages // npp),
        in_specs=[page_spec(i) for i in range(npp)]
        + [const(cw["w"]), const(cw["pe"]), const(cw["b1"]), const(cw["w2"]), const(cw["b2"]),
           pl.BlockSpec((None, HEAD_PAD, HEAD_DIM), lambda b, j, pt: (b, 0, 0))],
        out_specs=[pl.BlockSpec((None, HEAD_PAD, HEAD_DIM), lambda b, j, pt: (b, 0, 0)),
                   pl.BlockSpec((None, HEAD_PAD, nsbp), lambda b, j, pt: (b, 0, 0))],
        scratch_shapes=[pltpu.VMEM((2 * C_KV, nc, width), F32)],
    )
    return pl.pallas_call(
        functools.partial(_cmp_dec_kernel, npp=npp, nc=nc, nsb_all=nsb_all),
        grid_spec=grid_spec,
        out_shape=[jax.ShapeDtypeStruct((nb, HEAD_PAD, HEAD_DIM), F32),
                   jax.ShapeDtypeStruct((nb, HEAD_PAD, nsbp), F32)],
        compiler_params=_cparams(("parallel", "arbitrary")),
        name="cmp_decode",
    )(page_table, *([pool] * npp), cw["w"], cw["pe"], cw["b1"], cw["w2"], cw["b2"], qh)


def _slc_dec_kernel(plist_ref, lpage_ref, cnt_ref, flag_ref, q_ref, *refs, npp, nslot, nflag, last_page):
    pages = refs[:npp]
    (nbias_ref, fbias_ref, win_ref, wbias_ref, b0_ref, new_ref, gate_ref, ocmp_ref, o_ref,
     m_sc, l_sc, acc_sc) = refs[npp:]
    b = pl.program_id(0)
    j = pl.program_id(1)
    page = pages[0].shape[1]
    q = q_ref[...]
    grp0 = _iota((HEAD_PAD, 1), 0) < C_REP
    lane = _iota((1, page), 1)

    def scores(kt):
        return _pad_heads(jnp.concatenate(
            [jnp.sum(_head_block(kt, h // C_REP) * _head_block(q, h), axis=0, keepdims=True)
             for h in range(C_HEADS)], axis=0))

    def weighted(acc, alpha, p_tiles, v_tiles):
        out = []
        for h in range(C_HEADS):
            a = _head_block(acc, h) * alpha[h:h + 1, :]
            for p, v in zip(p_tiles, v_tiles):
                a = a + p[h:h + 1, :] * _head_block(v, h // C_REP)
            out.append(a)
        return jnp.concatenate(out, axis=0)

    def finish(acc, l):
        l_inv = 1.0 / l
        return jnp.concatenate(
            [jnp.sum(_head_block(acc, h), axis=1, keepdims=True) * l_inv[h:h + 1, :] for h in range(C_HEADS)], axis=0)

    @pl.when(j == 0)
    def _():
        m_sc[...] = jnp.full(m_sc.shape, M_INIT, F32)
        l_sc[...] = jnp.zeros(l_sc.shape, F32)
        acc_sc[...] = jnp.zeros(acc_sc.shape, F32)

    nblk = page // SLC_BLOCK
    s_parts = []
    for i, pg in enumerate(pages):
        slot = j * npp + i
        lp = lpage_ref[b * nslot + slot]
        live = slot < cnt_ref[b]
        mask = jnp.zeros((HEAD_PAD, page), jnp.int32)
        for blk in range(nblk):
            f = [flag_ref[(b * C_KV + g) * nflag + lp * nblk + blk] for g in range(C_KV)]
            mask = jnp.where(_shr(lane, SLC_BLOCK) == blk, jnp.where(grp0, f[0], f[1]), mask)
        mask = (mask > 0) & live
        bias = jnp.where(lp == last_page, nbias_ref[...], fbias_ref[...])
        s_parts.append(jnp.where(mask, scores(pg[0:KV_W, :]) + bias, NEG))
    m_new, alpha, p = _row_softmax_step(jnp.concatenate(s_parts, axis=1), m_sc[...])
    l_sc[...] = alpha * l_sc[...] + jnp.sum(p, axis=1, keepdims=True)
    m_sc[...] = m_new
    acc_sc[...] = weighted(acc_sc[...], alpha, [p[:, i * page:(i + 1) * page] for i in range(npp)],
                           [pg[KV_W:2 * KV_W, :] for pg in pages])

    @pl.when(j == pl.num_programs(1) - 1)
    def _():
        first = lane == 0
        s_new = jnp.where(first, scores(new_ref[0:KV_W, :]) + b0_ref[...], NEG)
        m_new, alpha, p_new = _row_softmax_step(s_new, m_sc[...])
        l = alpha * l_sc[...] + jnp.sum(p_new, axis=1, keepdims=True)
        o_slc = finish(weighted(acc_sc[...], alpha, [p_new], [new_ref[KV_W:2 * KV_W, :]]), l)
        nwin = win_ref.shape[1] // page
        tiles = lambda ref, lo: [ref[lo:lo + KV_W, c * page:(c + 1) * page] for c in range(nwin)]
        s_win = [scores(kt) + wbias_ref[:, c * page:(c + 1) * page] for c, kt in enumerate(tiles(win_ref, 0))]
        s_win.append(jnp.where(first, scores(new_ref[2 * KV_W:3 * KV_W, :]) + b0_ref[...], NEG))
        s = jnp.concatenate(s_win, axis=1)
        p = jnp.exp2(s - jnp.max(s, axis=1, keepdims=True))
        acc = weighted(jnp.zeros((C_W, page), F32), jnp.zeros((HEAD_PAD, 1), F32),
                       [p[:, c * page:(c + 1) * page] for c in range(nwin + 1)],
                       tiles(win_ref, KV_W) + [new_ref[3 * KV_W:4 * KV_W, :]])
        o_win = finish(acc, jnp.sum(p, axis=1, keepdims=True))
        o_ref[...] = gate_ref[:, 0:1] * ocmp_ref[...] + gate_ref[:, 1:2] * o_slc + gate_ref[:, 2:3] * o_win


def _slc_decode(layer, plist, lpage, cnt, flags, nflag, q_rep, pool_t, nbias, fbias, win_t, wbias, b0, new_t, gates,
                ocmp, npp, nslot, last_page):
    nb = q_rep.shape[0]
    page = pool_t.shape[3]
    wb = win_t.shape[3]

    def page_spec(i):
        return pl.BlockSpec((None, None, 2 * KV_W, page),
                            lambda b, j, pls, lps, cn, fl: (layer, pls[b * nslot + j * npp + i], 0, 0))

    fixed = lambda shape: pl.BlockSpec(shape, lambda b, j, pls, lps, cn, fl: (0,) * len(shape))
    per_b = lambda shape: pl.BlockSpec((None,) + shape, lambda b, j, pls, lps, cn, fl: (b,) + (0,) * len(shape))
    grid_spec = pltpu.PrefetchScalarGridSpec(
        num_scalar_prefetch=4,
        grid=(nb, nslot // npp),
        in_specs=[per_b((C_W, page))] + [page_spec(i) for i in range(npp)]
        + [fixed((HEAD_PAD, page)), fixed((HEAD_PAD, 1)),
           pl.BlockSpec((None, None, 2 * KV_W, wb), lambda b, j, pls, lps, cn, fl: (layer, b, 0, 0)),
           fixed((HEAD_PAD, wb)), fixed((HEAD_PAD, 1)), per_b((4 * KV_W, page)), per_b((C_W, 3)), per_b((C_W, 1))],
        out_specs=per_b((C_W, 1)),
        scratch_shapes=[pltpu.VMEM((HEAD_PAD, 1), F32), pltpu.VMEM((HEAD_PAD, 1), F32), pltpu.VMEM((C_W, page), F32)],
    )
    return pl.pallas_call(
        functools.partial(_slc_dec_kernel, npp=npp, nslot=nslot, nflag=nflag, last_page=last_page),
        grid_spec=grid_spec,
        out_shape=jax.ShapeDtypeStruct((nb, C_W, 1), F32),
        compiler_params=_cparams(("parallel", "arbitrary")),
        name="slc_win_decode",
    )(plist, lpage, cnt, flags, q_rep, *([pool_t] * npp), nbias, fbias, win_t, wbias, b0, new_t, gates, ocmp)


def _t5_bucket(dist):
    n = jnp.maximum(dist, 0)
    max_exact = NUM_BUCKETS // 2
    nf = jnp.maximum(n, 1).astype(F32)
    large = max_exact + (jnp.log(nf / max_exact) / math.log(MAX_DISTANCE / max_exact)
                         * (NUM_BUCKETS - max_exact)).astype(jnp.int32)
    return jnp.where(n < max_exact, n, jnp.minimum(large, NUM_BUCKETS - 1))


def _prep_w_in(w_in):
    splits = [int(s) for s in np.cumsum(PROJ_WIDTHS)[:-1]]
    a_q, a_k, a_v, a_f, b_u, b_v, c_q, c_kc, c_vc, c_ks, c_vs, c_kw, c_vw, c_g = jnp.split(w_in, splits, axis=-1)
    pad = jnp.zeros(w_in.shape[:-1] + (LANES - A_HEADS - 3 * C_HEADS,), w_in.dtype)
    return jnp.concatenate([a_q, a_k, a_v, b_u, b_v, c_q, c_kc, c_vc, c_ks, c_vs, c_kw, c_vw, a_f, c_g, pad],
                           axis=-1).astype(BF16)


def _prep_w_state(w_in):
    splits = [int(s) for s in np.cumsum(PROJ_WIDTHS)[:-1]]
    _, a_k, a_v, _, _, _, _, c_kc, c_vc, c_ks, c_vs, c_kw, c_vw, _ = jnp.split(w_in, splits, axis=-1)
    return jnp.concatenate([a_k, a_v, c_kc, c_vc, c_ks, c_vs, c_kw, c_vw], axis=-1).transpose(0, 2, 1).astype(BF16)


def _prep_compress(cmp_pe, cmp_w1, cmp_b1, cmp_w2, cmp_b2):
    depth = cmp_w1.shape[0]
    half = CMP_LEN // 2
    eye = jnp.eye(C_KV, dtype=F32)
    w1 = cmp_w1.reshape(depth, 2, 2, half, HEAD_DIM, CMP_HIDDEN)
    wab = jnp.einsum("zkhldj,gG->zklgdhGj", w1, eye).reshape(depth, 2, half * KV_W, 2 * C_KV * CMP_HIDDEN)
    pe = cmp_pe.reshape(depth, 2, 2, half, 1, HEAD_DIM)
    pe_bd = jnp.broadcast_to(pe, (depth, 2, 2, half, C_KV, HEAD_DIM)).reshape(depth, 2, 2, 1, half * KV_W)
    pe_bd = jnp.broadcast_to(pe_bd, (depth, 2, 2, SUBLANES, half * KV_W))
    w2bd = jnp.einsum("zkjd,gG->zkgjGd", cmp_w2, eye).reshape(depth, 2, C_KV * CMP_HIDDEN, KV_W)
    w_g = w1.transpose(0, 1, 3, 4, 2, 5).reshape(depth, 2, half * HEAD_DIM, 2 * CMP_HIDDEN)
    pe_g = jnp.broadcast_to(cmp_pe.reshape(depth, 2, 2, 1, half * HEAD_DIM), (depth, 2, 2, SUBLANES, half * HEAD_DIM))
    prompt = {
        "wab": wab.astype(BF16), "pe": pe_bd.astype(BF16),
        "b1t": jnp.tile(cmp_b1, (1, 1, C_KV))[:, :, None, :],
        "w2bd": w2bd.astype(BF16),
        "b2t": jnp.tile(cmp_b2, (1, 1, C_KV))[:, :, None, :],
    }
    sample = {"w": w_g.astype(BF16), "pe": pe_g.astype(BF16), "b1": cmp_b1[:, :, None, :],
              "w2": cmp_w2.astype(BF16), "b2": cmp_b2[:, :, None, :]}
    return prompt, sample


def _prompt_bias_tiles(table):
    tl = jnp.arange(Q_BLOCK)[:, None]
    sl = jnp.arange(Q_BLOCK)[None, :]
    d = tl - sl
    far = table[_t5_bucket(jnp.asarray(8 * MAX_DISTANCE))]
    tiles = []
    for delta in (0, Q_BLOCK):
        b = table[_t5_bucket(d + delta)] - far
        tiles.append(b.transpose(2, 0, 1).reshape(C_KV, C_REP * Q_BLOCK, Q_BLOCK))
    return (jnp.stack(tiles, axis=1) * LOG2E).astype(F32)


def _sample_bias(table, page, wb):
    rows = lambda x: jnp.pad(x.T * LOG2E, ((0, HEAD_PAD - C_HEADS), (0, 0))).astype(F32)
    nbias = rows(table[_t5_bucket(page - jnp.arange(page))])
    fbias = rows(table[_t5_bucket(jnp.asarray([8 * MAX_DISTANCE]))])
    wbias = rows(table[_t5_bucket(wb - jnp.arange(wb))])
    b0 = rows(table[0:1])
    return nbias, fbias, wbias, b0


def _lane_repeat(x, n):
    return jnp.broadcast_to(x.astype(F32)[:, :, None], x.shape + (n,))


def _lane_zero(x, n):
    return jnp.pad(x.astype(F32)[:, :, None], ((0, 0), (0, 0), (0, n - 1)))


def _layer_prompt(layer, depth, x, lw, cw, tb, states, tm_proj, tm_mix, tq_fox):
    B, S, _ = x.shape
    R = B * S
    x2 = x.reshape(R, D_MODEL)
    p = _proj(x2, lw["g_pre"], lw["w_in"], lw["bf"], lw["lng"], lw["lnb"], tm_proj, state_rows=False)
    states = _state_proj(layer, x2, lw["g_pre"], lw["w_state"], states, depth, B, S, tm_proj)
    lf = p["misc"][:, :A_HEADS].reshape(B, S, A_HEADS)
    lf_t = jnp.pad(lf, ((0, 0), (0, 0), (0, HEAD_PAD - A_HEADS))).transpose(0, 2, 1)
    crow = _cumsum_lanes(lf_t)[:, :A_HEADS].reshape(B, A_HEADS // 2, 2, S)
    a_out = _fox_prompt(p["qa"].reshape(B, S, A_W), p["kvab"].reshape(B, S, 2 * A_W), crow, tq_fox)
    b_out = _gmlp(p["bu"], p["vn"], lw["ws"], lw["bst"], min(8, R // CHUNK))
    kcv = _compress_prompt(p["cmp"].reshape(B, S, 2 * KV_W), cw)
    c_out = _nsa_prompt(p["qc"].reshape(B, S, C_W), p["kvcb"].reshape(B, S, 6 * KV_W), kcv,
                        p["misc"].reshape(B, S, 128), tb)
    y = _mix_ffn(a_out.reshape(R, A_W), b_out, c_out.reshape(R, C_W), x2, lw, tm_mix)
    return y.reshape(B, S, D_MODEL), lf, states


def _layer_sample(layer, x, lw, cw, sb, pools, page_table):
    nb = x.shape[0]
    fox_t, logf_t, cmp_pool, slc_t, win_t = pools
    npages = page_table.shape[1]
    page = fox_t.shape[3]
    past = npages * page
    npp = min(PAGES_PER_STEP, npages)
    x2 = x.reshape(nb, D_MODEL)
    p = _proj(x2, lw["g_pre"], lw["w_in"], lw["bf"], lw["lng"], lw["lnb"], nb, state_rows=True)
    qa, kva, bu, vn, qc, misc = p["qa"], p["kva"], p["bu"], p["vn"], p["qc"], p["misc"]
    cmp_r, slc_r, win_r = p["cmp"], p["slc"], p["win"]
    lf = misc[:, :A_HEADS]

    new_lf = jnp.pad(lf, ((0, 0), (0, HEAD_PAD - A_HEADS)))[:, :, None]
    a_out = _fox_decode(layer, page_table, _lane_repeat(qa, page), fox_t, logf_t, _lane_zero(kva, page), new_lf,
                        npp).reshape(nb, A_W)

    pad_chunk = lambda t: jnp.zeros((nb, CHUNK, B_W), F32).at[:, 0].set(t).reshape(nb * CHUNK, B_W)
    b_out = _gmlp(pad_chunk(bu), pad_chunk(vn), lw["ws"], lw["bst"], min(8, nb)).reshape(nb, CHUNK, B_W)[:, 0]

    qh = jnp.pad(qc.reshape(nb, C_HEADS, HEAD_DIM), ((0, 0), (0, HEAD_PAD - C_HEADS), (0, 0)))
    o_cmp, sel = _cmp_decode(layer, page_table, cmp_pool, cw, qh, min(CMP_PAGES_PER_STEP, npages))
    nsb_all = past // SLC_BLOCK + 1
    per_page = page // SLC_BLOCK
    nflag = (npages + 1) * per_page
    flags = jnp.pad(sel[:, :C_KV, :nsb_all] > 0.5, ((0, 0), (0, 0), (0, nflag - nsb_all)))
    need = flags[:, :, :npages * per_page].reshape(nb, C_KV, npages, per_page).any(axis=(1, 3))
    nslot = -(-min(npages, C_KV * SLC_TOPN) // npp) * npp
    pidx = jnp.arange(npages, dtype=jnp.int32)[None, :]
    order = jnp.argsort(jnp.where(need, pidx, pidx + npages), axis=1)[:, :nslot]
    cnt = jnp.sum(need, axis=1).astype(jnp.int32)
    last = jnp.take_along_axis(order, jnp.maximum(cnt - 1, 0)[:, None], axis=1)
    lpage = jnp.where(jnp.arange(nslot)[None, :] < cnt[:, None], order, last).astype(jnp.int32)
    plist = jnp.take_along_axis(page_table, lpage, axis=1)
    g3 = misc[:, GATE_OFF:GATE_OFF + 3 * C_HEADS].reshape(nb, 3, C_HEADS).transpose(0, 2, 1)
    gates = jnp.repeat(g3, HEAD_DIM, axis=1)
    nbias, fbias, wbias, b0 = sb
    new_t = _lane_zero(jnp.concatenate([slc_r, win_r], axis=1), page)
    c_out = _slc_decode(layer, plist.reshape(-1), lpage.reshape(-1), cnt, flags.reshape(-1).astype(jnp.int32), nflag,
                        _lane_repeat(qc, page), slc_t, nbias, fbias, win_t, wbias, b0, new_t, gates,
                        o_cmp[:, :C_HEADS].reshape(nb, C_W, 1), npp, nslot, npages - 1).reshape(nb, C_W)

    y = _mix_ffn(a_out, b_out, c_out, x2, lw, nb)
    kv5 = lambda r: r.reshape(nb, 1, 2, C_KV, HEAD_DIM)
    states = (kva.reshape(nb, 1, 2, A_HEADS, HEAD_DIM), lf[:, None, :], kv5(cmp_r), kv5(slc_r), kv5(win_r),
              vn.reshape(nb, 1, B_GROUPS, HEAD_DIM))
    return y.reshape(nb, 1, D_MODEL), states


def kernel(x_prompt, x_sample, cache_fox_kv, cache_fox_logf, cache_cmp_kv, cache_slc_kv, state_win_kv, page_table, rel_bias_table, norm_pre_mix, w_in, b_forget, gmlp_ln_g, gmlp_ln_b, gmlp_ws, gmlp_bs, cmp_pe, cmp_w1, cmp_b1, cmp_w2, cmp_b2, norm_group_a, norm_group_b, norm_group_c, w_o, norm_post_mix, norm_pre_ffn, w_ffn_in, w_ffn_out, norm_post_ffn):
    depth = w_in.shape[0]
    assert x_sample.shape[1] == 1
    B, S, _ = x_prompt.shape
    nb = x_sample.shape[0]
    n_pool, page = cache_fox_kv.shape[1], cache_fox_kv.shape[2]
    wb = state_win_kv.shape[2]

    w_in_p = _prep_w_in(w_in)
    bf = jnp.pad(b_forget, ((0, 0), (0, LANES - A_HEADS)))[:, None, :]
    cw_prompt, cw_sample = _prep_compress(cmp_pe, cmp_w1, cmp_b1, cmp_w2, cmp_b2)
    tb = _prompt_bias_tiles(rel_bias_table)
    sb = _sample_bias(rel_bias_table, page, wb)
    w_o_b = w_o.astype(BF16)
    w_fi_b = w_ffn_in.astype(BF16)
    w_fo_b = w_ffn_out.astype(BF16)
    bst = gmlp_bs.transpose(0, 2, 1)
    feat_pos = lambda a: jnp.moveaxis(a, 2, -1).reshape(a.shape[0], a.shape[1], -1, a.shape[2])
    chunks = page // CMP_STRIDE
    cmp_x = cache_cmp_kv.reshape(depth, n_pool, chunks, CMP_STRIDE, 2, C_KV, HEAD_DIM)
    cmp_x = cmp_x.transpose(0, 1, 4, 5, 2, 3, 6).reshape(depth, n_pool, 2 * C_KV, chunks, CMP_STRIDE * HEAD_DIM)
    logf_t = jnp.pad(cache_fox_logf.transpose(0, 1, 3, 2), ((0, 0), (0, 0), (0, HEAD_PAD - A_HEADS), (0, 0)))
    pools = (feat_pos(cache_fox_kv), logf_t, cmp_x, feat_pos(cache_slc_kv), feat_pos(state_win_kv))

    tm_proj = min(512, B * S)
    tm_mix = min(256, B * S)
    tq_fox = min(512, S)

    w_state = _prep_w_state(w_in)
    xp, xs = x_prompt, x_sample
    prompt_states = None
    logf_p = []
    ss = [[] for _ in range(6)]
    for l in range(depth):
        r1 = lambda a: a[l][None, :]
        lw = {
            "g_pre": r1(norm_pre_mix), "w_in": w_in_p[l], "w_state": w_state[l], "bf": bf[l], "lng": r1(gmlp_ln_g),
            "lnb": r1(gmlp_ln_b), "ws": gmlp_ws[l], "bst": bst[l], "ga": r1(norm_group_a), "gb": r1(norm_group_b),
            "gc": r1(norm_group_c), "w_o": w_o_b[l], "gpm": r1(norm_post_mix), "gpf": r1(norm_pre_ffn),
            "w_ffn_in": w_fi_b[l], "w_ffn_out": w_fo_b[l], "gpo": r1(norm_post_ffn),
        }
        xp, lf, prompt_states = _layer_prompt(l, depth, xp, lw, {k: v[l] for k, v in cw_prompt.items()}, tb,
                                              prompt_states, tm_proj, tm_mix, tq_fox)
        logf_p.append(lf)
        xs, st_s = _layer_sample(l, xs, lw, {k: v[l] for k, v in cw_sample.items()}, sb, pools, page_table)
        for lst, st in zip(ss, st_s):
            lst.append(st)
    pos_major = lambda a, heads: jnp.moveaxis(a.reshape(depth, B, 2, heads, HEAD_DIM, a.shape[-1]), -1, 2)
    fox_t, cmp_t, slc_t, win_t = prompt_states
    wbp = min(WINDOW, S)
    sp = [pos_major(fox_t, A_HEADS), jnp.stack(logf_p), pos_major(cmp_t, C_KV), pos_major(slc_t, C_KV),
          pos_major(win_t[..., S - wbp:], C_KV)]
    ss = [jnp.stack(st) for st in ss]
    ss[4] = jnp.concatenate([state_win_kv[:, :, 1:], ss[4]], axis=2)
    return tuple([xp, xs] + sp + ss)
```
